```python
import math
import jax, jax.numpy as jnp
from jax import lax
import numpy as np

D_MODEL = 1024
BATCH = 8
SEQ = 2048
DEPTH = 1
DEC_BATCH = 128
DEC_SEQ = 8
PAST_LEN = 16384
PAGE_SIZE = 128

D_MIX = D_MODEL
D_CONV = D_MIX // 2
D_MLSTM = D_MIX - D_CONV
N_MHEADS = 4
HEAD_DIM = D_MLSTM // N_MHEADS
CONV_WIDTH = 31
MLSTM_CHUNK = 64
D_FF = ((8 * D_MODEL // 3 + 127) // 128) * 128
N_ADA = 9
EPS = 1e-6
RES_HALF = 0.5
D_IN = 2 * D_CONV + 4 * D_MLSTM + 2 * N_MHEADS

kernel_name = 'hybrid_conv_mlstm_macaron_adaln_step'


def _rmsnorm(x, g):
    xf = x.astype(jnp.float32)
    xf = xf * lax.rsqrt(jnp.mean(xf * xf, axis=-1, keepdims=True) + EPS)
    return xf.astype(x.dtype) * g


def _layernorm(x, g, b):
    xf = x.astype(jnp.float32)
    xc = xf - jnp.mean(xf, axis=-1, keepdims=True)
    xf = xc * lax.rsqrt(jnp.mean(xc * xc, axis=-1, keepdims=True) + EPS)
    return xf.astype(x.dtype) * g + b


def _modulate(xn, shift, scale):
    return xn * (1.0 + scale) + shift


def _swiglu(x, w_gu, w_down):
    gate, up = jnp.split(x @ w_gu, 2, axis=-1)
    return (jax.nn.silu(gate) * up) @ w_down


def _causal_dwconv(u, buf, w, b):
    full = jnp.concatenate([buf.astype(u.dtype), u], axis=1)
    y = lax.conv_general_dilated(full, w[:, None, :].astype(u.dtype), (1,), 'VALID',
                                 dimension_numbers=('NWC', 'WIO', 'NWC'),
                                 feature_group_count=u.shape[-1])
    return y + b, full[:, full.shape[1] - (CONV_WIDTH - 1):]


def _mlstm(q, k, v, i_pre, f_pre, C0, n0, m0):
    B, T, H, DH = q.shape
    L = math.gcd(T, MLSTM_CHUNK)
    NC = T // L

    def to_chunks(a):
        return a.reshape((B, NC, L) + a.shape[2:]).swapaxes(0, 1)

    logf = jax.nn.log_sigmoid(f_pre)
    mask = jnp.tril(jnp.ones((L, L), dtype=bool))

    def step(carry, xs):
        C, n, m = carry
        qc, kc, vc, ic, lfc = xs
        cum = lax.cumsum(lfc, axis=1).transpose(0, 2, 1)
        ig = ic.transpose(0, 2, 1)
        dmat = cum[:, :, :, None] - cum[:, :, None, :] + ig[:, :, None, :]
        dmat = jnp.where(mask, dmat, -jnp.inf)
        inter = cum + m[:, :, None]
        m_j = jnp.maximum(inter, jnp.max(dmat, axis=-1))
        w_intra = jnp.exp(dmat - m_j[..., None])
        w_inter = jnp.exp(inter - m_j)
        s = jnp.einsum('blhd,bshd->bhls', qc, kc) * w_intra
        num = (jnp.einsum('bhls,bshe->bhle', s, vc)
               + w_inter[..., None] * jnp.einsum('blhd,bhde->bhle', qc, C))
        den = jnp.sum(s, axis=-1) + w_inter * jnp.einsum('blhd,bhd->bhl', qc, n)
        h = num / jnp.maximum(jnp.abs(den), jnp.exp(-m_j))[..., None]
        m_new = m_j[:, :, -1]
        decay = jnp.exp(cum[:, :, -1] + m - m_new)
        w_k = jnp.exp(cum[:, :, -1:] - cum + ig - m_new[:, :, None])
        C_new = decay[..., None, None] * C + jnp.einsum('bhs,bshd,bshe->bhde', w_k, kc, vc)
        n_new = decay[..., None] * n + jnp.einsum('bhs,bshd->bhd', w_k, kc)
        return (C_new, n_new, m_new), h.transpose(0, 2, 1, 3)

    (C1, n1, m1), hs = lax.scan(step, (C0, n0, m0),
                                (to_chunks(q), to_chunks(k), to_chunks(v),
                                 to_chunks(i_pre), to_chunks(logf)))
    h = hs.swapaxes(0, 1).reshape(B, T, H, DH)
    return h, C1, n1, m1


def _mixer(hn, conv_buf, C0, n0, m0, w_in, b_in, conv_w, conv_b, cn_g, cn_b, mn_g, w_out):
    B, T, _ = hn.shape
    z = hn @ w_in + b_in
    o0 = 2 * D_CONV
    u = z[..., :D_CONV] * jax.nn.sigmoid(z[..., D_CONV:o0])
    y, new_buf = _causal_dwconv(u, conv_buf, conv_w, conv_b)
    y = jax.nn.silu(_layernorm(y, cn_g, cn_b))
    def heads(j):
        return z[..., o0 + j * D_MLSTM:o0 + (j + 1) * D_MLSTM].astype(jnp.float32).reshape(
            B, T, N_MHEADS, HEAD_DIM)
    q = heads(0)
    k = heads(1) * (HEAD_DIM ** -0.5)
    v = heads(2)
    o = heads(3)
    og = o0 + 4 * D_MLSTM
    i_pre = z[..., og:og + N_MHEADS].astype(jnp.float32)
    f_pre = z[..., og + N_MHEADS:og + 2 * N_MHEADS].astype(jnp.float32)
    h, C1, n1, m1 = _mlstm(q, k, v, i_pre, f_pre, C0.astype(jnp.float32),
                           n0.astype(jnp.float32), m0.astype(jnp.float32))
    h = h * lax.rsqrt(jnp.mean(h * h, axis=-1, keepdims=True) + EPS) * mn_g.reshape(
        N_MHEADS, HEAD_DIM).astype(jnp.float32)
    h = (jax.nn.sigmoid(o) * h).reshape(B, T, D_MLSTM).astype(hn.dtype)
    out = jnp.concatenate([y, h], axis=-1) @ w_out
    return (out, new_buf.astype(conv_buf.dtype), C1.astype(C0.dtype),
            n1.astype(n0.dtype), m1.astype(m0.dtype))


def _layer(x, c, conv_buf, C0, n0, m0, w_ada, b_ada, g_ffn1, ffn1_w_gu, ffn1_w_down, g_mix,
           w_in, b_in, conv_w, conv_b, cn_g, cn_b, mn_g, w_out, g_ffn2, ffn2_w_gu, ffn2_w_down):
    ada = (jax.nn.silu(c) @ w_ada + b_ada).reshape(c.shape[0], N_ADA, 1, D_MODEL)
    sh1, sc1, gt1 = ada[:, 0], ada[:, 1], ada[:, 2]
    sh2, sc2, gt2 = ada[:, 3], ada[:, 4], ada[:, 5]
    sh3, sc3, gt3 = ada[:, 6], ada[:, 7], ada[:, 8]
    x = x + RES_HALF * gt1 * _swiglu(_modulate(_rmsnorm(x, g_ffn1), sh1, sc1), ffn1_w_gu, ffn1_w_down)
    mix, cb, C1, n1, m1 = _mixer(_modulate(_rmsnorm(x, g_mix), sh2, sc2), conv_buf, C0, n0, m0,
                                 w_in, b_in, conv_w, conv_b, cn_g, cn_b, mn_g, w_out)
    x = x + gt2 * mix
    x = x + RES_HALF * gt3 * _swiglu(_modulate(_rmsnorm(x, g_ffn2), sh3, sc3), ffn2_w_gu, ffn2_w_down)
    return x, cb, C1, n1, m1


def _trunk(x, c, conv_buf, C0, n0, m0, layer_params, norm_final):
    convs, Cs, ns, ms = [], [], [], []
    for l in range(DEPTH):
        p = [w[l] for w in layer_params]
        x, cb, Cl, nl, ml = _layer(x, c, conv_buf[l], C0[l], n0[l], m0[l], *p)
        convs.append(cb)
        Cs.append(Cl)
        ns.append(nl)
        ms.append(ml)
    return _rmsnorm(x, norm_final), jnp.stack(convs), jnp.stack(Cs), jnp.stack(ns), jnp.stack(ms)


def setup_inputs(seed: int = 0) -> dict:
    key = jax.random.key(seed)
    ks = jax.random.split(key, 26)

    def nrm(k, shape, scale):
        return scale * jax.random.normal(k, shape, jnp.float32)

    def gain(k, shape):
        return 1.0 + 0.02 * jax.random.normal(k, shape, jnp.float32)

    f_off = 2 * D_CONV + 4 * D_MLSTM + N_MHEADS
    b_in = nrm(ks[15], (DEPTH, D_IN), 0.02).at[:, f_off:f_off + N_MHEADS].add(
        jnp.linspace(3.0, 6.0, N_MHEADS))
    return {
        'x_prompt': nrm(ks[0], (BATCH, SEQ, D_MODEL), 1.0),
        'x_sample': nrm(ks[1], (DEC_BATCH, DEC_SEQ, D_MODEL), 1.0),
        'c_prompt': nrm(ks[2], (BATCH, D_MODEL), 1.0),
        'c_sample': nrm(ks[3], (DEC_BATCH, D_MODEL), 1.0),
        'state_conv': nrm(ks[4], (DEPTH, DEC_BATCH, CONV_WIDTH - 1, D_CONV), 0.5),
        'state_C': nrm(ks[5], (DEPTH, DEC_BATCH, N_MHEADS, HEAD_DIM, HEAD_DIM), 0.05),
        'state_n': nrm(ks[6], (DEPTH, DEC_BATCH, N_MHEADS, HEAD_DIM), 0.2),
        'state_m': nrm(ks[7], (DEPTH, DEC_BATCH, N_MHEADS), 1.0),
        'w_ada': nrm(ks[8], (DEPTH, D_MODEL, N_ADA * D_MODEL), 0.5 * D_MODEL ** -0.5),
        'b_ada': nrm(ks[9], (DEPTH, N_ADA * D_MODEL), 0.01),
        'norm_ffn1': gain(ks[10], (DEPTH, D_MODEL)),
        'ffn1_w_gu': nrm(ks[11], (DEPTH, D_MODEL, 2 * D_FF), D_MODEL ** -0.5),
        'ffn1_w_down': nrm(ks[12], (DEPTH, D_FF, D_MODEL), D_FF ** -0.5),
        'norm_mix': gain(ks[13], (DEPTH, D_MODEL)),
        'w_in': nrm(ks[14], (DEPTH, D_MODEL, D_IN), D_MODEL ** -0.5),
        'b_in': b_in,
        'conv_w': nrm(ks[16], (DEPTH, CONV_WIDTH, D_CONV), CONV_WIDTH ** -0.5),
        'conv_b': nrm(ks[17], (DEPTH, D_CONV), 0.02),
        'conv_norm_g': gain(ks[18], (DEPTH, D_CONV)),
        'conv_norm_b': nrm(ks[19], (DEPTH, D_CONV), 0.02),
        'mlstm_norm_g': gain(ks[20], (DEPTH, D_MLSTM)),
        'w_out': nrm(ks[21], (DEPTH, D_MIX, D_MODEL), D_MIX ** -0.5),
        'norm_ffn2': gain(ks[22], (DEPTH, D_MODEL)),
        'ffn2_w_gu': nrm(ks[23], (DEPTH, D_MODEL, 2 * D_FF), D_MODEL ** -0.5),
        'ffn2_w_down': nrm(ks[24], (DEPTH, D_FF, D_MODEL), D_FF ** -0.5),
        'norm_final': gain(ks[25], (D_MODEL,)),
    }


def reference(x_prompt, x_sample, c_prompt, c_sample, state_conv, state_C, state_n, state_m,
              w_ada, b_ada, norm_ffn1, ffn1_w_gu, ffn1_w_down, norm_mix, w_in, b_in, conv_w,
              conv_b, conv_norm_g, conv_norm_b, mlstm_norm_g, w_out, norm_ffn2, ffn2_w_gu,
              ffn2_w_down, norm_final):
    layer_params = (w_ada, b_ada, norm_ffn1, ffn1_w_gu, ffn1_w_down, norm_mix, w_in, b_in,
                    conv_w, conv_b, conv_norm_g, conv_norm_b, mlstm_norm_g, w_out,
                    norm_ffn2, ffn2_w_gu, ffn2_w_down)
    bp = x_prompt.shape[0]
    conv0 = jnp.zeros((DEPTH, bp, CONV_WIDTH - 1, D_CONV), state_conv.dtype)
    C0 = jnp.zeros((DEPTH, bp, N_MHEADS, HEAD_DIM, HEAD_DIM), state_C.dtype)
    n0 = jnp.zeros((DEPTH, bp, N_MHEADS, HEAD_DIM), state_n.dtype)
    m0 = jnp.zeros((DEPTH, bp, N_MHEADS), state_m.dtype)
    y_prompt, conv_p, C_p, n_p, m_p = _trunk(x_prompt, c_prompt, conv0, C0, n0, m0,
                                             layer_params, norm_final)
    y_sample, conv_s, C_s, n_s, m_s = _trunk(x_sample, c_sample, state_conv, state_C, state_n,
                                             state_m, layer_params, norm_final)
    return (y_prompt, y_sample, conv_p, C_p, n_p, m_p, conv_s, C_s, n_s, m_s)
```

```python
import functools

import jax
import jax.numpy as jnp
from jax import lax
from jax.experimental import pallas as pl
from jax.experimental.pallas import tpu as pltpu

F32 = jnp.float32
BF16 = jnp.bfloat16

D_MODEL = 1024
D_CONV = 512
D_MLSTM = 512
N_HEADS = 4
HEAD_DIM = 128
CONV_WIDTH = 31
CONV_TAIL = CONV_WIDTH - 1
D_FF = 2816
N_ADA = 9
EPS = 1e-6
D_Z = 2 * D_CONV + 4 * D_MLSTM
N_GATES = 2 * N_HEADS
LANES = 128
SUBLANES = 8
FF_CHUNK = 256
N_FF_CHUNKS = D_FF // FF_CHUNK
GROUP = 128
TAIL_PAD = 32
VMEM_LIMIT = 56 * 1024 * 1024


def _dot(a, b):
    return jnp.dot(a, b, preferred_element_type=F32)


def _dot_exact(a, b):
    return jnp.dot(a, b, preferred_element_type=F32, precision=lax.Precision.HIGHEST)


def _rms(x, g):
    ms = jnp.mean(x * x, axis=-1, keepdims=True)
    return x * lax.rsqrt(ms + EPS) * g


def _const_spec(shape):
    nd = len(shape)
    return pl.BlockSpec(shape, lambda *_: (0,) * nd, pipeline_mode=pl.Buffered(1))


def _ada_kernel(c_ref, w_ref, b_ref, o_ref):
    c = c_ref[...]
    s = (c * jax.nn.sigmoid(c)).astype(BF16)
    o_ref[...] = _dot(s, w_ref[...].astype(BF16)) + b_ref[...]


def _ada(c_all, w_ada, b_ada):
    n = c_all.shape[0]
    tile = D_MODEL
    return pl.pallas_call(
        _ada_kernel,
        grid=(N_ADA,),
        in_specs=[
            pl.BlockSpec((n, D_MODEL), lambda j: (0, 0)),
            pl.BlockSpec((D_MODEL, tile), lambda j: (0, j)),
            pl.BlockSpec((1, tile), lambda j: (0, j)),
        ],
        out_specs=pl.BlockSpec((n, tile), lambda j: (0, j)),
        out_shape=jax.ShapeDtypeStruct((n, N_ADA * D_MODEL), F32),
        compiler_params=pltpu.CompilerParams(vmem_limit_bytes=VMEM_LIMIT),
        name="ada",
    )(c_all, w_ada, b_ada)


def _swiglu(xm, wgu_ref, wd_ref, hid_ref):
    for j in range(N_FF_CHUNKS):
        gu = _dot(xm, wgu_ref[j])
        gate = gu[:, :FF_CHUNK]
        up = gu[:, FF_CHUNK:]
        hid_ref[:, j * FF_CHUNK:(j + 1) * FF_CHUNK] = (gate * jax.nn.sigmoid(gate) * up).astype(BF16)
    return _dot(hid_ref[...], wd_ref[...])


def _ffn_a_kernel(x_ref, ada_ref, g1_ref, wgu_ref, wd_ref, g2_ref, win_ref, bin_ref, wg_ref, bg_ref,
                  x1_ref, z_ref, gate_ref, hid_ref, *, bb, tt):
    n = bb * tt
    x = x_ref[...]
    ada = ada_ref[...]
    xm = _rms(x, g1_ref[...]) * (1.0 + ada[:, 1:2, :]) + ada[:, 0:1, :]
    ff = _swiglu(xm.reshape(n, D_MODEL).astype(BF16), wgu_ref, wd_ref, hid_ref)
    x1 = x + 0.5 * ada[:, 2:3, :] * ff.reshape(bb, tt, D_MODEL)
    x1_ref[...] = x1
    hm = _rms(x1, g2_ref[...]) * (1.0 + ada[:, 4:5, :]) + ada[:, 3:4, :]
    hm = hm.reshape(n, D_MODEL).astype(BF16)
    z_ref[...] = _dot(hm, win_ref[...]) + bin_ref[...]
    gate_ref[...] = _dot(hm, wg_ref[...]) + bg_ref[...]


def _ffn_b_kernel(x_ref, ada_ref, y_ref, h_ref, wout_ref, g3_ref, wgu_ref, wd_ref, gf_ref,
                  o_ref, hid_ref, *, bb, tt):
    n = bb * tt
    x = x_ref[...]
    ada = ada_ref[...]
    yv = y_ref[...].reshape(n, D_CONV).astype(BF16)
    hv = h_ref[...].reshape(n, D_MLSTM).astype(BF16)
    mix = _dot(yv, wout_ref[:D_CONV, :]) + _dot(hv, wout_ref[D_CONV:, :])
    x2 = x + ada[:, 5:6, :] * mix.reshape(bb, tt, D_MODEL)
    xm = _rms(x2, g3_ref[...]) * (1.0 + ada[:, 7:8, :]) + ada[:, 6:7, :]
    ff = _swiglu(xm.reshape(n, D_MODEL).astype(BF16), wgu_ref, wd_ref, hid_ref)
    x3 = x2 + 0.5 * ada[:, 8:9, :] * ff.reshape(bb, tt, D_MODEL)
    o_ref[...] = _rms(x3, gf_ref[...])


def _tok_specs(bb, tt, width):
    return pl.BlockSpec((bb, tt, width), lambda b, t: (b, t, 0))


def _ffn_a(x, ada3, g1, wgu, wd, g2, win, b_main, wg, b_gate, *, bb, tt):
    bsz, seq, _ = x.shape
    nt = seq // tt
    n = bb * tt
    m = bsz * seq
    row_spec = lambda w: pl.BlockSpec((n, w), lambda b, t: (b * nt + t, 0))
    return pl.pallas_call(
        functools.partial(_ffn_a_kernel, bb=bb, tt=tt),
        grid=(bsz // bb, nt),
        in_specs=[
            _tok_specs(bb, tt, D_MODEL),
            pl.BlockSpec((bb, N_ADA, D_MODEL), lambda b, t: (b, 0, 0)),
            _const_spec((1, D_MODEL)),
            _const_spec(wgu.shape),
            _const_spec(wd.shape),
            _const_spec((1, D_MODEL)),
            _const_spec(win.shape),
            _const_spec(b_main.shape),
            _const_spec(wg.shape),
            _const_spec(b_gate.shape),
        ],
        out_specs=[_tok_specs(bb, tt, D_MODEL), row_spec(D_Z), row_spec(LANES)],
        out_shape=[
            jax.ShapeDtypeStruct(x.shape, F32),
            jax.ShapeDtypeStruct((m, D_Z), F32),
            jax.ShapeDtypeStruct((m, LANES), F32),
        ],
        scratch_shapes=[pltpu.VMEM((n, D_FF), BF16)],
        compiler_params=pltpu.CompilerParams(
            dimension_semantics=("arbitrary", "arbitrary"), vmem_limit_bytes=VMEM_LIMIT),
        name="ffn_a",
    )(x, ada3, g1, wgu, wd, g2, win, b_main, wg, b_gate)


def _ffn_b(x, ada3, y, h, wout, g3, wgu, wd, gf, *, bb, tt):
    bsz, seq, _ = x.shape
    n = bb * tt
    return pl.pallas_call(
        functools.partial(_ffn_b_kernel, bb=bb, tt=tt),
        grid=(bsz // bb, seq // tt),
        in_specs=[
            _tok_specs(bb, tt, D_MODEL),
            pl.BlockSpec((bb, N_ADA, D_MODEL), lambda b, t: (b, 0, 0)),
            _tok_specs(bb, tt, D_CONV),
            _tok_specs(bb, tt, D_MLSTM),
            _const_spec(wout.shape),
            _const_spec((1, D_MODEL)),
            _const_spec(wgu.shape),
            _const_spec(wd.shape),
            _const_spec((1, D_MODEL)),
        ],
        out_specs=_tok_specs(bb, tt, D_MODEL),
        out_shape=jax.ShapeDtypeStruct(x.shape, F32),
        scratch_shapes=[pltpu.VMEM((n, D_FF), BF16)],
        compiler_params=pltpu.CompilerParams(
            dimension_semantics=("arbitrary", "arbitrary"), vmem_limit_bytes=VMEM_LIMIT),
        name="ffn_b",
    )(x, ada3, y, h, wout, g3, wgu, wd, gf)


def _conv_kernel(*refs, bb, tt, bt, rt, has_state):
    if has_state:
        a_ref, b_ref, cs_ref, w_ref, cb_ref, lg_ref, lb_ref, y_ref, so_ref, ubuf = refs
    else:
        a_ref, b_ref, w_ref, cb_ref, lg_ref, lb_ref, y_ref, so_ref, ubuf = refs
    t = pl.program_id(1)
    lo = TAIL_PAD - CONV_TAIL

    @pl.when(t == 0)
    def _():
        if has_state:
            ubuf[:, lo:TAIL_PAD, :] = cs_ref[...]
        else:
            ubuf[:, 0:TAIL_PAD, :] = jnp.zeros((bb, TAIL_PAD, D_CONV), F32)

    ubuf[:, TAIL_PAD:TAIL_PAD + tt, :] = a_ref[...] * jax.nn.sigmoid(b_ref[...])

    for b0 in range(0, bb, bt):
        for r0 in range(0, tt, rt):
            acc = jnp.broadcast_to(cb_ref[...].reshape(1, 1, D_CONV), (bt, rt, D_CONV))
            for j in range(CONV_WIDTH):
                win = ubuf[b0:b0 + bt, lo + r0 + j:lo + r0 + j + rt, :]
                acc = acc + w_ref[j:j + 1, :].reshape(1, 1, D_CONV) * win
            mu = jnp.mean(acc, axis=-1, keepdims=True)
            xc = acc - mu
            var = jnp.mean(xc * xc, axis=-1, keepdims=True)
            yn = xc * lax.rsqrt(var + EPS) * lg_ref[...].reshape(1, 1, D_CONV) + lb_ref[...].reshape(1, 1, D_CONV)
            y_ref[b0:b0 + bt, r0:r0 + rt, :] = yn * jax.nn.sigmoid(yn)

    tail = ubuf[:, lo + tt:TAIL_PAD + tt, :]
    ubuf[:, lo:TAIL_PAD, :] = tail

    @pl.when(t == pl.num_programs(1) - 1)
    def _():
        so_ref[...] = tail


def _conv(z3, state, w, cb, lg, lb, *, bb, tt, bt, rt):
    bsz, seq, _ = z3.shape
    has_state = state is not None
    in_specs = [
        pl.BlockSpec((bb, tt, D_CONV), lambda b, t: (b, t, 0)),
        pl.BlockSpec((bb, tt, D_CONV), lambda b, t: (b, t, 1)),
    ]
    args = [z3, z3]
    if has_state:
        in_specs.append(pl.BlockSpec((bb, CONV_TAIL, D_CONV), lambda b, t: (b, 0, 0)))
        args.append(state)
    in_specs += [_const_spec(w.shape), _const_spec(cb.shape), _const_spec(lg.shape), _const_spec(lb.shape)]
    args += [w, cb, lg, lb]
    return pl.pallas_call(
        functools.partial(_conv_kernel, bb=bb, tt=tt, bt=bt, rt=rt, has_state=has_state),
        grid=(bsz // bb, seq // tt),
        in_specs=in_specs,
        out_specs=[
            pl.BlockSpec((bb, tt, D_CONV), lambda b, t: (b, t, 0)),
            pl.BlockSpec((bb, CONV_TAIL, D_CONV), lambda b, t: (b, 0, 0)),
        ],
        out_shape=[
            jax.ShapeDtypeStruct((bsz, seq, D_CONV), F32),
            jax.ShapeDtypeStruct((bsz, CONV_TAIL, D_CONV), F32),
        ],
        scratch_shapes=[pltpu.VMEM((bb, TAIL_PAD + tt, D_CONV), F32)],
        compiler_params=pltpu.CompilerParams(
            dimension_semantics=("arbitrary", "arbitrary"), vmem_limit_bytes=VMEM_LIMIT),
        name="conv",
    )(*args)


def _mlstm_kernel(*refs, nseq, ls, ng, has_state):
    if has_state:
        (q_ref, k_ref, v_ref, o_ref, gt_ref, mg_ref, c0_ref, n0_ref, m0_ref,
         h_ref, c_ref, n_ref, m_ref) = refs
    else:
        (q_ref, k_ref, v_ref, o_ref, gt_ref, mg_ref,
         h_ref, c_ref, n_ref, m_ref, mcar) = refs
        c0_ref, n0_ref = c_ref, n_ref

        @pl.when(pl.program_id(1) == 0)
        def _():
            c_ref[...] = jnp.zeros(c_ref.shape, F32)
            n_ref[...] = jnp.zeros(n_ref.shape, F32)
            mcar[...] = jnp.zeros(mcar.shape, F32)

    row = lax.broadcasted_iota(jnp.int32, (GROUP, GROUP), 0)
    col = lax.broadcasted_iota(jnp.int32, (GROUP, GROUP), 1)
    shift = ls.bit_length() - 1
    seg_r = lax.shift_right_logical(row, shift)
    seg_c = lax.shift_right_logical(col, shift)
    causal = jnp.logical_and(col <= row, seg_r == seg_c)
    segtril = causal.astype(F32)
    lastsel = (col == seg_r * ls + (ls - 1)).astype(F32)

    def rows(ref, g, width):
        if ng == 1:
            return ref[...].reshape(GROUP, width)
        return ref[0, g * GROUP:(g + 1) * GROUP, :]

    for g in range(ng):
        gates = rows(gt_ref, g, LANES)
        logf = jnp.minimum(gates, 0.0) - jnp.log(1.0 + jnp.exp(-jnp.abs(gates)))
        cum = _dot_exact(segtril, logf)
        cumlast = _dot_exact(lastsel, cum)
        cum_t = cum.T
        gates_t = gates.T
        if has_state:
            mprev = rows(m0_ref, g, LANES)
        else:
            mprev = jnp.broadcast_to(mcar[...], (GROUP, LANES))
        qg = rows(q_ref, g, D_MLSTM)
        kg = rows(k_ref, g, D_MLSTM) * (HEAD_DIM ** -0.5)
        vg = rows(v_ref, g, D_MLSTM)
        og = rows(o_ref, g, D_MLSTM)

        m_all = jnp.zeros((GROUP, LANES), F32)
        lane = lax.broadcasted_iota(jnp.int32, (GROUP, LANES), 1)
        hs = []
        for h in range(N_HEADS):
            sl = slice(h * HEAD_DIM, (h + 1) * HEAD_DIM)
            cum_col = cum[:, N_HEADS + h:N_HEADS + h + 1]
            cum_row = cum_t[N_HEADS + h:N_HEADS + h + 1, :]
            i_row = gates_t[h:h + 1, :]
            mprev_col = mprev[:, h:h + 1]
            dmat = jnp.where(causal, cum_col - cum_row + i_row, -jnp.inf)
            inter = cum_col + mprev_col
            m_col = jnp.maximum(inter, jnp.max(dmat, axis=-1, keepdims=True))
            w_intra = jnp.exp(dmat - m_col)
            w_inter = jnp.exp(inter - m_col)
            qh = qg[:, sl]
            qb = qh.astype(BF16)
            kb = kg[:, sl].astype(BF16)
            vb = vg[:, sl].astype(BF16)
            s = lax.dot_general(qb, kb, (((1,), (1,)), ((), ())), preferred_element_type=F32)
            p = s * w_intra
            num = _dot(p.astype(BF16), vb)
            den = jnp.sum(p, axis=-1, keepdims=True)
            if nseq == 1:
                qc = _dot(qb, c0_ref[0, h].astype(BF16))
                qn = jnp.sum(qh * n0_ref[0, h:h + 1, :], axis=-1, keepdims=True)
            else:
                qc = jnp.zeros((GROUP, HEAD_DIM), F32)
                qn = jnp.zeros((GROUP, 1), F32)
                for sq in range(nseq):
                    qc = jnp.where(seg_r == sq, _dot(qb, c0_ref[sq, h].astype(BF16)), qc)
                    qn = jnp.where(seg_r[:, 0:1] == sq,
                                   jnp.sum(qh * n0_ref[sq, h:h + 1, :], axis=-1, keepdims=True), qn)
            num = num + w_inter * qc
            den = den + w_inter * qn
            hh = num / jnp.maximum(jnp.abs(den), jnp.exp(-m_col))
            hh = hh * lax.rsqrt(jnp.mean(hh * hh, axis=-1, keepdims=True) + EPS) * mg_ref[:, sl]
            hs.append(jax.nn.sigmoid(og[:, sl]) * hh)
            m_all = jnp.where(lane == h, m_col, m_all)

        hcat = jnp.concatenate(hs, axis=-1)
        if ng == 1:
            h_ref[...] = hcat.reshape(h_ref.shape)
            m_ref[...] = m_all.reshape(m_ref.shape)
        else:
            h_ref[0, g * GROUP:(g + 1) * GROUP, :] = hcat
            m_ref[0, g * GROUP:(g + 1) * GROUP, :] = m_all

        mnew = _dot_exact(lastsel, m_all)
        for h in range(N_HEADS):
            sl = slice(h * HEAD_DIM, (h + 1) * HEAD_DIM)
            cum_col = cum[:, N_HEADS + h:N_HEADS + h + 1]
            last_col = cumlast[:, N_HEADS + h:N_HEADS + h + 1]
            i_col = gates[:, h:h + 1]
            mnew_col = mnew[:, h:h + 1]
            w_k = jnp.exp(last_col - cum_col + i_col - mnew_col)
            decay = jnp.exp(last_col + mprev[:, h:h + 1] - mnew_col)
            kw = kg[:, sl] * w_k
            kwb = kw.astype(BF16)
            vh = vg[:, sl]
            for sq in range(nseq):
                if nseq == 1:
                    vs, kws = vh, kw
                else:
                    vs = jnp.where(seg_r == sq, vh, 0.0)
                    kws = jnp.where(seg_r == sq, kw, 0.0)
                dc = lax.dot_general(kwb, vs.astype(BF16), (((0,), (0,)), ((), ())),
                                     preferred_element_type=F32)
                dn = jnp.sum(kws, axis=0, keepdims=True)
                dec = decay[sq * ls:sq * ls + 1, :]
                c_ref[sq, h] = dec * c0_ref[sq, h] + dc
                n_ref[sq, h:h + 1, :] = dec * n0_ref[sq, h:h + 1, :] + dn
        if not has_state:
            mcar[...] = mnew[GROUP - 1:GROUP, :]


def _mlstm(z3, gates3, mg, states, *, bb, tt):
    bsz, seq, _ = z3.shape
    has_state = states is not None
    if has_state:
        nseq, ls, ng = bb, tt, 1
    else:
        nseq, ls, ng = 1, GROUP, tt // GROUP
    assert nseq * ls == GROUP and bb * tt == ng * GROUP
    zspec = lambda k: pl.BlockSpec((bb, tt, D_MLSTM), lambda b, t: (b, t, k))
    in_specs = [zspec(2), zspec(3), zspec(4), zspec(5),
                pl.BlockSpec((bb, tt, LANES), lambda b, t: (b, t, 0)),
                _const_spec(mg.shape)]
    args = [z3, z3, z3, z3, gates3, mg]
    c_spec = pl.BlockSpec((bb, N_HEADS, HEAD_DIM, HEAD_DIM), lambda b, t: (b, 0, 0, 0))
    n_spec = pl.BlockSpec((bb, N_HEADS, HEAD_DIM), lambda b, t: (b, 0, 0))
    m_spec = pl.BlockSpec((bb, tt, LANES), lambda b, t: (b, t, 0))
    scratch = []
    if has_state:
        in_specs += [c_spec, n_spec, m_spec]
        args += list(states)
    else:
        scratch = [pltpu.VMEM((1, LANES), F32)]
    return pl.pallas_call(
        functools.partial(_mlstm_kernel, nseq=nseq, ls=ls, ng=ng, has_state=has_state),
        grid=(bsz // bb, seq // tt),
        in_specs=in_specs,
        out_specs=[pl.BlockSpec((bb, tt, D_MLSTM), lambda b, t: (b, t, 0)), c_spec, n_spec, m_spec],
        out_shape=[
            jax.ShapeDtypeStruct((bsz, seq, D_MLSTM), F32),
            jax.ShapeDtypeStruct((bsz, N_HEADS, HEAD_DIM, HEAD_DIM), F32),
            jax.ShapeDtypeStruct((bsz, N_HEADS, HEAD_DIM), F32),
            jax.ShapeDtypeStruct((bsz, seq, LANES), F32),
        ],
        scratch_shapes=scratch,
        compiler_params=pltpu.CompilerParams(
            dimension_semantics=("arbitrary", "arbitrary"), vmem_limit_bytes=VMEM_LIMIT),
        name="mlstm",
    )(*args)


def _trunk(x, ada3, states, p, *, ffn_a_tiles, ffn_b_tiles, conv_tiles, mlstm_tiles):
    bsz, seq, _ = x.shape
    x1, z, gates = _ffn_a(x, ada3, p["g1"], p["wgu1"], p["wd1"], p["g2"], p["win"], p["b_main"],
                          p["wg"], p["b_gate"], **ffn_a_tiles)
    z3 = z.reshape(bsz, seq, D_Z)
    gates3 = gates.reshape(bsz, seq, LANES)
    if states is None:
        conv_state, ml_states = None, None
    else:
        conv_state, c0, n0, m0 = states
        m_tok = jnp.broadcast_to(
            jnp.pad(m0, ((0, 0), (0, LANES - N_HEADS)))[:, None, :], (bsz, seq, LANES))
        ml_states = (c0, n0, m_tok)
    y, conv_new = _conv(z3, conv_state, p["conv_w"], p["conv_b"], p["cn_g"], p["cn_b"], **conv_tiles)
    h, c_new, n_new, m_tok_new = _mlstm(z3, gates3, p["mn_g"], ml_states, **mlstm_tiles)
    out = _ffn_b(x1, ada3, y, h, p["wout"], p["g3"], p["wgu2"], p["wd2"], p["gf"], **ffn_b_tiles)
    m_new = m_tok_new[:, seq - 1, :N_HEADS]
    return out, conv_new[None], c_new[None], n_new[None], m_new[None]


def _chunk_gu(w_gu):
    w = w_gu.reshape(D_MODEL, 2, N_FF_CHUNKS, FF_CHUNK)
    return jnp.transpose(w, (2, 0, 1, 3)).reshape(N_FF_CHUNKS, D_MODEL, 2 * FF_CHUNK).astype(BF16)


def kernel(x_prompt, x_sample, c_prompt, c_sample, state_conv, state_C, state_n, state_m, w_ada, b_ada, norm_ffn1, ffn1_w_gu, ffn1_w_down, norm_mix, w_in, b_in, conv_w, conv_b, conv_norm_g, conv_norm_b, mlstm_norm_g, w_out, norm_ffn2, ffn2_w_gu, ffn2_w_down, norm_final):
    assert w_ada.shape[0] == 1, "single layer"
    bp = x_prompt.shape[0]
    pad_lanes = LANES - N_GATES
    p = {
        "g1": norm_ffn1, "g2": norm_mix, "g3": norm_ffn2, "gf": norm_final[None],
        "wgu1": _chunk_gu(ffn1_w_gu[0]), "wd1": ffn1_w_down[0].astype(BF16),
        "wgu2": _chunk_gu(ffn2_w_gu[0]), "wd2": ffn2_w_down[0].astype(BF16),
        "win": w_in[0][:, :D_Z].astype(BF16),
        "wg": jnp.pad(w_in[0][:, D_Z:], ((0, 0), (0, pad_lanes))).astype(BF16),
        "b_main": b_in[:, :D_Z],
        "b_gate": jnp.pad(b_in[:, D_Z:], ((0, 0), (0, pad_lanes))),
        "conv_w": jnp.pad(conv_w[0], ((0, TAIL_PAD - CONV_WIDTH), (0, 0))),
        "conv_b": conv_b, "cn_g": conv_norm_g, "cn_b": conv_norm_b, "mn_g": mlstm_norm_g,
        "wout": w_out[0].astype(BF16),
    }
    c_all = jnp.concatenate([c_prompt, c_sample], axis=0)
    ada = _ada(c_all, w_ada[0], b_ada).reshape(c_all.shape[0], N_ADA, D_MODEL)

    yp, conv_p, c_p, n_p, m_p = _trunk(
        x_prompt, ada[:bp], None, p,
        ffn_a_tiles=dict(bb=1, tt=256), ffn_b_tiles=dict(bb=1, tt=512),
        conv_tiles=dict(bb=1, tt=512, bt=1, rt=64),
        mlstm_tiles=dict(bb=1, tt=512))
    ys, conv_s, c_s, n_s, m_s = _trunk(
        x_sample, ada[bp:], (state_conv[0], state_C[0], state_n[0], state_m[0]), p,
        ffn_a_tiles=dict(bb=32, tt=8), ffn_b_tiles=dict(bb=64, tt=8),
        conv_tiles=dict(bb=32, tt=8, bt=8, rt=8),
        mlstm_tiles=dict(bb=16, tt=8))
    return (yp, ys, conv_p, c_p, n_p, m_p, conv_s, c_s, n_s, m_s)
```

```python
import functools

import jax
import jax.numpy as jnp
from jax import lax
from jax.experimental import pallas as pl
from jax.experimental.pallas import tpu as pltpu

F32 = jnp.float32
BF16 = jnp.bfloat16

D_MODEL = 1024
D_CONV = 512
D_MLSTM = 512
N_HEADS = 4
HEAD_DIM = 128
CONV_WIDTH = 31
CONV_TAIL = CONV_WIDTH - 1
D_FF = 2816
N_ADA = 9
EPS = 1e-6
D_Z = 2 * D_CONV + 4 * D_MLSTM
LANES = 128
SUBLANES = 8
D_GATE = 2 * LANES
FF_CHUNK = 256
N_FF_CHUNKS = D_FF // FF_CHUNK
GROUP = 128
TAIL_PAD = 32
VMEM_LIMIT = 56 * 1024 * 1024
NT_DIMS = (((1,), (1,)), ((), ()))
TN_DIMS = (((0,), (0,)), ((), ()))


def _dot(a, b):
    return jnp.dot(a, b, preferred_element_type=F32)


def _dot_exact(a, b):
    return jnp.dot(a, b, preferred_element_type=F32, precision=lax.Precision.HIGHEST)


def _rms(x, g):
    ms = jnp.mean(x * x, axis=-1, keepdims=True)
    return x * lax.rsqrt(ms + EPS) * g


def _log_sigmoid(x):
    return jnp.minimum(x, 0.0) - jnp.log(1.0 + jnp.exp(-jnp.abs(x)))


def _const_spec(shape):
    nd = len(shape)
    return pl.BlockSpec(shape, lambda *_: (0,) * nd, pipeline_mode=pl.Buffered(1))


def _ada_kernel(c_ref, w_ref, b_ref, o_ref):
    c = c_ref[...]
    s = (c * jax.nn.sigmoid(c)).astype(BF16)
    o_ref[...] = _dot(s, w_ref[...].astype(BF16)) + b_ref[...]


def _ada(c_all, w_ada, b_ada):
    n = c_all.shape[0]
    tile = D_MODEL
    return pl.pallas_call(
        _ada_kernel,
        grid=(N_ADA,),
        in_specs=[
            pl.BlockSpec((n, D_MODEL), lambda j: (0, 0)),
            pl.BlockSpec((D_MODEL, tile), lambda j: (0, j)),
            pl.BlockSpec((1, tile), lambda j: (0, j)),
        ],
        out_specs=pl.BlockSpec((n, tile), lambda j: (0, j)),
        out_shape=jax.ShapeDtypeStruct((n, N_ADA * D_MODEL), F32),
        compiler_params=pltpu.CompilerParams(vmem_limit_bytes=VMEM_LIMIT),
        name="ada",
    )(c_all, w_ada, b_ada)


def _swiglu(xm, wgu_ref, wd_ref, hid_ref):
    for j in range(N_FF_CHUNKS):
        lo = j * FF_CHUNK
        gate = _dot(xm, wgu_ref[:, lo:lo + FF_CHUNK])
        up = _dot(xm, wgu_ref[:, D_FF + lo:D_FF + lo + FF_CHUNK])
        hid_ref[:, lo:lo + FF_CHUNK] = (gate * jax.nn.sigmoid(gate) * up).astype(BF16)
    return _dot(hid_ref[...], wd_ref[...])


def _ffn_a_kernel(x_ref, ada_ref, g1_ref, wgu_ref, wd_ref, g2_ref, win_ref, bin_ref, wg_ref, bg_ref,
                  x1_ref, z_ref, gate_ref, hid_ref, *, bb, tt):
    n = bb * tt
    x = x_ref[...]
    ada = ada_ref[...]
    xm = _rms(x, g1_ref[...]) * (1.0 + ada[:, 1:2, :]) + ada[:, 0:1, :]
    ff = _swiglu(xm.reshape(n, D_MODEL).astype(BF16), wgu_ref, wd_ref, hid_ref)
    x1 = x + 0.5 * ada[:, 2:3, :] * ff.reshape(bb, tt, D_MODEL)
    x1_ref[...] = x1
    hm = _rms(x1, g2_ref[...]) * (1.0 + ada[:, 4:5, :]) + ada[:, 3:4, :]
    hm = hm.reshape(n, D_MODEL).astype(BF16)
    z_ref[...] = _dot(hm, win_ref[...]) + bin_ref[...]
    gate_ref[...] = _dot(hm, wg_ref[...]) + bg_ref[...]


def _ffn_b_kernel(x_ref, ada_ref, y_ref, h_ref, wout_ref, g3_ref, wgu_ref, wd_ref, gf_ref,
                  o_ref, hid_ref, *, bb, tt):
    n = bb * tt
    x = x_ref[...]
    ada = ada_ref[...]
    yv = y_ref[...].reshape(n, D_CONV).astype(BF16)
    hv = h_ref[...].reshape(n, D_MLSTM).astype(BF16)
    mix = _dot(yv, wout_ref[:D_CONV, :]) + _dot(hv, wout_ref[D_CONV:, :])
    x2 = x + ada[:, 5:6, :] * mix.reshape(bb, tt, D_MODEL)
    xm = _rms(x2, g3_ref[...]) * (1.0 + ada[:, 7:8, :]) + ada[:, 6:7, :]
    ff = _swiglu(xm.reshape(n, D_MODEL).astype(BF16), wgu_ref, wd_ref, hid_ref)
    x3 = x2 + 0.5 * ada[:, 8:9, :] * ff.reshape(bb, tt, D_MODEL)
    o_ref[...] = _rms(x3, gf_ref[...])


def _tok_specs(bb, tt, width):
    return pl.BlockSpec((bb, tt, width), lambda b, t: (b, t, 0))


def _ffn_a(x, ada3, g1, wgu, wd, g2, win, b_main, wg, b_gate, *, bb, tt):
    bsz, seq, _ = x.shape
    nt = seq // tt
    n = bb * tt
    m = bsz * seq
    row_spec = lambda w: pl.BlockSpec((n, w), lambda b, t: (b * nt + t, 0))
    return pl.pallas_call(
        functools.partial(_ffn_a_kernel, bb=bb, tt=tt),
        grid=(bsz // bb, nt),
        in_specs=[
            _tok_specs(bb, tt, D_MODEL),
            pl.BlockSpec((bb, N_ADA, D_MODEL), lambda b, t: (b, 0, 0)),
            _const_spec((1, D_MODEL)),
            _const_spec(wgu.shape),
            _const_spec(wd.shape),
            _const_spec((1, D_MODEL)),
            _const_spec(win.shape),
            _const_spec(b_main.shape),
            _const_spec(wg.shape),
            _const_spec(b_gate.shape),
        ],
        out_specs=[_tok_specs(bb, tt, D_MODEL), row_spec(D_Z), row_spec(D_GATE)],
        out_shape=[
            jax.ShapeDtypeStruct(x.shape, F32),
            jax.ShapeDtypeStruct((m, D_Z), F32),
            jax.ShapeDtypeStruct((m, D_GATE), F32),
        ],
        scratch_shapes=[pltpu.VMEM((n, D_FF), BF16)],
        compiler_params=pltpu.CompilerParams(
            dimension_semantics=("arbitrary", "arbitrary"), vmem_limit_bytes=VMEM_LIMIT),
        name="ffn_a",
    )(x, ada3, g1, wgu, wd, g2, win, b_main, wg, b_gate)


def _ffn_b(x, ada3, y, h, wout, g3, wgu, wd, gf, *, bb, tt):
    bsz, seq, _ = x.shape
    n = bb * tt
    return pl.pallas_call(
        functools.partial(_ffn_b_kernel, bb=bb, tt=tt),
        grid=(bsz // bb, seq // tt),
        in_specs=[
            _tok_specs(bb, tt, D_MODEL),
            pl.BlockSpec((bb, N_ADA, D_MODEL), lambda b, t: (b, 0, 0)),
            _tok_specs(bb, tt, D_CONV),
            _tok_specs(bb, tt, D_MLSTM),
            _const_spec(wout.shape),
            _const_spec((1, D_MODEL)),
            _const_spec(wgu.shape),
            _const_spec(wd.shape),
            _const_spec((1, D_MODEL)),
        ],
        out_specs=_tok_specs(bb, tt, D_MODEL),
        out_shape=jax.ShapeDtypeStruct(x.shape, F32),
        scratch_shapes=[pltpu.VMEM((n, D_FF), BF16)],
        compiler_params=pltpu.CompilerParams(
            dimension_semantics=("arbitrary", "arbitrary"), vmem_limit_bytes=VMEM_LIMIT),
        name="ffn_b",
    )(x, ada3, y, h, wout, g3, wgu, wd, gf)


def _conv_kernel(*refs, bb, tt, bt, rt, has_state):
    if has_state:
        a_ref, b_ref, cs_ref, w_ref, cb_ref, lg_ref, lb_ref, y_ref, so_ref, ubuf, ush = refs
    else:
        a_ref, b_ref, w_ref, cb_ref, lg_ref, lb_ref, y_ref, so_ref, ubuf, ush = refs
    t = pl.program_id(1)
    lo = TAIL_PAD - CONV_TAIL
    sh_rows = TAIL_PAD - SUBLANES + tt

    @pl.when(t == 0)
    def _():
        if has_state:
            ubuf[:, lo:TAIL_PAD, :] = cs_ref[...]
        else:
            ubuf[:, 0:TAIL_PAD, :] = jnp.zeros((bb, TAIL_PAD, D_CONV), F32)

    ubuf[:, TAIL_PAD:TAIL_PAD + tt, :] = a_ref[...] * jax.nn.sigmoid(b_ref[...])
    for r in range(1, SUBLANES):
        ush[r - 1] = ubuf[:, r:r + sh_rows, :]

    for b0 in range(0, bb, bt):
        for r0 in range(0, tt, rt):
            acc = jnp.broadcast_to(cb_ref[...].reshape(1, 1, D_CONV), (bt, rt, D_CONV))
            for j in range(CONV_WIDTH):
                q, r = divmod(lo + j, SUBLANES)
                p0 = r0 + q * SUBLANES
                if r == 0:
                    win = ubuf[b0:b0 + bt, p0:p0 + rt, :]
                else:
                    win = ush[r - 1, b0:b0 + bt, p0:p0 + rt, :]
                acc = acc + w_ref[j:j + 1, :].reshape(1, 1, D_CONV) * win
            mu = jnp.mean(acc, axis=-1, keepdims=True)
            xc = acc - mu
            var = jnp.mean(xc * xc, axis=-1, keepdims=True)
            yn = xc * lax.rsqrt(var + EPS) * lg_ref[...].reshape(1, 1, D_CONV) + lb_ref[...].reshape(1, 1, D_CONV)
            y_ref[b0:b0 + bt, r0:r0 + rt, :] = yn * jax.nn.sigmoid(yn)

    tail = ubuf[:, lo + tt:TAIL_PAD + tt, :]
    ubuf[:, lo:TAIL_PAD, :] = tail

    @pl.when(t == pl.num_programs(1) - 1)
    def _():
        so_ref[...] = tail


def _conv(z3, state, w, cb, lg, lb, *, bb, tt, bt, rt):
    bsz, seq, _ = z3.shape
    has_state = state is not None
    in_specs = [
        pl.BlockSpec((bb, tt, D_CONV), lambda b, t: (b, t, 0)),
        pl.BlockSpec((bb, tt, D_CONV), lambda b, t: (b, t, 1)),
    ]
    args = [z3, z3]
    if has_state:
        in_specs.append(pl.BlockSpec((bb, CONV_TAIL, D_CONV), lambda b, t: (b, 0, 0)))
        args.append(state)
    in_specs += [_const_spec(w.shape), _const_spec(cb.shape), _const_spec(lg.shape), _const_spec(lb.shape)]
    args += [w, cb, lg, lb]
    return pl.pallas_call(
        functools.partial(_conv_kernel, bb=bb, tt=tt, bt=bt, rt=rt, has_state=has_state),
        grid=(bsz // bb, seq // tt),
        in_specs=in_specs,
        out_specs=[
            pl.BlockSpec((bb, tt, D_CONV), lambda b, t: (b, t, 0)),
            pl.BlockSpec((bb, CONV_TAIL, D_CONV), lambda b, t: (b, 0, 0)),
        ],
        out_shape=[
            jax.ShapeDtypeStruct((bsz, seq, D_CONV), F32),
            jax.ShapeDtypeStruct((bsz, CONV_TAIL, D_CONV), F32),
        ],
        scratch_shapes=[
            pltpu.VMEM((bb, TAIL_PAD + tt, D_CONV), F32),
            pltpu.VMEM((SUBLANES - 1, bb, TAIL_PAD - SUBLANES + tt, D_CONV), F32),
        ],
        compiler_params=pltpu.CompilerParams(
            dimension_semantics=("arbitrary", "arbitrary"), vmem_limit_bytes=VMEM_LIMIT),
        name="conv",
    )(*args)


def _scan_rows(x, op, ident):
    row = lax.broadcasted_iota(jnp.int32, x.shape, 0)
    s = 1
    while s < x.shape[0]:
        x = op(x, jnp.where(row >= s, pltpu.roll(x, s, axis=0), ident))
        s *= 2
    return x


def _mlstm_prompt_kernel(q_ref, k_ref, v_ref, o_ref, gt_ref, mg_ref,
                         h_ref, c_ref, n_ref, m_ref, cext, mcar, *, ng):
    t = pl.program_id(1)

    @pl.when(t == 0)
    def _():
        cext[...] = jnp.zeros(cext.shape, F32)
        mcar[...] = jnp.zeros(mcar.shape, F32)

    row = lax.broadcasted_iota(jnp.int32, (GROUP, GROUP), 0)
    col = lax.broadcasted_iota(jnp.int32, (GROUP, GROUP), 1)
    causal = col <= row
    ones_b = jnp.ones((GROUP, HEAD_DIM), BF16)

    for g in range(ng):
        rs = slice(g * GROUP, (g + 1) * GROUP)
        gi = gt_ref[0, rs, 0:LANES]
        cum = _scan_rows(_log_sigmoid(gt_ref[0, rs, LANES:D_GATE]), jnp.add, 0.0)
        gv = gi - cum
        mx = _scan_rows(gv, jnp.maximum, -jnp.inf)
        mprev = mcar[...]
        mm = jnp.maximum(mprev, mx)
        mcol = cum + mm
        mm_last = mm[GROUP - 1:GROUP, :]
        w_k = jnp.exp(gv - mm_last)
        decay = jnp.exp(mprev - mm_last)
        mcar[...] = mcol[GROUP - 1:GROUP, :]
        gv_t = gv.T

        for h in range(N_HEADS):
            sl = slice(h * HEAD_DIM, (h + 1) * HEAD_DIM)
            mm_b = jnp.broadcast_to(mm[:, h:h + 1], (GROUP, GROUP))
            mcol_b = jnp.broadcast_to(mcol[:, h:h + 1], (GROUP, GROUP))
            w_intra = jnp.exp(jnp.where(causal, gv_t[h:h + 1, :] - mm_b, -jnp.inf))
            w_inter = jnp.exp(mprev[:, h:h + 1] - mm_b)
            qb = q_ref[0, rs, sl].astype(BF16)
            kh = k_ref[0, rs, sl] * (HEAD_DIM ** -0.5)
            v_ext = jnp.concatenate([v_ref[0, rs, sl].astype(BF16), ones_b], axis=1)
            s = lax.dot_general(qb, kh.astype(BF16), NT_DIMS, preferred_element_type=F32)
            pv = _dot((s * w_intra).astype(BF16), v_ext)
            cprev = cext[h]
            qc = _dot(qb, cprev.astype(BF16))
            num = pv[:, :HEAD_DIM] + w_inter * qc[:, :HEAD_DIM]
            den = pv[:, HEAD_DIM:] + w_inter * qc[:, HEAD_DIM:]
            hh = num / jnp.maximum(jnp.abs(den), jnp.exp(-mcol_b))
            hh = hh * lax.rsqrt(jnp.mean(hh * hh, axis=-1, keepdims=True) + EPS) * mg_ref[:, sl]
            h_ref[0, rs, sl] = jax.nn.sigmoid(o_ref[0, rs, sl]) * hh
            kw = (kh * w_k[:, h:h + 1]).astype(BF16)
            dc = lax.dot_general(kw, v_ext, TN_DIMS, preferred_element_type=F32)
            cext[h] = decay[:, h:h + 1] * cprev + dc

    @pl.when(t == pl.num_programs(1) - 1)
    def _():
        for h in range(N_HEADS):
            c_ref[0, h] = cext[h, :, :HEAD_DIM]
            n_ref[0, h:h + 1, :] = cext[h, :, HEAD_DIM:].T[0:1, :]
        m_ref[0] = mcar[...]


def _mlstm_prompt(z3, gates3, mg, *, tt):
    bsz, seq, _ = z3.shape
    zspec = lambda k: pl.BlockSpec((1, tt, D_MLSTM), lambda b, t: (b, t, k))
    return pl.pallas_call(
        functools.partial(_mlstm_prompt_kernel, ng=tt // GROUP),
        grid=(bsz, seq // tt),
        in_specs=[zspec(2), zspec(3), zspec(4), zspec(5),
                  pl.BlockSpec((1, tt, D_GATE), lambda b, t: (b, t, 0)),
                  _const_spec(mg.shape)],
        out_specs=[
            pl.BlockSpec((1, tt, D_MLSTM), lambda b, t: (b, t, 0)),
            pl.BlockSpec((1, N_HEADS, HEAD_DIM, HEAD_DIM), lambda b, t: (b, 0, 0, 0)),
            pl.BlockSpec((1, N_HEADS, HEAD_DIM), lambda b, t: (b, 0, 0)),
            pl.BlockSpec((1, 1, LANES), lambda b, t: (b, 0, 0)),
        ],
        out_shape=[
            jax.ShapeDtypeStruct((bsz, seq, D_MLSTM), F32),
            jax.ShapeDtypeStruct((bsz, N_HEADS, HEAD_DIM, HEAD_DIM), F32),
            jax.ShapeDtypeStruct((bsz, N_HEADS, HEAD_DIM), F32),
            jax.ShapeDtypeStruct((bsz, 1, LANES), F32),
        ],
        scratch_shapes=[
            pltpu.VMEM((N_HEADS, HEAD_DIM, 2 * HEAD_DIM), F32),
            pltpu.VMEM((1, LANES), F32),
        ],
        compiler_params=pltpu.CompilerParams(
            dimension_semantics=("arbitrary", "arbitrary"), vmem_limit_bytes=VMEM_LIMIT),
        name="mlstm_prompt",
    )(z3, z3, z3, z3, gates3, mg)


def _mlstm_sample_kernel(q_ref, k_ref, v_ref, o_ref, gt_ref, mg_ref, c0_ref, n0_ref, m0_ref,
                         h_ref, c_ref, n_ref, m_ref, *, nseq, ls):
    row = lax.broadcasted_iota(jnp.int32, (GROUP, GROUP), 0)
    col = lax.broadcasted_iota(jnp.int32, (GROUP, GROUP), 1)
    shift = ls.bit_length() - 1
    seg_r = lax.shift_right_logical(row, shift)
    seg_c = lax.shift_right_logical(col, shift)
    causal = jnp.logical_and(col <= row, seg_r == seg_c)
    segtril = causal.astype(F32)
    lastsel = (col == seg_r * ls + (ls - 1)).astype(F32)

    gates = gt_ref[...].reshape(GROUP, D_GATE)
    gi = gates[:, :LANES]
    cum = _dot_exact(segtril, _log_sigmoid(gates[:, LANES:]))
    cumlast = _dot_exact(lastsel, cum)
    cum_t = cum.T
    gi_t = gi.T
    mprev = m0_ref[...].reshape(GROUP, LANES)
    qg = q_ref[...].reshape(GROUP, D_MLSTM)
    kg = k_ref[...].reshape(GROUP, D_MLSTM) * (HEAD_DIM ** -0.5)
    vg = v_ref[...].reshape(GROUP, D_MLSTM)
    og = o_ref[...].reshape(GROUP, D_MLSTM)

    m_all = jnp.zeros((GROUP, LANES), F32)
    lane = lax.broadcasted_iota(jnp.int32, (GROUP, LANES), 1)
    hs = []
    for h in range(N_HEADS):
        sl = slice(h * HEAD_DIM, (h + 1) * HEAD_DIM)
        cum_col = cum[:, h:h + 1]
        dmat = jnp.where(causal, cum_col - cum_t[h:h + 1, :] + gi_t[h:h + 1, :], -jnp.inf)
        inter = cum_col + mprev[:, h:h + 1]
        m_col = jnp.maximum(inter, jnp.max(dmat, axis=-1, keepdims=True))
        w_intra = jnp.exp(dmat - m_col)
        w_inter = jnp.exp(inter - m_col)
        qh = qg[:, sl]
        qb = qh.astype(BF16)
        s = lax.dot_general(qb, kg[:, sl].astype(BF16), NT_DIMS, preferred_element_type=F32)
        p = s * w_intra
        num = _dot(p.astype(BF16), vg[:, sl].astype(BF16))
        den = jnp.sum(p, axis=-1, keepdims=True)
        qc = jnp.zeros((GROUP, HEAD_DIM), F32)
        qn = jnp.zeros((GROUP, 1), F32)
        for sq in range(nseq):
            qc = jnp.where(seg_r == sq, _dot(qb, c0_ref[sq, h].astype(BF16)), qc)
            qn = jnp.where(seg_r[:, 0:1] == sq,
                           jnp.sum(qh * n0_ref[sq, h:h + 1, :], axis=-1, keepdims=True), qn)
        num = num + w_inter * qc
        den = den + w_inter * qn
        hh = num / jnp.maximum(jnp.abs(den), jnp.exp(-m_col))
        hh = hh * lax.rsqrt(jnp.mean(hh * hh, axis=-1, keepdims=True) + EPS) * mg_ref[:, sl]
        hs.append(jax.nn.sigmoid(og[:, sl]) * hh)
        m_all = jnp.where(lane == h, m_col, m_all)

    h_ref[...] = jnp.concatenate(hs, axis=-1).reshape(h_ref.shape)
    m_ref[...] = m_all.reshape(m_ref.shape)

    mnew = _dot_exact(lastsel, m_all)
    for h in range(N_HEADS):
        sl = slice(h * HEAD_DIM, (h + 1) * HEAD_DIM)
        last_col = cumlast[:, h:h + 1]
        mnew_col = mnew[:, h:h + 1]
        w_k = jnp.exp(last_col - cum[:, h:h + 1] + gi[:, h:h + 1] - mnew_col)
        decay = jnp.exp(last_col + mprev[:, h:h + 1] - mnew_col)
        kw = kg[:, sl] * w_k
        kwb = kw.astype(BF16)
        vh = vg[:, sl]
        for sq in range(nseq):
            vs = jnp.where(seg_r == sq, vh, 0.0)
            kws = jnp.where(seg_r == sq, kw, 0.0)
            dc = lax.dot_general(kwb, vs.astype(BF16), TN_DIMS, preferred_element_type=F32)
            dn = jnp.sum(kws, axis=0, keepdims=True)
            dec = decay[sq * ls:sq * ls + 1, :]
            c_ref[sq, h] = dec * c0_ref[sq, h] + dc
            n_ref[sq, h:h + 1, :] = dec * n0_ref[sq, h:h + 1, :] + dn


def _mlstm_sample(z3, gates3, mg, c0, n0, m_tok, *, bb):
    bsz, seq, _ = z3.shape
    assert bb * seq == GROUP
    zspec = lambda k: pl.BlockSpec((bb, seq, D_MLSTM), lambda b: (b, 0, k))
    c_spec = pl.BlockSpec((bb, N_HEADS, HEAD_DIM, HEAD_DIM), lambda b: (b, 0, 0, 0))
    n_spec = pl.BlockSpec((bb, N_HEADS, HEAD_DIM), lambda b: (b, 0, 0))
    m_spec = pl.BlockSpec((bb, seq, LANES), lambda b: (b, 0, 0))
    return pl.pallas_call(
        functools.partial(_mlstm_sample_kernel, nseq=bb, ls=seq),
        grid=(bsz // bb,),
        in_specs=[zspec(2), zspec(3), zspec(4), zspec(5),
                  pl.BlockSpec((bb, seq, D_GATE), lambda b: (b, 0, 0)),
                  pl.BlockSpec(mg.shape, lambda b: (0, 0)),
                  c_spec, n_spec, m_spec],
        out_specs=[pl.BlockSpec((bb, seq, D_MLSTM), lambda b: (b, 0, 0)), c_spec, n_spec, m_spec],
        out_shape=[
            jax.ShapeDtypeStruct((bsz, seq, D_MLSTM), F32),
            jax.ShapeDtypeStruct((bsz, N_HEADS, HEAD_DIM, HEAD_DIM), F32),
            jax.ShapeDtypeStruct((bsz, N_HEADS, HEAD_DIM), F32),
            jax.ShapeDtypeStruct((bsz, seq, LANES), F32),
        ],
        compiler_params=pltpu.CompilerParams(
            dimension_semantics=("arbitrary",), vmem_limit_bytes=VMEM_LIMIT),
        name="mlstm_sample",
    )(z3, z3, z3, z3, gates3, mg, c0, n0, m_tok)


def _trunk(x, ada3, states, p, *, ffn_a_tiles, ffn_b_tiles, conv_tiles, mlstm_tiles):
    bsz, seq, _ = x.shape
    x1, z, gates = _ffn_a(x, ada3, p["g1"], p["wgu1"], p["wd1"], p["g2"], p["win"], p["b_main"],
                          p["wg"], p["b_gate"], **ffn_a_tiles)
    z3 = z.reshape(bsz, seq, D_Z)
    gates3 = gates.reshape(bsz, seq, D_GATE)
    conv_state = None if states is None else states[0]
    y, conv_new = _conv(z3, conv_state, p["conv_w"], p["conv_b"], p["cn_g"], p["cn_b"], **conv_tiles)
    if states is None:
        h, c_new, n_new, m_row = _mlstm_prompt(z3, gates3, p["mn_g"], **mlstm_tiles)
        m_new = m_row[:, 0, :N_HEADS]
    else:
        _, c0, n0, m0 = states
        m_tok = jnp.broadcast_to(
            jnp.pad(m0, ((0, 0), (0, LANES - N_HEADS)))[:, None, :], (bsz, seq, LANES))
        h, c_new, n_new, m_tok_new = _mlstm_sample(z3, gates3, p["mn_g"], c0, n0, m_tok, **mlstm_tiles)
        m_new = m_tok_new[:, seq - 1, :N_HEADS]
    out = _ffn_b(x1, ada3, y, h, p["wout"], p["g3"], p["wgu2"], p["wd2"], p["gf"], **ffn_b_tiles)
    return out, conv_new[None], c_new[None], n_new[None], m_new[None]


def _gate_cols(a):
    pad = ((0, 0), (0, LANES - N_HEADS))
    return jnp.concatenate([jnp.pad(a[:, :N_HEADS], pad), jnp.pad(a[:, N_HEADS:], pad)], axis=1)


def kernel(x_prompt, x_sample, c_prompt, c_sample, state_conv, state_C, state_n, state_m, w_ada, b_ada, norm_ffn1, ffn1_w_gu, ffn1_w_down, norm_mix, w_in, b_in, conv_w, conv_b, conv_norm_g, conv_norm_b, mlstm_norm_g, w_out, norm_ffn2, ffn2_w_gu, ffn2_w_down, norm_final):
    assert w_ada.shape[0] == 1, "single layer"
    bp = x_prompt.shape[0]
    p = {
        "g1": norm_ffn1, "g2": norm_mix, "g3": norm_ffn2, "gf": norm_final[None],
        "wgu1": ffn1_w_gu[0].astype(BF16), "wd1": ffn1_w_down[0].astype(BF16),
        "wgu2": ffn2_w_gu[0].astype(BF16), "wd2": ffn2_w_down[0].astype(BF16),
        "win": w_in[0][:, :D_Z].astype(BF16),
        "wg": _gate_cols(w_in[0][:, D_Z:]).astype(BF16),
        "b_main": b_in[:, :D_Z],
        "b_gate": _gate_cols(b_in[:, D_Z:]),
        "conv_w": jnp.pad(conv_w[0], ((0, TAIL_PAD - CONV_WIDTH), (0, 0))),
        "conv_b": conv_b, "cn_g": conv_norm_g, "cn_b": conv_norm_b, "mn_g": mlstm_norm_g,
        "wout": w_out[0].astype(BF16),
    }
    c_all = jnp.concatenate([c_prompt, c_sample], axis=0)
    ada = _ada(c_all, w_ada[0], b_ada).reshape(c_all.shape[0], N_ADA, D_MODEL)

    yp, conv_p, c_p, n_p, m_p = _trunk(
        x_prompt, ada[:bp], None, p,
        ffn_a_tiles=dict(bb=1, tt=256), ffn_b_tiles=dict(bb=1, tt=512),
        conv_tiles=dict(bb=1, tt=512, bt=1, rt=64),
        mlstm_tiles=dict(tt=512))
    ys, conv_s, c_s, n_s, m_s = _trunk(
        x_sample, ada[bp:], (state_conv[0], state_C[0], state_n[0], state_m[0]), p,
        ffn_a_tiles=dict(bb=32, tt=8), ffn_b_tiles=dict(bb=64, tt=8),
        conv_tiles=dict(bb=32, tt=8, bt=8, rt=8),
        mlstm_tiles=dict(bb=16))
    return (yp, ys, conv_p, c_p, n_p, m_p, conv_s, c_s, n_s, m_s)
```

```python
import functools

import jax
import jax.numpy as jnp
from jax import lax
from jax.experimental import pallas as pl
from jax.experimental.pallas import tpu as pltpu

F32 = jnp.float32
BF16 = jnp.bfloat16

D_MODEL = 1024
D_CONV = 512
D_MLSTM = 512
D_MIX = D_CONV + D_MLSTM
N_HEADS = 4
HEAD_DIM = 128
CONV_WIDTH = 31
CONV_TAIL = CONV_WIDTH - 1
D_FF = 2816
N_ADA = 9
EPS = 1e-6
D_Z = 2 * D_CONV + 4 * D_MLSTM
Q_OFF = 2 * D_CONV
LANES = 128
SUBLANES = 8
D_GATE = 2 * LANES
FF_CHUNK = 256
N_FF_CHUNKS = D_FF // FF_CHUNK
GROUP = 128
TAIL_PAD = 32
TAIL_LO = TAIL_PAD - CONV_TAIL
VMEM_LIMIT = 56 * 1024 * 1024
NT_DIMS = (((1,), (1,)), ((), ()))
TN_DIMS = (((0,), (0,)), ((), ()))


def _dot(a, b):
    return jnp.dot(a, b, preferred_element_type=F32)


def _dot_exact(a, b):
    return jnp.dot(a, b, preferred_element_type=F32, precision=lax.Precision.HIGHEST)


def _rms(x, g):
    ms = jnp.mean(x * x, axis=-1, keepdims=True)
    return x * lax.rsqrt(ms + EPS) * g


def _log_sigmoid(x):
    return jnp.minimum(x, 0.0) - jnp.log(1.0 + jnp.exp(-jnp.abs(x)))


def _const_spec(shape):
    nd = len(shape)
    return pl.BlockSpec(shape, lambda *_: (0,) * nd, pipeline_mode=pl.Buffered(1))


def _params(n_grid):
    return pltpu.CompilerParams(dimension_semantics=("arbitrary",) * n_grid, vmem_limit_bytes=VMEM_LIMIT)


def _ada_kernel(c_ref, w_ref, b_ref, o_ref):
    c = c_ref[...]
    s = (c * jax.nn.sigmoid(c)).astype(BF16)
    o_ref[...] = _dot(s, w_ref[...].astype(BF16)) + b_ref[...]


def _ada(c_all, w_ada, b_ada):
    n = c_all.shape[0]
    tile = D_MODEL
    return pl.pallas_call(
        _ada_kernel,
        grid=(N_ADA,),
        in_specs=[
            pl.BlockSpec((n, D_MODEL), lambda j: (0, 0)),
            pl.BlockSpec((D_MODEL, tile), lambda j: (0, j)),
            pl.BlockSpec((1, tile), lambda j: (0, j)),
        ],
        out_specs=pl.BlockSpec((n, tile), lambda j: (0, j)),
        out_shape=jax.ShapeDtypeStruct((n, N_ADA * D_MODEL), F32),
        compiler_params=_params(1),
        name="ada",
    )(c_all, w_ada, b_ada)


def _swiglu(xm, wgu_ref, wd_ref, hid_ref, between=None):
    n = xm.shape[0]
    reps = (n // SUBLANES, FF_CHUNK // LANES)
    for j in range(N_FF_CHUNKS):
        lo = j * FF_CHUNK
        zero = None if between is None else between()
        gate = _dot(xm, wgu_ref[:, lo:lo + FF_CHUNK])
        up = _dot(xm, wgu_ref[:, D_FF + lo:D_FF + lo + FF_CHUNK])
        hidden = gate * jax.nn.sigmoid(gate) * up
        if zero is not None:
            hidden = hidden + jnp.tile(zero, reps)
        hid_ref[:, lo:lo + FF_CHUNK] = hidden.astype(BF16)
    if between is None:
        return _dot(hid_ref[...], wd_ref[...])
    pieces = []
    for lo in range(0, D_MODEL, FF_CHUNK):
        zero = between()
        piece = _dot(hid_ref[...], wd_ref[:, lo:lo + FF_CHUNK])
        pieces.append(piece if zero is None else piece + jnp.tile(zero, reps))
    return jnp.concatenate(pieces, axis=1)


def _ffn1_inproj(x, ada, g1_ref, wgu_ref, wd_ref, g2_ref, win_ref, bin_ref, wg_ref, bg_ref, hid_ref,
                 between=None):
    bb, tt, _ = x.shape
    n = bb * tt
    xm = _rms(x, g1_ref[...]) * (1.0 + ada[:, 1:2, :]) + ada[:, 0:1, :]
    ff = _swiglu(xm.reshape(n, D_MODEL).astype(BF16), wgu_ref, wd_ref, hid_ref, between)
    x1 = x + 0.5 * ada[:, 2:3, :] * ff.reshape(bb, tt, D_MODEL)
    hm = _rms(x1, g2_ref[...]) * (1.0 + ada[:, 4:5, :]) + ada[:, 3:4, :]
    hm = hm.reshape(n, D_MODEL).astype(BF16)
    return x1, _dot(hm, win_ref[...]) + bin_ref[...], _dot(hm, wg_ref[...]) + bg_ref[...]


def _run(steps):
    for _ in steps:
        pass


def _all_bits(*arrays):
    acc = None
    for a in arrays:
        bits = lax.bitcast_convert_type(a.reshape(-1, a.shape[-1]), jnp.uint32)
        for r in range(0, bits.shape[0], SUBLANES):
            for c in range(0, bits.shape[1], LANES):
                v = bits[r:r + SUBLANES, c:c + LANES]
                acc = v if acc is None else acc | v
    return acc


def _conv_group(u, ubuf, ush, w_ref, cb_ref, lg_ref, lb_ref, store_y, store_tail, *, bt, rt):
    bb, tt, _ = u.shape
    sh_rows = TAIL_PAD - SUBLANES + tt
    ubuf[:, TAIL_PAD:TAIL_PAD + tt, :] = u
    yield None
    for r in range(1, SUBLANES):
        ush[r - 1] = ubuf[:, r:r + sh_rows, :]
        yield None
    for b0 in range(0, bb, bt):
        for r0 in range(0, tt, rt):
            acc = jnp.broadcast_to(cb_ref[...].reshape(1, 1, D_CONV), (bt, rt, D_CONV))
            for j in range(CONV_WIDTH):
                q, r = divmod(TAIL_LO + j, SUBLANES)
                p0 = r0 + q * SUBLANES
                if r == 0:
                    win = ubuf[b0:b0 + bt, p0:p0 + rt, :]
                else:
                    win = ush[r - 1, b0:b0 + bt, p0:p0 + rt, :]
                acc = acc + w_ref[j:j + 1, :].reshape(1, 1, D_CONV) * win
            mu = jnp.mean(acc, axis=-1, keepdims=True)
            xc = acc - mu
            var = jnp.mean(xc * xc, axis=-1, keepdims=True)
            yn = xc * lax.rsqrt(var + EPS) * lg_ref[...].reshape(1, 1, D_CONV) + lb_ref[...].reshape(1, 1, D_CONV)
            out = yn * jax.nn.sigmoid(yn)
            store_y(b0, r0, out)
            yield _all_bits(out)
    tail = ubuf[:, TAIL_LO + tt:TAIL_PAD + tt, :]
    ubuf[:, TAIL_LO:TAIL_PAD, :] = tail
    store_tail(tail)


def _scan_rows(x, op, ident):
    row = lax.broadcasted_iota(jnp.int32, x.shape, 0)
    s = 1
    while s < x.shape[0]:
        x = op(x, jnp.where(row >= s, pltpu.roll(x, s, axis=0), ident))
        s *= 2
    return x


def _mlstm_group(gi, gf, load_z, store_h, mg_ref, cext, mcar):
    row = lax.broadcasted_iota(jnp.int32, (GROUP, GROUP), 0)
    col = lax.broadcasted_iota(jnp.int32, (GROUP, GROUP), 1)
    causal = col <= row
    ones_b = jnp.ones((GROUP, HEAD_DIM), BF16)
    cum = _scan_rows(_log_sigmoid(gf), jnp.add, 0.0)
    gv = gi - cum
    mx = _scan_rows(gv, jnp.maximum, -jnp.inf)
    mprev = mcar[...]
    mm = jnp.maximum(mprev, mx)
    mcol = cum + mm
    mm_last = mm[GROUP - 1:GROUP, :]
    w_k = jnp.exp(gv - mm_last)
    decay = jnp.exp(mprev - mm_last)
    mcar[...] = mcol[GROUP - 1:GROUP, :]
    gv_t = gv.T
    yield _all_bits(gv_t[0:SUBLANES, :])
    for h in range(N_HEADS):
        sl = slice(h * HEAD_DIM, (h + 1) * HEAD_DIM)
        mm_b = jnp.broadcast_to(mm[:, h:h + 1], (GROUP, GROUP))
        mcol_b = jnp.broadcast_to(mcol[:, h:h + 1], (GROUP, GROUP))
        w_intra = jnp.exp(jnp.where(causal, gv_t[h:h + 1, :] - mm_b, -jnp.inf))
        w_inter = jnp.exp(mprev[:, h:h + 1] - mm_b)
        qb = load_z(0, h).astype(BF16)
        kh = load_z(1, h) * (HEAD_DIM ** -0.5)
        v_ext = jnp.concatenate([load_z(2, h).astype(BF16), ones_b], axis=1)
        s = lax.dot_general(qb, kh.astype(BF16), NT_DIMS, preferred_element_type=F32)
        pv = _dot((s * w_intra).astype(BF16), v_ext)
        cprev = cext[h]
        qc = _dot(qb, cprev.astype(BF16))
        num = pv[:, :HEAD_DIM] + w_inter * qc[:, :HEAD_DIM]
        den = pv[:, HEAD_DIM:] + w_inter * qc[:, HEAD_DIM:]
        hh = num / jnp.maximum(jnp.abs(den), jnp.exp(-mcol_b))
        hh = hh * lax.rsqrt(jnp.mean(hh * hh, axis=-1, keepdims=True) + EPS) * mg_ref[:, sl]
        out = jax.nn.sigmoid(load_z(3, h)) * hh
        store_h(h, out)
        kw = (kh * w_k[:, h:h + 1]).astype(BF16)
        dc = lax.dot_general(kw, v_ext, TN_DIMS, preferred_element_type=F32)
        cnew = decay[:, h:h + 1] * cprev + dc
        cext[h] = cnew
        yield _all_bits(out, cnew)


def _front_kernel(x_ref, ada_ref, g1_ref, wgu_ref, wd_ref, g2_ref, win_ref, bin_ref, wg_ref, bg_ref,
                  cw_ref, cb_ref, lg_ref, lb_ref, mg_ref,
                  x1_ref, yh_ref, cs_ref, c_ref, n_ref, m_ref,
                  hid_ref, zbuf, gbuf, ubuf, ush, cext, mcar, *, tt, nt, rt):
    s = pl.program_id(0)
    tl = lax.rem(jnp.maximum(s - 1, 0), nt)

    @pl.when(s == 0)
    def _():
        zbuf[...] = jnp.zeros(zbuf.shape, F32)
        gbuf[...] = jnp.zeros(gbuf.shape, F32)

    @pl.when(tl == 0)
    def _():
        ubuf[:, 0:TAIL_PAD, :] = jnp.zeros((1, TAIL_PAD, D_CONV), F32)
        cext[...] = jnp.zeros(cext.shape, F32)
        mcar[...] = jnp.zeros(mcar.shape, F32)

    def store_y(b0, r0, val):
        yh_ref[:, r0:r0 + rt, 0:D_CONV] = val.astype(BF16)

    def mlstm_pieces(g):
        rs = slice(g * GROUP, (g + 1) * GROUP)

        def load_z(which, h):
            c0 = Q_OFF + which * D_MLSTM + h * HEAD_DIM
            return zbuf[0, rs, c0:c0 + HEAD_DIM]

        def store_h(h, val):
            yh_ref[0, rs, D_CONV + h * HEAD_DIM:D_CONV + (h + 1) * HEAD_DIM] = val.astype(BF16)

        return _mlstm_group(gbuf[rs, 0:LANES], gbuf[rs, LANES:D_GATE], load_z, store_h, mg_ref, cext, mcar)

    def mixer_pieces():
        u = zbuf[:, :, 0:D_CONV] * jax.nn.sigmoid(zbuf[:, :, D_CONV:2 * D_CONV])
        yield from _conv_group(u, ubuf, ush, cw_ref, cb_ref, lg_ref, lb_ref, store_y, lambda tail: None,
                               bt=1, rt=rt)
        for g in range(tt // GROUP):
            yield from mlstm_pieces(g)

    pieces = mixer_pieces()
    costs = ([8] + [5] * (SUBLANES - 1) + [3 * rt // SUBLANES] * (tt // rt)
             + ([10] + [5] * N_HEADS) * (tt // GROUP))
    n_slots = N_FF_CHUNKS + D_MODEL // FF_CHUNK
    done = [0, 0]

    def between():
        done[0] += 1
        target = sum(costs) * done[0] / n_slots
        bits = None
        while done[1] < len(costs) and sum(costs[:done[1]]) + costs[done[1]] / 2 <= target:
            piece_bits = next(pieces)
            if piece_bits is not None:
                bits = piece_bits if bits is None else bits | piece_bits
            done[1] += 1
        if bits is None:
            return None
        zero_bits = lax.shift_right_logical(lax.shift_right_logical(bits, jnp.uint32(16)), jnp.uint32(16))
        return lax.bitcast_convert_type(zero_bits, F32)

    x1, z, gates = _ffn1_inproj(x_ref[...], ada_ref[...], g1_ref, wgu_ref, wd_ref, g2_ref,
                                win_ref, bin_ref, wg_ref, bg_ref, hid_ref, between)
    _run(pieces)
    x1_ref[...] = x1
    zbuf[0] = z
    gbuf[...] = gates

    @pl.when(tl == nt - 1)
    def _():
        cs_ref[...] = ubuf[:, TAIL_LO:TAIL_PAD, :]
        for h in range(N_HEADS):
            c_ref[0, h] = cext[h, :, :HEAD_DIM]
            n_ref[0, h:h + 1, :] = cext[h, :, HEAD_DIM:].T[0:1, :]
        m_ref[0] = mcar[...]


def _front(x, ada3, p, *, tt, rt):
    bsz, seq, _ = x.shape
    nt = seq // tt
    steps = bsz * nt
    cur = lambda s: jnp.minimum(s, steps - 1)
    lag = lambda s: jnp.maximum(s - 1, 0)
    weights = [p["g1"], p["wgu1"], p["wd1"], p["g2"], p["win"], p["b_main"], p["wg"], p["b_gate"],
               p["conv_w"], p["conv_b"], p["cn_g"], p["cn_b"], p["mn_g"]]
    return pl.pallas_call(
        functools.partial(_front_kernel, tt=tt, nt=nt, rt=rt),
        grid=(steps + 1,),
        in_specs=[
            pl.BlockSpec((1, tt, D_MODEL), lambda s: (cur(s) // nt, cur(s) % nt, 0)),
            pl.BlockSpec((1, N_ADA, D_MODEL), lambda s: (cur(s) // nt, 0, 0)),
        ] + [_const_spec(w.shape) for w in weights],
        out_specs=[
            pl.BlockSpec((1, tt, D_MODEL), lambda s: (cur(s) // nt, cur(s) % nt, 0)),
            pl.BlockSpec((1, tt, D_MIX), lambda s: (lag(s) // nt, lag(s) % nt, 0)),
            pl.BlockSpec((1, CONV_TAIL, D_CONV), lambda s: (lag(s) // nt, 0, 0)),
            pl.BlockSpec((1, N_HEADS, HEAD_DIM, HEAD_DIM), lambda s: (lag(s) // nt, 0, 0, 0)),
            pl.BlockSpec((1, N_HEADS, HEAD_DIM), lambda s: (lag(s) // nt, 0, 0)),
            pl.BlockSpec((1, 1, LANES), lambda s: (lag(s) // nt, 0, 0)),
        ],
        out_shape=[
            jax.ShapeDtypeStruct(x.shape, F32),
            jax.ShapeDtypeStruct((bsz, seq, D_MIX), BF16),
            jax.ShapeDtypeStruct((bsz, CONV_TAIL, D_CONV), F32),
            jax.ShapeDtypeStruct((bsz, N_HEADS, HEAD_DIM, HEAD_DIM), F32),
            jax.ShapeDtypeStruct((bsz, N_HEADS, HEAD_DIM), F32),
            jax.ShapeDtypeStruct((bsz, 1, LANES), F32),
        ],
        scratch_shapes=[
            pltpu.VMEM((tt, D_FF), BF16),
            pltpu.VMEM((1, tt, D_Z), F32),
            pltpu.VMEM((tt, D_GATE), F32),
            pltpu.VMEM((1, TAIL_PAD + tt, D_CONV), F32),
            pltpu.VMEM((SUBLANES - 1, 1, TAIL_PAD - SUBLANES + tt, D_CONV), F32),
            pltpu.VMEM((N_HEADS, HEAD_DIM, 2 * HEAD_DIM), F32),
            pltpu.VMEM((1, LANES), F32),
        ],
        compiler_params=_params(1),
        name="front",
    )(x, ada3, *weights)


def _ffn_a_kernel(x_ref, ada_ref, g1_ref, wgu_ref, wd_ref, g2_ref, win_ref, bin_ref, wg_ref, bg_ref,
                  x1_ref, z_ref, gate_ref, hid_ref):
    x1, z, gates = _ffn1_inproj(x_ref[...], ada_ref[...], g1_ref, wgu_ref, wd_ref, g2_ref,
                                win_ref, bin_ref, wg_ref, bg_ref, hid_ref)
    x1_ref[...] = x1
    z_ref[...] = z
    gate_ref[...] = gates


def _ffn_a(x, ada3, p, *, bb):
    bsz, seq, _ = x.shape
    n = bb * seq
    weights = [p["g1"], p["wgu1"], p["wd1"], p["g2"], p["win"], p["b_main"], p["wg"], p["b_gate"]]
    return pl.pallas_call(
        _ffn_a_kernel,
        grid=(bsz // bb,),
        in_specs=[
            pl.BlockSpec((bb, seq, D_MODEL), lambda b: (b, 0, 0)),
            pl.BlockSpec((bb, N_ADA, D_MODEL), lambda b: (b, 0, 0)),
        ] + [_const_spec(w.shape) for w in weights],
        out_specs=[
            pl.BlockSpec((bb, seq, D_MODEL), lambda b: (b, 0, 0)),
            pl.BlockSpec((n, D_Z), lambda b: (b, 0)),
            pl.BlockSpec((n, D_GATE), lambda b: (b, 0)),
        ],
        out_shape=[
            jax.ShapeDtypeStruct(x.shape, F32),
            jax.ShapeDtypeStruct((bsz * seq, D_Z), F32),
            jax.ShapeDtypeStruct((bsz * seq, D_GATE), F32),
        ],
        scratch_shapes=[pltpu.VMEM((n, D_FF), BF16)],
        compiler_params=_params(1),
        name="ffn_a",
    )(x, ada3, *weights)


def _conv_kernel(a_ref, b_ref, cs_ref, w_ref, cb_ref, lg_ref, lb_ref, y_ref, so_ref, ubuf, ush, *, bt, rt):
    ubuf[:, TAIL_LO:TAIL_PAD, :] = cs_ref[...]

    def store_y(b0, r0, val):
        y_ref[b0:b0 + bt, r0:r0 + rt, :] = val

    def store_tail(tail):
        so_ref[...] = tail

    u = a_ref[...] * jax.nn.sigmoid(b_ref[...])
    _run(_conv_group(u, ubuf, ush, w_ref, cb_ref, lg_ref, lb_ref, store_y, store_tail, bt=bt, rt=rt))


def _conv(z3, state, p, *, bb, bt):
    bsz, seq, _ = z3.shape
    weights = [p["conv_w"], p["conv_b"], p["cn_g"], p["cn_b"]]
    return pl.pallas_call(
        functools.partial(_conv_kernel, bt=bt, rt=seq),
        grid=(bsz // bb,),
        in_specs=[
            pl.BlockSpec((bb, seq, D_CONV), lambda b: (b, 0, 0)),
            pl.BlockSpec((bb, seq, D_CONV), lambda b: (b, 0, 1)),
            pl.BlockSpec((bb, CONV_TAIL, D_CONV), lambda b: (b, 0, 0)),
        ] + [_const_spec(w.shape) for w in weights],
        out_specs=[
            pl.BlockSpec((bb, seq, D_CONV), lambda b: (b, 0, 0)),
            pl.BlockSpec((bb, CONV_TAIL, D_CONV), lambda b: (b, 0, 0)),
        ],
        out_shape=[
            jax.ShapeDtypeStruct((bsz, seq, D_CONV), F32),
            jax.ShapeDtypeStruct((bsz, CONV_TAIL, D_CONV), F32),
        ],
        scratch_shapes=[
            pltpu.VMEM((bb, TAIL_PAD + seq, D_CONV), F32),
            pltpu.VMEM((SUBLANES - 1, bb, TAIL_PAD - SUBLANES + seq, D_CONV), F32),
        ],
        compiler_params=_params(1),
        name="conv",
    )(z3, z3, state, *weights)


def _mlstm_sample_kernel(q_ref, k_ref, v_ref, o_ref, gt_ref, mg_ref, c0_ref, n0_ref, m0_ref,
                         h_ref, c_ref, n_ref, m_ref, *, nseq, ls):
    row = lax.broadcasted_iota(jnp.int32, (GROUP, GROUP), 0)
    col = lax.broadcasted_iota(jnp.int32, (GROUP, GROUP), 1)
    shift = ls.bit_length() - 1
    seg_r = lax.shift_right_logical(row, shift)
    seg_c = lax.shift_right_logical(col, shift)
    causal = jnp.logical_and(col <= row, seg_r == seg_c)
    segtril = causal.astype(F32)
    lastsel = (col == seg_r * ls + (ls - 1)).astype(F32)

    gates = gt_ref[...].reshape(GROUP, D_GATE)
    gi = gates[:, :LANES]
    cum = _dot_exact(segtril, _log_sigmoid(gates[:, LANES:]))
    cumlast = _dot_exact(lastsel, cum)
    cum_t = cum.T
    gi_t = gi.T
    mprev = m0_ref[...].reshape(GROUP, LANES)
    qg = q_ref[...].reshape(GROUP, D_MLSTM)
    kg = k_ref[...].reshape(GROUP, D_MLSTM) * (HEAD_DIM ** -0.5)
    vg = v_ref[...].reshape(GROUP, D_MLSTM)
    og = o_ref[...].reshape(GROUP, D_MLSTM)

    m_all = jnp.zeros((GROUP, LANES), F32)
    lane = lax.broadcasted_iota(jnp.int32, (GROUP, LANES), 1)
    hs = []
    for h in range(N_HEADS):
        sl = slice(h * HEAD_DIM, (h + 1) * HEAD_DIM)
        cum_col = cum[:, h:h + 1]
        dmat = jnp.where(causal, cum_col - cum_t[h:h + 1, :] + gi_t[h:h + 1, :], -jnp.inf)
        inter = cum_col + mprev[:, h:h + 1]
        m_col = jnp.maximum(inter, jnp.max(dmat, axis=-1, keepdims=True))
        w_intra = jnp.exp(dmat - m_col)
        w_inter = jnp.exp(inter - m_col)
        qh = qg[:, sl]
        qb = qh.astype(BF16)
        s = lax.dot_general(qb, kg[:, sl].astype(BF16), NT_DIMS, preferred_element_type=F32)
        p = s * w_intra
        num = _dot(p.astype(BF16), vg[:, sl].astype(BF16))
        den = jnp.sum(p, axis=-1, keepdims=True)
        qc = jnp.zeros((GROUP, HEAD_DIM), F32)
        qn = jnp.zeros((GROUP, 1), F32)
        for sq in range(nseq):
            qc = jnp.where(seg_r == sq, _dot(qb, c0_ref[sq, h].astype(BF16)), qc)
            qn = jnp.where(seg_r[:, 0:1] == sq,
                           jnp.sum(qh * n0_ref[sq, h:h + 1, :], axis=-1, keepdims=True), qn)
        num = num + w_inter * qc
        den = den + w_inter * qn
        hh = num / jnp.maximum(jnp.abs(den), jnp.exp(-m_col))
        hh = hh * lax.rsqrt(jnp.mean(hh * hh, axis=-1, keepdims=True) + EPS) * mg_ref[:, sl]
        hs.append(jax.nn.sigmoid(og[:, sl]) * hh)
        m_all = jnp.where(lane == h, m_col, m_all)

    h_ref[...] = jnp.concatenate(hs, axis=-1).reshape(h_ref.shape)
    m_ref[...] = m_all.reshape(m_ref.shape)

    mnew = _dot_exact(lastsel, m_all)
    for h in range(N_HEADS):
        sl = slice(h * HEAD_DIM, (h + 1) * HEAD_DIM)
        last_col = cumlast[:, h:h + 1]
        mnew_col = mnew[:, h:h + 1]
        w_k = jnp.exp(last_col - cum[:, h:h + 1] + gi[:, h:h + 1] - mnew_col)
        decay = jnp.exp(last_col + mprev[:, h:h + 1] - mnew_col)
        kw = kg[:, sl] * w_k
        kwb = kw.astype(BF16)
        vh = vg[:, sl]
        for sq in range(nseq):
            vs = jnp.where(seg_r == sq, vh, 0.0)
            kws = jnp.where(seg_r == sq, kw, 0.0)
            dc = lax.dot_general(kwb, vs.astype(BF16), TN_DIMS, preferred_element_type=F32)
            dn = jnp.sum(kws, axis=0, keepdims=True)
            dec = decay[sq * ls:sq * ls + 1, :]
            c_ref[sq, h] = dec * c0_ref[sq, h] + dc
            n_ref[sq, h:h + 1, :] = dec * n0_ref[sq, h:h + 1, :] + dn


def _mlstm_sample(z3, gates3, mg, c0, n0, m_tok, *, bb):
    bsz, seq, _ = z3.shape
    assert bb * seq == GROUP
    zspec = lambda k: pl.BlockSpec((bb, seq, D_MLSTM), lambda b: (b, 0, k))
    c_spec = pl.BlockSpec((bb, N_HEADS, HEAD_DIM, HEAD_DIM), lambda b: (b, 0, 0, 0))
    n_spec = pl.BlockSpec((bb, N_HEADS, HEAD_DIM), lambda b: (b, 0, 0))
    m_spec = pl.BlockSpec((bb, seq, LANES), lambda b: (b, 0, 0))
    return pl.pallas_call(
        functools.partial(_mlstm_sample_kernel, nseq=bb, ls=seq),
        grid=(bsz // bb,),
        in_specs=[zspec(2), zspec(3), zspec(4), zspec(5),
                  pl.BlockSpec((bb, seq, D_GATE), lambda b: (b, 0, 0)),
                  _const_spec(mg.shape),
                  c_spec, n_spec, m_spec],
        out_specs=[pl.BlockSpec((bb, seq, D_MLSTM), lambda b: (b, 0, 0)), c_spec, n_spec, m_spec],
        out_shape=[
            jax.ShapeDtypeStruct((bsz, seq, D_MLSTM), F32),
            jax.ShapeDtypeStruct((bsz, N_HEADS, HEAD_DIM, HEAD_DIM), F32),
            jax.ShapeDtypeStruct((bsz, N_HEADS, HEAD_DIM), F32),
            jax.ShapeDtypeStruct((bsz, seq, LANES), F32),
        ],
        compiler_params=_params(1),
        name="mlstm_sample",
    )(z3, z3, z3, z3, gates3, mg, c0, n0, m_tok)


def _ffn_b_kernel(*refs, split_mix):
    if split_mix:
        x_ref, ada_ref, y_ref, h_ref, wout_ref, g3_ref, wgu_ref, wd_ref, gf_ref, o_ref, hid_ref = refs
    else:
        x_ref, ada_ref, yh_ref, wout_ref, g3_ref, wgu_ref, wd_ref, gf_ref, o_ref, hid_ref = refs
    x = x_ref[...]
    bb, tt, _ = x.shape
    n = bb * tt
    ada = ada_ref[...]
    if split_mix:
        mix = (_dot(y_ref[...].reshape(n, D_CONV).astype(BF16), wout_ref[:D_CONV, :])
               + _dot(h_ref[...].reshape(n, D_MLSTM).astype(BF16), wout_ref[D_CONV:, :]))
    else:
        mix = _dot(yh_ref[...].reshape(n, D_MIX), wout_ref[...])
    x2 = x + ada[:, 5:6, :] * mix.reshape(bb, tt, D_MODEL)
    xm = _rms(x2, g3_ref[...]) * (1.0 + ada[:, 7:8, :]) + ada[:, 6:7, :]
    ff = _swiglu(xm.reshape(n, D_MODEL).astype(BF16), wgu_ref, wd_ref, hid_ref)
    x3 = x2 + 0.5 * ada[:, 8:9, :] * ff.reshape(bb, tt, D_MODEL)
    o_ref[...] = _rms(x3, gf_ref[...])


def _ffn_b(x, ada3, mix_in, p, *, bb, tt):
    bsz, seq, _ = x.shape
    split_mix = isinstance(mix_in, tuple)
    tok = lambda w: pl.BlockSpec((bb, tt, w), lambda b, t: (b, t, 0))
    mix_specs = [tok(D_CONV), tok(D_MLSTM)] if split_mix else [tok(D_MIX)]
    mix_args = list(mix_in) if split_mix else [mix_in]
    weights = [p["wout"], p["g3"], p["wgu2"], p["wd2"], p["gf"]]
    return pl.pallas_call(
        functools.partial(_ffn_b_kernel, split_mix=split_mix),
        grid=(bsz // bb, seq // tt),
        in_specs=[tok(D_MODEL), pl.BlockSpec((bb, N_ADA, D_MODEL), lambda b, t: (b, 0, 0))]
        + mix_specs + [_const_spec(w.shape) for w in weights],
        out_specs=tok(D_MODEL),
        out_shape=jax.ShapeDtypeStruct(x.shape, F32),
        scratch_shapes=[pltpu.VMEM((bb * tt, D_FF), BF16)],
        compiler_params=_params(2),
        name="ffn_b",
    )(x, ada3, *mix_args, *weights)


def _prompt_trunk(x, ada3, p):
    x1, yh, conv_new, c_new, n_new, m_row = _front(x, ada3, p, tt=256, rt=32)
    out = _ffn_b(x1, ada3, yh, p, bb=1, tt=512)
    return out, conv_new[None], c_new[None], n_new[None], m_row[None, :, 0, :N_HEADS]


def _sample_trunk(x, ada3, states, p):
    bsz, seq, _ = x.shape
    conv_state, c0, n0, m0 = states
    x1, z, gates = _ffn_a(x, ada3, p, bb=32)
    z3 = z.reshape(bsz, seq, D_Z)
    gates3 = gates.reshape(bsz, seq, D_GATE)
    y, conv_new = _conv(z3, conv_state, p, bb=32, bt=8)
    m_tok = jnp.broadcast_to(
        jnp.pad(m0, ((0, 0), (0, LANES - N_HEADS)))[:, None, :], (bsz, seq, LANES))
    h, c_new, n_new, m_tok_new = _mlstm_sample(z3, gates3, p["mn_g"], c0, n0, m_tok, bb=GROUP // seq)
    out = _ffn_b(x1, ada3, (y, h), p, bb=64, tt=seq)
    return out, conv_new[None], c_new[None], n_new[None], m_tok_new[None, :, seq - 1, :N_HEADS]


def _gate_cols(a):
    pad = ((0, 0), (0, LANES - N_HEADS))
    return jnp.concatenate([jnp.pad(a[:, :N_HEADS], pad), jnp.pad(a[:, N_HEADS:], pad)], axis=1)


def kernel(x_prompt, x_sample, c_prompt, c_sample, state_conv, state_C, state_n, state_m, w_ada, b_ada, norm_ffn1, ffn1_w_gu, ffn1_w_down, norm_mix, w_in, b_in, conv_w, conv_b, conv_norm_g, conv_norm_b, mlstm_norm_g, w_out, norm_ffn2, ffn2_w_gu, ffn2_w_down, norm_final):
    assert w_ada.shape[0] == 1, "single layer"
    bp = x_prompt.shape[0]
    p = {
        "g1": norm_ffn1, "g2": norm_mix, "g3": norm_ffn2, "gf": norm_final[None],
        "wgu1": ffn1_w_gu[0].astype(BF16), "wd1": ffn1_w_down[0].astype(BF16),
        "wgu2": ffn2_w_gu[0].astype(BF16), "wd2": ffn2_w_down[0].astype(BF16),
        "win": w_in[0][:, :D_Z].astype(BF16),
        "wg": _gate_cols(w_in[0][:, D_Z:]).astype(BF16),
        "b_main": b_in[:, :D_Z],
        "b_gate": _gate_cols(b_in[:, D_Z:]),
        "conv_w": jnp.pad(conv_w[0], ((0, TAIL_PAD - CONV_WIDTH), (0, 0))),
        "conv_b": conv_b, "cn_g": conv_norm_g, "cn_b": conv_norm_b, "mn_g": mlstm_norm_g,
        "wout": w_out[0].astype(BF16),
    }
    c_all = jnp.concatenate([c_prompt, c_sample], axis=0)
    ada = _ada(c_all, w_ada[0], b_ada).reshape(c_all.shape[0], N_ADA, D_MODEL)
    yp, conv_p, c_p, n_p, m_p = _prompt_trunk(x_prompt, ada[:bp], p)
    ys, conv_s, c_s, n_s, m_s = _sample_trunk(
        x_sample, ada[bp:], (state_conv[0], state_C[0], state_n[0], state_m[0]), p)
    return (yp, ys, conv_p, c_p, n_p, m_p, conv_s, c_s, n_s, m_s)
```

```python
import functools

import jax
import jax.numpy as jnp
from jax import lax
from jax.experimental import pallas as pl
from jax.experimental.pallas import tpu as pltpu

F32 = jnp.float32
BF16 = jnp.bfloat16

D_MODEL = 1024
D_CONV = 512
D_MLSTM = 512
N_HEADS = 4
HEAD_DIM = 128
CONV_WIDTH = 31
CONV_TAIL = CONV_WIDTH - 1
D_FF = 2816
N_ADA = 9
EPS = 1e-6
D_Z = 2 * D_CONV + 4 * D_MLSTM
Q_OFF = 2 * D_CONV
LANES = 128
SUBLANES = 8
D_GATE = 2 * LANES
FF_CHUNK = 256
N_FF_CHUNKS = D_FF // FF_CHUNK
GROUP = 128
TAIL_PAD = 32
TAIL_LO = TAIL_PAD - CONV_TAIL
TIE_PARTS = 4
VMEM_LIMIT = 56 * 1024 * 1024
NT_DIMS = (((1,), (1,)), ((), ()))
TN_DIMS = (((0,), (0,)), ((), ()))


def _dot(a, b):
    return jnp.dot(a, b, preferred_element_type=F32)


def _dot_exact(a, b):
    return jnp.dot(a, b, preferred_element_type=F32, precision=lax.Precision.HIGHEST)


def _rms(x, g):
    ms = jnp.mean(x * x, axis=-1, keepdims=True)
    return x * lax.rsqrt(ms + EPS) * g


def _log_sigmoid(x):
    return jnp.minimum(x, 0.0) - jnp.log(1.0 + jnp.exp(-jnp.abs(x)))


def _const_spec(shape):
    nd = len(shape)
    return pl.BlockSpec(shape, lambda *_: (0,) * nd, pipeline_mode=pl.Buffered(1))


def _params(n_grid):
    return pltpu.CompilerParams(dimension_semantics=("arbitrary",) * n_grid, vmem_limit_bytes=VMEM_LIMIT)


def _ada_kernel(c_ref, w_ref, b_ref, o_ref):
    c = c_ref[...]
    s = (c * jax.nn.sigmoid(c)).astype(BF16)
    o_ref[...] = _dot(s, w_ref[...].astype(BF16)) + b_ref[...]


def _ada(c_all, w_ada, b_ada):
    n = c_all.shape[0]
    tile = D_MODEL
    return pl.pallas_call(
        _ada_kernel,
        grid=(N_ADA,),
        in_specs=[
            pl.BlockSpec((n, D_MODEL), lambda j: (0, 0)),
            pl.BlockSpec((D_MODEL, tile), lambda j: (0, j)),
            pl.BlockSpec((1, tile), lambda j: (0, j)),
        ],
        out_specs=pl.BlockSpec((n, tile), lambda j: (0, j)),
        out_shape=jax.ShapeDtypeStruct((n, N_ADA * D_MODEL), F32),
        compiler_params=_params(1),
        name="ada",
    )(c_all, w_ada, b_ada)


def _tie(value, zeros):
    rb = value.shape[0] // TIE_PARTS
    reps = (rb // SUBLANES, value.shape[1] // LANES)
    blocks = [value[k * rb:(k + 1) * rb] for k in range(TIE_PARTS)]
    blocks = [b if z is None else b + jnp.tile(z, reps) for b, z in zip(blocks, zeros)]
    return jnp.concatenate(blocks, axis=0)


def _swiglu(xm, wgu_ref, wd_ref, hid_ref, between=None):
    for j in range(N_FF_CHUNKS):
        lo = j * FF_CHUNK
        zeros = None if between is None else between()
        gate = _dot(xm, wgu_ref[:, lo:lo + FF_CHUNK])
        up = _dot(xm, wgu_ref[:, D_FF + lo:D_FF + lo + FF_CHUNK])
        hidden = gate * jax.nn.sigmoid(gate) * up
        if zeros is not None:
            hidden = _tie(hidden, zeros)
        hid_ref[:, lo:lo + FF_CHUNK] = hidden.astype(BF16)
    return _dot(hid_ref[...], wd_ref[...])


def _ffn1_inproj(x, ada, g1_ref, wgu_ref, wd_ref, g2_ref, win_ref, bin_ref, wg_ref, bg_ref, hid_ref,
                 store_z, between=None):
    bb, tt, _ = x.shape
    n = bb * tt
    xm = _rms(x, g1_ref[...]) * (1.0 + ada[:, 1:2, :]) + ada[:, 0:1, :]
    ff = _swiglu(xm.reshape(n, D_MODEL).astype(BF16), wgu_ref, wd_ref, hid_ref, between)
    x1 = x + 0.5 * ada[:, 2:3, :] * ff.reshape(bb, tt, D_MODEL)
    hm = _rms(x1, g2_ref[...]) * (1.0 + ada[:, 4:5, :]) + ada[:, 3:4, :]
    hm = hm.reshape(n, D_MODEL).astype(BF16)
    for c0 in range(0, D_Z, D_CONV):
        store_z(c0, _dot(hm, win_ref[:, c0:c0 + D_CONV]) + bin_ref[:, c0:c0 + D_CONV])
    return x1, _dot(hm, wg_ref[...]) + bg_ref[...]


def _run(steps):
    for _ in steps:
        pass


def _all_bits(*arrays):
    acc = None
    for a in arrays:
        bits = lax.bitcast_convert_type(a.reshape(-1, a.shape[-1]), jnp.uint32)
        for r in range(0, bits.shape[0], SUBLANES):
            for c in range(0, bits.shape[1], LANES):
                v = bits[r:r + SUBLANES, c:c + LANES]
                acc = v if acc is None else acc | v
    return acc


def _conv_group(u, ubuf, ush, w_ref, cb_ref, lg_ref, lb_ref, store_y, store_tail, *, bt, rt):
    bb, tt, _ = u.shape
    sh_rows = TAIL_PAD - SUBLANES + tt
    ubuf[:, TAIL_PAD:TAIL_PAD + tt, :] = u
    yield None
    for r in range(1, SUBLANES):
        ush[r - 1] = ubuf[:, r:r + sh_rows, :]
        yield None
    for b0 in range(0, bb, bt):
        for r0 in range(0, tt, rt):
            acc = jnp.broadcast_to(cb_ref[...].reshape(1, 1, D_CONV), (bt, rt, D_CONV))
            for j in range(CONV_WIDTH):
                q, r = divmod(TAIL_LO + j, SUBLANES)
                p0 = r0 + q * SUBLANES
                if r == 0:
                    win = ubuf[b0:b0 + bt, p0:p0 + rt, :]
                else:
                    win = ush[r - 1, b0:b0 + bt, p0:p0 + rt, :]
                acc = acc + w_ref[j:j + 1, :].reshape(1, 1, D_CONV) * win
            mu = jnp.mean(acc, axis=-1, keepdims=True)
            xc = acc - mu
            var = jnp.mean(xc * xc, axis=-1, keepdims=True)
            yn = xc * lax.rsqrt(var + EPS) * lg_ref[...].reshape(1, 1, D_CONV) + lb_ref[...].reshape(1, 1, D_CONV)
            out = yn * jax.nn.sigmoid(yn)
            store_y(b0, r0, out)
            yield _all_bits(out)
    tail = ubuf[:, TAIL_LO + tt:TAIL_PAD + tt, :]
    ubuf[:, TAIL_LO:TAIL_PAD, :] = tail
    store_tail(tail)


def _scan_rows(x, op, ident):
    row = lax.broadcasted_iota(jnp.int32, x.shape, 0)
    s = 1
    while s < x.shape[0]:
        x = op(x, jnp.where(row >= s, pltpu.roll(x, s, axis=0), ident))
        s *= 2
    return x


def _mlstm_group(gi, gf, load_z, store_h, mg_ref, cext, mcar):
    row = lax.broadcasted_iota(jnp.int32, (GROUP, GROUP), 0)
    col = lax.broadcasted_iota(jnp.int32, (GROUP, GROUP), 1)
    causal = col <= row
    ones_b = jnp.ones((GROUP, HEAD_DIM), BF16)
    cum = _scan_rows(_log_sigmoid(gf), jnp.add, 0.0)
    gv = gi - cum
    mx = _scan_rows(gv, jnp.maximum, -jnp.inf)
    mprev = mcar[...]
    mm = jnp.maximum(mprev, mx)
    mcol = cum + mm
    mm_last = mm[GROUP - 1:GROUP, :]
    w_k = jnp.exp(gv - mm_last)
    decay = jnp.exp(mprev - mm_last)
    mcar[...] = mcol[GROUP - 1:GROUP, :]
    gv_t = gv.T
    yield _all_bits(gv_t[0:SUBLANES, :])
    for h in range(N_HEADS):
        sl = slice(h * HEAD_DIM, (h + 1) * HEAD_DIM)
        mm_b = jnp.broadcast_to(mm[:, h:h + 1], (GROUP, GROUP))
        mcol_b = jnp.broadcast_to(mcol[:, h:h + 1], (GROUP, GROUP))
        w_intra = jnp.exp(jnp.where(causal, gv_t[h:h + 1, :] - mm_b, -jnp.inf))
        w_inter = jnp.exp(mprev[:, h:h + 1] - mm_b)
        qb = load_z(0, h).astype(BF16)
        kh = load_z(1, h) * (HEAD_DIM ** -0.5)
        v_ext = jnp.concatenate([load_z(2, h).astype(BF16), ones_b], axis=1)
        s = lax.dot_general(qb, kh.astype(BF16), NT_DIMS, preferred_element_type=F32)
        pv = _dot((s * w_intra).astype(BF16), v_ext)
        cprev = cext[h]
        qc = _dot(qb, cprev.astype(BF16))
        num = pv[:, :HEAD_DIM] + w_inter * qc[:, :HEAD_DIM]
        den = pv[:, HEAD_DIM:] + w_inter * qc[:, HEAD_DIM:]
        hh = num / jnp.maximum(jnp.abs(den), jnp.exp(-mcol_b))
        hh = hh * lax.rsqrt(jnp.mean(hh * hh, axis=-1, keepdims=True) + EPS) * mg_ref[:, sl]
        out = jax.nn.sigmoid(load_z(3, h)) * hh
        store_h(h, out)
        kw = (kh * w_k[:, h:h + 1]).astype(BF16)
        dc = lax.dot_general(kw, v_ext, TN_DIMS, preferred_element_type=F32)
        cnew = decay[:, h:h + 1] * cprev + dc
        cext[h] = cnew
        yield _all_bits(out, cnew)


def _piece_feeder(pieces, costs, n_calls):
    n_points = n_calls * TIE_PARTS
    done = [0, 0]

    def one_point():
        done[0] += 1
        target = sum(costs) * done[0] / n_points
        bits = None
        while done[1] < len(costs) and sum(costs[:done[1]]) + costs[done[1]] / 2 <= target:
            piece_bits = next(pieces)
            if piece_bits is not None:
                bits = piece_bits if bits is None else bits | piece_bits
            done[1] += 1
        if bits is None:
            return None
        zero_bits = lax.shift_right_logical(lax.shift_right_logical(bits, jnp.uint32(16)), jnp.uint32(16))
        return lax.bitcast_convert_type(zero_bits, F32)

    return lambda: [one_point() for _ in range(TIE_PARTS)]


def _front_kernel(x_ref, ada_ref, g1_ref, wgu_ref, wd_ref, g2_ref, win_ref, bin_ref, wg_ref, bg_ref, mg_ref,
                  x1_ref, u_ref, h_ref, c_ref, n_ref, m_ref,
                  hid_ref, zbuf, gbuf, cext, mcar, *, tt, nt):
    s = pl.program_id(0)
    tl = lax.rem(jnp.maximum(s - 1, 0), nt)

    @pl.when(s == 0)
    def _():
        zbuf[...] = jnp.zeros(zbuf.shape, F32)
        gbuf[...] = jnp.zeros(gbuf.shape, F32)

    @pl.when(tl == 0)
    def _():
        cext[...] = jnp.zeros(cext.shape, F32)
        mcar[...] = jnp.zeros(mcar.shape, F32)

    def mlstm_pieces(g):
        rs = slice(g * GROUP, (g + 1) * GROUP)

        def load_z(which, h):
            c0 = which * D_MLSTM + h * HEAD_DIM
            return zbuf[rs, c0:c0 + HEAD_DIM]

        def store_h(h, val):
            h_ref[0, rs, h * HEAD_DIM:(h + 1) * HEAD_DIM] = val.astype(BF16)

        return _mlstm_group(gbuf[rs, 0:LANES], gbuf[rs, LANES:D_GATE], load_z, store_h, mg_ref, cext, mcar)

    def mixer_pieces():
        for g in range(tt // GROUP):
            yield from mlstm_pieces(g)

    pieces = mixer_pieces()
    costs = ([10] + [5] * N_HEADS) * (tt // GROUP)
    glu_in = []

    def store_z(c0, zc):
        if c0 < Q_OFF:
            glu_in.append(zc)
            if len(glu_in) == 2:
                u_ref[0] = glu_in[0] * jax.nn.sigmoid(glu_in[1])
        else:
            if c0 == Q_OFF:
                _run(pieces)
            zbuf[:, c0 - Q_OFF:c0 - Q_OFF + D_CONV] = zc

    x1, gates = _ffn1_inproj(x_ref[...], ada_ref[...], g1_ref, wgu_ref, wd_ref, g2_ref,
                             win_ref, bin_ref, wg_ref, bg_ref, hid_ref, store_z,
                             _piece_feeder(pieces, costs, N_FF_CHUNKS))
    x1_ref[...] = x1
    gbuf[...] = gates

    @pl.when(tl == nt - 1)
    def _():
        for h in range(N_HEADS):
            c_ref[0, h] = cext[h, :, :HEAD_DIM]
            n_ref[0, h:h + 1, :] = cext[h, :, HEAD_DIM:].T[0:1, :]
        m_ref[0] = mcar[...]


def _front(x, ada3, p, *, tt):
    bsz, seq, _ = x.shape
    nt = seq // tt
    steps = bsz * nt
    cur = lambda s: jnp.minimum(s, steps - 1)
    lag = lambda s: jnp.maximum(s - 1, 0)
    weights = [p["g1"], p["wgu1"], p["wd1"], p["g2"], p["win"], p["b_main"], p["wg"], p["b_gate"], p["mn_g"]]
    return pl.pallas_call(
        functools.partial(_front_kernel, tt=tt, nt=nt),
        grid=(steps + 1,),
        in_specs=[
            pl.BlockSpec((1, tt, D_MODEL), lambda s: (cur(s) // nt, cur(s) % nt, 0)),
            pl.BlockSpec((1, N_ADA, D_MODEL), lambda s: (cur(s) // nt, 0, 0)),
        ] + [_const_spec(w.shape) for w in weights],
        out_specs=[
            pl.BlockSpec((1, tt, D_MODEL), lambda s: (cur(s) // nt, cur(s) % nt, 0)),
            pl.BlockSpec((1, tt, D_CONV), lambda s: (cur(s) // nt, cur(s) % nt, 0)),
            pl.BlockSpec((1, tt, D_MLSTM), lambda s: (lag(s) // nt, lag(s) % nt, 0)),
            pl.BlockSpec((1, N_HEADS, HEAD_DIM, HEAD_DIM), lambda s: (lag(s) // nt, 0, 0, 0)),
            pl.BlockSpec((1, N_HEADS, HEAD_DIM), lambda s: (lag(s) // nt, 0, 0)),
            pl.BlockSpec((1, 1, LANES), lambda s: (lag(s) // nt, 0, 0)),
        ],
        out_shape=[
            jax.ShapeDtypeStruct(x.shape, F32),
            jax.ShapeDtypeStruct((bsz, seq, D_CONV), F32),
            jax.ShapeDtypeStruct((bsz, seq, D_MLSTM), BF16),
            jax.ShapeDtypeStruct((bsz, N_HEADS, HEAD_DIM, HEAD_DIM), F32),
            jax.ShapeDtypeStruct((bsz, N_HEADS, HEAD_DIM), F32),
            jax.ShapeDtypeStruct((bsz, 1, LANES), F32),
        ],
        scratch_shapes=[
            pltpu.VMEM((tt, D_FF), BF16),
            pltpu.VMEM((tt, 4 * D_MLSTM), F32),
            pltpu.VMEM((tt, D_GATE), F32),
            pltpu.VMEM((N_HEADS, HEAD_DIM, 2 * HEAD_DIM), F32),
            pltpu.VMEM((1, LANES), F32),
        ],
        compiler_params=_params(1),
        name="front",
    )(x, ada3, *weights)


def _back_kernel(x_ref, ada_ref, u_ref, h_ref, wout_ref, g3_ref, wgu_ref, wd_ref, gf_ref,
                 cw_ref, cb_ref, lg_ref, lb_ref,
                 o_ref, cs_ref,
                 hid_ref, ubuf, ush, ycur, ynext, *, tt, nt, rt, steps):
    s = pl.program_id(0)
    tc = lax.rem(jnp.minimum(s, steps - 1), nt)

    @pl.when(s == 0)
    def _():
        ycur[...] = jnp.zeros(ycur.shape, BF16)

    @pl.when(tc == 0)
    def _():
        ubuf[:, 0:TAIL_PAD, :] = jnp.zeros((1, TAIL_PAD, D_CONV), F32)

    def store_y(b0, r0, val):
        ynext[r0:r0 + rt, :] = val[0].astype(BF16)

    pieces = _conv_group(u_ref[...], ubuf, ush, cw_ref, cb_ref, lg_ref, lb_ref, store_y, lambda tail: None,
                         bt=1, rt=rt)
    costs = [3] + [10] * (SUBLANES - 1) + [3 * rt // SUBLANES] * (tt // rt)

    x = x_ref[...]
    ada = ada_ref[...]
    between = _piece_feeder(pieces, costs, 1 + N_FF_CHUNKS)
    mix = _dot(ycur[...], wout_ref[:D_CONV, :]) + _dot(h_ref[0], wout_ref[D_CONV:, :])
    x2 = x + ada[:, 5:6, :] * _tie(mix, between()).reshape(1, tt, D_MODEL)
    xm = _rms(x2, g3_ref[...]) * (1.0 + ada[:, 7:8, :]) + ada[:, 6:7, :]
    ff = _swiglu(xm.reshape(tt, D_MODEL).astype(BF16), wgu_ref, wd_ref, hid_ref, between)
    _run(pieces)
    x3 = x2 + 0.5 * ada[:, 8:9, :] * ff.reshape(1, tt, D_MODEL)
    o_ref[...] = _rms(x3, gf_ref[...])
    ycur[...] = ynext[...]

    @pl.when(jnp.logical_and(tc == nt - 1, s < steps))
    def _():
        cs_ref[...] = ubuf[:, TAIL_LO:TAIL_PAD, :]


def _back(x1, ada3, u, h, p, *, tt, rt):
    bsz, seq, _ = x1.shape
    nt = seq // tt
    steps = bsz * nt
    cur = lambda s: jnp.minimum(s, steps - 1)
    lag = lambda s: jnp.maximum(s - 1, 0)
    weights = [p["wout"], p["g3"], p["wgu2"], p["wd2"], p["gf"], p["conv_w"], p["conv_b"], p["cn_g"], p["cn_b"]]
    return pl.pallas_call(
        functools.partial(_back_kernel, tt=tt, nt=nt, rt=rt, steps=steps),
        grid=(steps + 1,),
        in_specs=[
            pl.BlockSpec((1, tt, D_MODEL), lambda s: (lag(s) // nt, lag(s) % nt, 0)),
            pl.BlockSpec((1, N_ADA, D_MODEL), lambda s: (lag(s) // nt, 0, 0)),
            pl.BlockSpec((1, tt, D_CONV), lambda s: (cur(s) // nt, cur(s) % nt, 0)),
            pl.BlockSpec((1, tt, D_MLSTM), lambda s: (lag(s) // nt, lag(s) % nt, 0)),
        ] + [_const_spec(w.shape) for w in weights],
        out_specs=[
            pl.BlockSpec((1, tt, D_MODEL), lambda s: (lag(s) // nt, lag(s) % nt, 0)),
            pl.BlockSpec((1, CONV_TAIL, D_CONV), lambda s: (cur(s) // nt, 0, 0)),
        ],
        out_shape=[
            jax.ShapeDtypeStruct(x1.shape, F32),
            jax.ShapeDtypeStruct((bsz, CONV_TAIL, D_CONV), F32),
        ],
        scratch_shapes=[
            pltpu.VMEM((tt, D_FF), BF16),
            pltpu.VMEM((1, TAIL_PAD + tt, D_CONV), F32),
            pltpu.VMEM((SUBLANES - 1, 1, TAIL_PAD - SUBLANES + tt, D_CONV), F32),
            pltpu.VMEM((tt, D_CONV), BF16),
            pltpu.VMEM((tt, D_CONV), BF16),
        ],
        compiler_params=_params(1),
        name="back",
    )(x1, ada3, u, h, *weights)


def _ffn_a_kernel(x_ref, ada_ref, g1_ref, wgu_ref, wd_ref, g2_ref, win_ref, bin_ref, wg_ref, bg_ref,
                  x1_ref, z_ref, gate_ref, hid_ref):
    def store_z(c0, zc):
        z_ref[:, c0:c0 + D_CONV] = zc

    x1, gates = _ffn1_inproj(x_ref[...], ada_ref[...], g1_ref, wgu_ref, wd_ref, g2_ref,
                             win_ref, bin_ref, wg_ref, bg_ref, hid_ref, store_z)
    x1_ref[...] = x1
    gate_ref[...] = gates


def _ffn_a(x, ada3, p, *, bb):
    bsz, seq, _ = x.shape
    n = bb * seq
    weights = [p["g1"], p["wgu1"], p["wd1"], p["g2"], p["win"], p["b_main"], p["wg"], p["b_gate"]]
    return pl.pallas_call(
        _ffn_a_kernel,
        grid=(bsz // bb,),
        in_specs=[
            pl.BlockSpec((bb, seq, D_MODEL), lambda b: (b, 0, 0)),
            pl.BlockSpec((bb, N_ADA, D_MODEL), lambda b: (b, 0, 0)),
        ] + [_const_spec(w.shape) for w in weights],
        out_specs=[
            pl.BlockSpec((bb, seq, D_MODEL), lambda b: (b, 0, 0)),
            pl.BlockSpec((n, D_Z), lambda b: (b, 0)),
            pl.BlockSpec((n, D_GATE), lambda b: (b, 0)),
        ],
        out_shape=[
            jax.ShapeDtypeStruct(x.shape, F32),
            jax.ShapeDtypeStruct((bsz * seq, D_Z), F32),
            jax.ShapeDtypeStruct((bsz * seq, D_GATE), F32),
        ],
        scratch_shapes=[pltpu.VMEM((n, D_FF), BF16)],
        compiler_params=_params(1),
        name="ffn_a",
    )(x, ada3, *weights)


def _conv_kernel(a_ref, b_ref, cs_ref, w_ref, cb_ref, lg_ref, lb_ref, y_ref, so_ref, ubuf, ush, *, bt, rt):
    ubuf[:, TAIL_LO:TAIL_PAD, :] = cs_ref[...]

    def store_y(b0, r0, val):
        y_ref[b0:b0 + bt, r0:r0 + rt, :] = val

    def store_tail(tail):
        so_ref[...] = tail

    u = a_ref[...] * jax.nn.sigmoid(b_ref[...])
    _run(_conv_group(u, ubuf, ush, w_ref, cb_ref, lg_ref, lb_ref, store_y, store_tail, bt=bt, rt=rt))


def _conv(z3, state, p, *, bb, bt):
    bsz, seq, _ = z3.shape
    weights = [p["conv_w"], p["conv_b"], p["cn_g"], p["cn_b"]]
    return pl.pallas_call(
        functools.partial(_conv_kernel, bt=bt, rt=seq),
        grid=(bsz // bb,),
        in_specs=[
            pl.BlockSpec((bb, seq, D_CONV), lambda b: (b, 0, 0)),
            pl.BlockSpec((bb, seq, D_CONV), lambda b: (b, 0, 1)),
            pl.BlockSpec((bb, CONV_TAIL, D_CONV), lambda b: (b, 0, 0)),
        ] + [_const_spec(w.shape) for w in weights],
        out_specs=[
            pl.BlockSpec((bb, seq, D_CONV), lambda b: (b, 0, 0)),
            pl.BlockSpec((bb, CONV_TAIL, D_CONV), lambda b: (b, 0, 0)),
        ],
        out_shape=[
            jax.ShapeDtypeStruct((bsz, seq, D_CONV), F32),
            jax.ShapeDtypeStruct((bsz, CONV_TAIL, D_CONV), F32),
        ],
        scratch_shapes=[
            pltpu.VMEM((bb, TAIL_PAD + seq, D_CONV), F32),
            pltpu.VMEM((SUBLANES - 1, bb, TAIL_PAD - SUBLANES + seq, D_CONV), F32),
        ],
        compiler_params=_params(1),
        name="conv",
    )(z3, z3, state, *weights)


def _mlstm_sample_kernel(q_ref, k_ref, v_ref, o_ref, gt_ref, mg_ref, c0_ref, n0_ref, m0_ref,
                         h_ref, c_ref, n_ref, m_ref, *, nseq, ls):
    row = lax.broadcasted_iota(jnp.int32, (GROUP, GROUP), 0)
    col = lax.broadcasted_iota(jnp.int32, (GROUP, GROUP), 1)
    shift = ls.bit_length() - 1
    seg_r = lax.shift_right_logical(row, shift)
    seg_c = lax.shift_right_logical(col, shift)
    causal = jnp.logical_and(col <= row, seg_r == seg_c)
    segtril = causal.astype(F32)
    lastsel = (col == seg_r * ls + (ls - 1)).astype(F32)

    gates = gt_ref[...].reshape(GROUP, D_GATE)
    gi = gates[:, :LANES]
    cum = _dot_exact(segtril, _log_sigmoid(gates[:, LANES:]))
    cumlast = _dot_exact(lastsel, cum)
    cum_t = cum.T
    gi_t = gi.T
    mprev = m0_ref[...].reshape(GROUP, LANES)
    qg = q_ref[...].reshape(GROUP, D_MLSTM)
    kg = k_ref[...].reshape(GROUP, D_MLSTM) * (HEAD_DIM ** -0.5)
    vg = v_ref[...].reshape(GROUP, D_MLSTM)
    og = o_ref[...].reshape(GROUP, D_MLSTM)

    m_all = jnp.zeros((GROUP, LANES), F32)
    lane = lax.broadcasted_iota(jnp.int32, (GROUP, LANES), 1)
    hs = []
    for h in range(N_HEADS):
        sl = slice(h * HEAD_DIM, (h + 1) * HEAD_DIM)
        cum_col = cum[:, h:h + 1]
        dmat = jnp.where(causal, cum_col - cum_t[h:h + 1, :] + gi_t[h:h + 1, :], -jnp.inf)
        inter = cum_col + mprev[:, h:h + 1]
        m_col = jnp.maximum(inter, jnp.max(dmat, axis=-1, keepdims=True))
        w_intra = jnp.exp(dmat - m_col)
        w_inter = jnp.exp(inter - m_col)
        qh = qg[:, sl]
        qb = qh.astype(BF16)
        s = lax.dot_general(qb, kg[:, sl].astype(BF16), NT_DIMS, preferred_element_type=F32)
        p = s * w_intra
        num = _dot(p.astype(BF16), vg[:, sl].astype(BF16))
        den = jnp.sum(p, axis=-1, keepdims=True)
        qc = jnp.zeros((GROUP, HEAD_DIM), F32)
        qn = jnp.zeros((GROUP, 1), F32)
        for sq in range(nseq):
            qc = jnp.where(seg_r == sq, _dot(qb, c0_ref[sq, h].astype(BF16)), qc)
            qn = jnp.where(seg_r[:, 0:1] == sq,
                           jnp.sum(qh * n0_ref[sq, h:h + 1, :], axis=-1, keepdims=True), qn)
        num = num + w_inter * qc
        den = den + w_inter * qn
        hh = num / jnp.maximum(jnp.abs(den), jnp.exp(-m_col))
        hh = hh * lax.rsqrt(jnp.mean(hh * hh, axis=-1, keepdims=True) + EPS) * mg_ref[:, sl]
        hs.append(jax.nn.sigmoid(og[:, sl]) * hh)
        m_all = jnp.where(lane == h, m_col, m_all)

    h_ref[...] = jnp.concatenate(hs, axis=-1).reshape(h_ref.shape)
    m_ref[...] = m_all.reshape(m_ref.shape)

    mnew = _dot_exact(lastsel, m_all)
    for h in range(N_HEADS):
        sl = slice(h * HEAD_DIM, (h + 1) * HEAD_DIM)
        last_col = cumlast[:, h:h + 1]
        mnew_col = mnew[:, h:h + 1]
        w_k = jnp.exp(last_col - cum[:, h:h + 1] + gi[:, h:h + 1] - mnew_col)
        decay = jnp.exp(last_col + mprev[:, h:h + 1] - mnew_col)
        kw = kg[:, sl] * w_k
        kwb = kw.astype(BF16)
        vh = vg[:, sl]
        for sq in range(nseq):
            vs = jnp.where(seg_r == sq, vh, 0.0)
            kws = jnp.where(seg_r == sq, kw, 0.0)
            dc = lax.dot_general(kwb, vs.astype(BF16), TN_DIMS, preferred_element_type=F32)
            dn = jnp.sum(kws, axis=0, keepdims=True)
            dec = decay[sq * ls:sq * ls + 1, :]
            c_ref[sq, h] = dec * c0_ref[sq, h] + dc
            n_ref[sq, h:h + 1, :] = dec * n0_ref[sq, h:h + 1, :] + dn


def _mlstm_sample(z3, gates3, mg, c0, n0, m_tok, *, bb):
    bsz, seq, _ = z3.shape
    assert bb * seq == GROUP
    zspec = lambda k: pl.BlockSpec((bb, seq, D_MLSTM), lambda b: (b, 0, k))
    c_spec = pl.BlockSpec((bb, N_HEADS, HEAD_DIM, HEAD_DIM), lambda b: (b, 0, 0, 0))
    n_spec = pl.BlockSpec((bb, N_HEADS, HEAD_DIM), lambda b: (b, 0, 0))
    m_spec = pl.BlockSpec((bb, seq, LANES), lambda b: (b, 0, 0))
    return pl.pallas_call(
        functools.partial(_mlstm_sample_kernel, nseq=bb, ls=seq),
        grid=(bsz // bb,),
        in_specs=[zspec(2), zspec(3), zspec(4), zspec(5),
                  pl.BlockSpec((bb, seq, D_GATE), lambda b: (b, 0, 0)),
                  _const_spec(mg.shape),
                  c_spec, n_spec, m_spec],
        out_specs=[pl.BlockSpec((bb, seq, D_MLSTM), lambda b: (b, 0, 0)), c_spec, n_spec, m_spec],
        out_shape=[
            jax.ShapeDtypeStruct((bsz, seq, D_MLSTM), F32),
            jax.ShapeDtypeStruct((bsz, N_HEADS, HEAD_DIM, HEAD_DIM), F32),
            jax.ShapeDtypeStruct((bsz, N_HEADS, HEAD_DIM), F32),
            jax.ShapeDtypeStruct((bsz, seq, LANES), F32),
        ],
        compiler_params=_params(1),
        name="mlstm_sample",
    )(z3, z3, z3, z3, gates3, mg, c0, n0, m_tok)


def _ffn_b_kernel(x_ref, ada_ref, y_ref, h_ref, wout_ref, g3_ref, wgu_ref, wd_ref, gf_ref, o_ref, hid_ref):
    x = x_ref[...]
    bb, tt, _ = x.shape
    n = bb * tt
    ada = ada_ref[...]
    mix = (_dot(y_ref[...].reshape(n, D_CONV).astype(BF16), wout_ref[:D_CONV, :])
           + _dot(h_ref[...].reshape(n, D_MLSTM).astype(BF16), wout_ref[D_CONV:, :]))
    x2 = x + ada[:, 5:6, :] * mix.reshape(bb, tt, D_MODEL)
    xm = _rms(x2, g3_ref[...]) * (1.0 + ada[:, 7:8, :]) + ada[:, 6:7, :]
    ff = _swiglu(xm.reshape(n, D_MODEL).astype(BF16), wgu_ref, wd_ref, hid_ref)
    x3 = x2 + 0.5 * ada[:, 8:9, :] * ff.reshape(bb, tt, D_MODEL)
    o_ref[...] = _rms(x3, gf_ref[...])


def _ffn_b(x, ada3, y, h, p, *, bb):
    bsz, seq, _ = x.shape
    tok = lambda w: pl.BlockSpec((bb, seq, w), lambda b: (b, 0, 0))
    weights = [p["wout"], p["g3"], p["wgu2"], p["wd2"], p["gf"]]
    return pl.pallas_call(
        _ffn_b_kernel,
        grid=(bsz // bb,),
        in_specs=[tok(D_MODEL), pl.BlockSpec((bb, N_ADA, D_MODEL), lambda b: (b, 0, 0)), tok(D_CONV), tok(D_MLSTM)]
        + [_const_spec(w.shape) for w in weights],
        out_specs=tok(D_MODEL),
        out_shape=jax.ShapeDtypeStruct(x.shape, F32),
        scratch_shapes=[pltpu.VMEM((bb * seq, D_FF), BF16)],
        compiler_params=_params(1),
        name="ffn_b",
    )(x, ada3, y, h, *weights)


def _prompt_trunk(x, ada3, p):
    x1, u, h, c_new, n_new, m_row = _front(x, ada3, p, tt=512)
    out, conv_new = _back(x1, ada3, u, h, p, tt=512, rt=16)
    return out, conv_new[None], c_new[None], n_new[None], m_row[None, :, 0, :N_HEADS]


def _sample_trunk(x, ada3, states, p):
    bsz, seq, _ = x.shape
    conv_state, c0, n0, m0 = states
    x1, z, gates = _ffn_a(x, ada3, p, bb=32)
    z3 = z.reshape(bsz, seq, D_Z)
    gates3 = gates.reshape(bsz, seq, D_GATE)
    y, conv_new = _conv(z3, conv_state, p, bb=32, bt=8)
    m_tok = jnp.broadcast_to(
        jnp.pad(m0, ((0, 0), (0, LANES - N_HEADS)))[:, None, :], (bsz, seq, LANES))
    h, c_new, n_new, m_tok_new = _mlstm_sample(z3, gates3, p["mn_g"], c0, n0, m_tok, bb=GROUP // seq)
    out = _ffn_b(x1, ada3, y, h, p, bb=64)
    return out, conv_new[None], c_new[None], n_new[None], m_tok_new[None, :, seq - 1, :N_HEADS]


def _gate_cols(a):
    pad = ((0, 0), (0, LANES - N_HEADS))
    return jnp.concatenate([jnp.pad(a[:, :N_HEADS], pad), jnp.pad(a[:, N_HEADS:], pad)], axis=1)


def kernel(x_prompt, x_sample, c_prompt, c_sample, state_conv, state_C, state_n, state_m, w_ada, b_ada, norm_ffn1, ffn1_w_gu, ffn1_w_down, norm_mix, w_in, b_in, conv_w, conv_b, conv_norm_g, conv_norm_b, mlstm_norm_g, w_out, norm_ffn2, ffn2_w_gu, ffn2_w_down, norm_final):
    assert w_ada.shape[0] == 1, "single layer"
    bp = x_prompt.shape[0]
    p = {
        "g1": norm_ffn1, "g2": norm_mix, "g3": norm_ffn2, "gf": norm_final[None],
        "wgu1": ffn1_w_gu[0].astype(BF16), "wd1": ffn1_w_down[0].astype(BF16),
        "wgu2": ffn2_w_gu[0].astype(BF16), "wd2": ffn2_w_down[0].astype(BF16),
        "win": w_in[0][:, :D_Z].astype(BF16),
        "wg": _gate_cols(w_in[0][:, D_Z:]).astype(BF16),
        "b_main": b_in[:, :D_Z],
        "b_gate": _gate_cols(b_in[:, D_Z:]),
        "conv_w": jnp.pad(conv_w[0], ((0, TAIL_PAD - CONV_WIDTH), (0, 0))),
        "conv_b": conv_b, "cn_g": conv_norm_g, "cn_b": conv_norm_b, "mn_g": mlstm_norm_g,
        "wout": w_out[0].astype(BF16),
    }
    c_all = jnp.concatenate([c_prompt, c_sample], axis=0)
    ada = _ada(c_all, w_ada[0], b_ada).reshape(c_all.shape[0], N_ADA, D_MODEL)
    yp, conv_p, c_p, n_p, m_p = _prompt_trunk(x_prompt, ada[:bp], p)
    ys, conv_s, c_s, n_s, m_s = _sample_trunk(
        x_sample, ada[bp:], (state_conv[0], state_C[0], state_n[0], state_m[0]), p)
    return (yp, ys, conv_p, c_p, n_p, m_p, conv_s, c_s, n_s, m_s)
```

```python
import functools

import jax
import jax.numpy as jnp
from jax import lax
from jax.experimental import pallas as pl
from jax.experimental.pallas import tpu as pltpu

F32 = jnp.float32
BF16 = jnp.bfloat16

D_MODEL = 1024
D_CONV = 512
D_MLSTM = 512
N_HEADS = 4
HEAD_DIM = 128
CONV_WIDTH = 31
CONV_TAIL = CONV_WIDTH - 1
D_FF = 2816
N_ADA = 9
EPS = 1e-6
D_Z = 2 * D_CONV + 4 * D_MLSTM
Q_OFF = 2 * D_CONV
LANES = 128
SUBLANES = 8
D_GATE = 2 * LANES
FF_CHUNK = 256
N_FF_CHUNKS = D_FF // FF_CHUNK
GROUP = 128
TAIL_PAD = 32
TAIL_LO = TAIL_PAD - CONV_TAIL
TIE_PARTS = 4
CAST_ROWS = 64
VMEM_LIMIT = 56 * 1024 * 1024
NT_DIMS = (((1,), (1,)), ((), ()))
TN_DIMS = (((0,), (0,)), ((), ()))


def _dot(a, b):
    return jnp.dot(a, b, preferred_element_type=F32)


def _dot_exact(a, b):
    return jnp.dot(a, b, preferred_element_type=F32, precision=lax.Precision.HIGHEST)


def _rms(x, g):
    ms = jnp.mean(x * x, axis=-1, keepdims=True)
    return x * lax.rsqrt(ms + EPS) * g


def _ada_row(ada, k):
    return ada[:, k * D_MODEL:(k + 1) * D_MODEL][:, None, :]


def _log_sigmoid(x):
    return jnp.minimum(x, 0.0) - jnp.log(1.0 + jnp.exp(-jnp.abs(x)))


def _const_spec(shape):
    nd = len(shape)
    return pl.BlockSpec(shape, lambda *_: (0,) * nd, pipeline_mode=pl.Buffered(1))


def _params(n_grid):
    return pltpu.CompilerParams(dimension_semantics=("arbitrary",) * n_grid, vmem_limit_bytes=VMEM_LIMIT)


def _ada_kernel(c_ref, w_ref, b_ref, o_ref):
    c = c_ref[...]
    s = (c * jax.nn.sigmoid(c)).astype(BF16)
    o_ref[...] = _dot(s, w_ref[...].astype(BF16)) + b_ref[...]


def _ada(c_all, w_ada, b_ada):
    n = c_all.shape[0]
    tile = D_MODEL
    return pl.pallas_call(
        _ada_kernel,
        grid=(N_ADA,),
        in_specs=[
            pl.BlockSpec((n, D_MODEL), lambda j: (0, 0)),
            pl.BlockSpec((D_MODEL, tile), lambda j: (0, j)),
            pl.BlockSpec((1, tile), lambda j: (0, j)),
        ],
        out_specs=pl.BlockSpec((n, tile), lambda j: (0, j)),
        out_shape=jax.ShapeDtypeStruct((n, N_ADA * D_MODEL), F32),
        compiler_params=_params(1),
        name="ada",
    )(c_all, w_ada, b_ada)


def _tie(value, zeros):
    rb = value.shape[0] // TIE_PARTS
    packing = 4 // value.dtype.itemsize
    reps = (rb // (SUBLANES * packing), value.shape[1] // LANES)

    def add(block, zero):
        if zero is None:
            return block
        zero = jnp.concatenate([zero] * packing, axis=0).astype(value.dtype)
        return block + jnp.tile(zero, reps)

    return jnp.concatenate([add(value[k * rb:(k + 1) * rb], z) for k, z in enumerate(zeros)], axis=0)


def _swiglu(xm, wgu_ref, wd_ref, hid_ref, between=None):
    for j in range(N_FF_CHUNKS):
        lo = j * FF_CHUNK
        zeros = None if between is None else between()
        gate = _dot(xm, wgu_ref[:, lo:lo + FF_CHUNK])
        up = _dot(xm, wgu_ref[:, D_FF + lo:D_FF + lo + FF_CHUNK])
        hidden = gate * jax.nn.sigmoid(gate) * up
        if zeros is not None:
            hidden = _tie(hidden, zeros)
        hid_ref[:, lo:lo + FF_CHUNK] = hidden.astype(BF16)
    return _dot(hid_ref[...], wd_ref[...])


def _ffn1_inproj(x, ada, g1_ref, wgu_ref, wd_ref, g2_ref, win_ref, bin_ref, wg_ref, bg_ref, hid_ref,
                 store_z, between=None):
    bb, tt, _ = x.shape
    n = bb * tt
    xm = _rms(x, g1_ref[...]) * (1.0 + _ada_row(ada, 1)) + _ada_row(ada, 0)
    ff = _swiglu(xm.reshape(n, D_MODEL).astype(BF16), wgu_ref, wd_ref, hid_ref, between)
    x1 = x + 0.5 * _ada_row(ada, 2) * ff.reshape(bb, tt, D_MODEL)
    hm = _rms(x1, g2_ref[...]) * (1.0 + _ada_row(ada, 4)) + _ada_row(ada, 3)
    hm = hm.reshape(n, D_MODEL).astype(BF16)
    for c0 in range(0, D_Z, D_CONV):
        store_z(c0, _dot(hm, win_ref[:, c0:c0 + D_CONV]) + bin_ref[:, c0:c0 + D_CONV])
    return x1, _dot(hm, wg_ref[...]) + bg_ref[...]


def _run(steps):
    for _ in steps:
        pass


def _all_bits(*arrays):
    acc = None
    for a in arrays:
        bits = lax.bitcast_convert_type(a.reshape(-1, a.shape[-1]), jnp.uint32)
        for r in range(0, bits.shape[0], SUBLANES):
            for c in range(0, bits.shape[1], LANES):
                v = bits[r:r + SUBLANES, c:c + LANES]
                acc = v if acc is None else acc | v
    return acc


def _conv_group(u, ubuf, ush, w_ref, cb_ref, lg_ref, lb_ref, store_y, store_tail, *, bt, rt):
    bb, tt, _ = u.shape
    sh_rows = TAIL_PAD - SUBLANES + tt
    ubuf[:, TAIL_PAD:TAIL_PAD + tt, :] = u
    yield None
    for r in range(1, SUBLANES):
        ush[r - 1] = ubuf[:, r:r + sh_rows, :]
        yield None
    for b0 in range(0, bb, bt):
        for r0 in range(0, tt, rt):
            acc = jnp.broadcast_to(cb_ref[...].reshape(1, 1, D_CONV), (bt, rt, D_CONV))
            for j in range(CONV_WIDTH):
                q, r = divmod(TAIL_LO + j, SUBLANES)
                p0 = r0 + q * SUBLANES
                if r == 0:
                    win = ubuf[b0:b0 + bt, p0:p0 + rt, :]
                else:
                    win = ush[r - 1, b0:b0 + bt, p0:p0 + rt, :]
                acc = acc + w_ref[j:j + 1, :].reshape(1, 1, D_CONV) * win
            mu = jnp.mean(acc, axis=-1, keepdims=True)
            xc = acc - mu
            var = jnp.mean(xc * xc, axis=-1, keepdims=True)
            yn = xc * lax.rsqrt(var + EPS) * lg_ref[...].reshape(1, 1, D_CONV) + lb_ref[...].reshape(1, 1, D_CONV)
            out = yn * jax.nn.sigmoid(yn)
            store_y(b0, r0, out)
            yield _all_bits(out)
    tail = ubuf[:, TAIL_LO + tt:TAIL_PAD + tt, :]
    ubuf[:, TAIL_LO:TAIL_PAD, :] = tail
    store_tail(tail)


def _scan_rows(x, op, ident):
    row = lax.broadcasted_iota(jnp.int32, x.shape, 0)
    s = 1
    while s < x.shape[0]:
        x = op(x, jnp.where(row >= s, pltpu.roll(x, s, axis=0), ident))
        s *= 2
    return x


def _mlstm_group(gi, gf, load_z, store_h, mg_ref, cext, mcar):
    row = lax.broadcasted_iota(jnp.int32, (GROUP, GROUP), 0)
    col = lax.broadcasted_iota(jnp.int32, (GROUP, GROUP), 1)
    causal = col <= row
    ones_b = jnp.ones((GROUP, HEAD_DIM), BF16)
    cum = _scan_rows(_log_sigmoid(gf), jnp.add, 0.0)
    gv = gi - cum
    mx = _scan_rows(gv, jnp.maximum, -jnp.inf)
    mprev = mcar[...]
    mm = jnp.maximum(mprev, mx)
    mcol = cum + mm
    mm_last = mm[GROUP - 1:GROUP, :]
    w_k = jnp.exp(gv - mm_last)
    decay = jnp.exp(mprev - mm_last)
    mcar[...] = mcol[GROUP - 1:GROUP, :]
    gv_t = gv.T
    yield _all_bits(gv_t[0:SUBLANES, :])
    for h in range(N_HEADS):
        sl = slice(h * HEAD_DIM, (h + 1) * HEAD_DIM)
        mm_b = jnp.broadcast_to(mm[:, h:h + 1], (GROUP, GROUP))
        mcol_b = jnp.broadcast_to(mcol[:, h:h + 1], (GROUP, GROUP))
        w_intra = jnp.exp(jnp.where(causal, gv_t[h:h + 1, :] - mm_b, -jnp.inf))
        w_inter = jnp.exp(mprev[:, h:h + 1] - mm_b)
        qb = load_z(0, h).astype(BF16)
        kh = load_z(1, h) * (HEAD_DIM ** -0.5)
        v_ext = jnp.concatenate([load_z(2, h).astype(BF16), ones_b], axis=1)
        s = lax.dot_general(qb, kh.astype(BF16), NT_DIMS, preferred_element_type=F32)
        pv = _dot((s * w_intra).astype(BF16), v_ext)
        cprev = cext[h]
        qc = _dot(qb, cprev.astype(BF16))
        num = pv[:, :HEAD_DIM] + w_inter * qc[:, :HEAD_DIM]
        den = pv[:, HEAD_DIM:] + w_inter * qc[:, HEAD_DIM:]
        hh = num / jnp.maximum(jnp.abs(den), jnp.exp(-mcol_b))
        hh = hh * lax.rsqrt(jnp.mean(hh * hh, axis=-1, keepdims=True) + EPS) * mg_ref[:, sl]
        out = jax.nn.sigmoid(load_z(3, h)) * hh
        store_h(h, out)
        kw = (kh * w_k[:, h:h + 1]).astype(BF16)
        dc = lax.dot_general(kw, v_ext, TN_DIMS, preferred_element_type=F32)
        cnew = decay[:, h:h + 1] * cprev + dc
        cext[h] = cnew
        yield _all_bits(out, cnew)


def _piece_feeder(pieces, costs, call_weights):
    point_weights = [w / TIE_PARTS for w in call_weights for _ in range(TIE_PARTS)]
    done = [0, 0]

    def one_point():
        done[0] += 1
        target = sum(costs) * sum(point_weights[:done[0]]) / sum(point_weights)
        bits = None
        while done[1] < len(costs) and sum(costs[:done[1]]) + costs[done[1]] / 2 <= target:
            piece_bits = next(pieces)
            if piece_bits is not None:
                bits = piece_bits if bits is None else bits | piece_bits
            done[1] += 1
        if bits is None:
            return None
        zero_bits = lax.shift_right_logical(lax.shift_right_logical(bits, jnp.uint32(16)), jnp.uint32(16))
        return lax.bitcast_convert_type(zero_bits, F32)

    return lambda: [one_point() for _ in range(TIE_PARTS)]


def _weight_export(k, vmem_refs, out_refs, sems):
    return pltpu.make_async_copy(vmem_refs[k], out_refs[k], sems.at[k])


def _weights_to_bf16(hbm_refs, vmem_refs, out_refs, stage, load_sems, out_sems):
    for src, dst in zip(hbm_refs, vmem_refs):
        rows, cols = dst.shape

        def load(i, src=src, cols=cols):
            return pltpu.make_async_copy(src.at[pl.ds(i * CAST_ROWS, CAST_ROWS), pl.ds(0, cols)],
                                         stage.at[i % 2, :, pl.ds(0, cols)], load_sems.at[i % 2])

        load(0).start()
        for i in range(rows // CAST_ROWS):
            if (i + 1) * CAST_ROWS < rows:
                load(i + 1).start()
            load(i).wait()
            dst[i * CAST_ROWS:(i + 1) * CAST_ROWS, :] = stage[i % 2, :, 0:cols].astype(BF16)
    for k in range(len(vmem_refs)):
        _weight_export(k, vmem_refs, out_refs, out_sems).start()


def _front_kernel(x_ref, ada_ref, wgu_hbm, wd_hbm, win_hbm, g1_ref, g2_ref, bin_ref, wg_ref, bg_ref, mg_ref,
                  x1_ref, u_ref, h_ref, c_ref, n_ref, m_ref, wgu_out, wd_out, win_out,
                  hid_ref, zbuf, gbuf, cext, mcar, wgu_ref, wd_ref, win_ref, stage, load_sems, out_sems,
                  *, tt, nt, steps):
    s = pl.program_id(0)
    tl = lax.rem(jnp.maximum(s - 1, 0), nt)
    resident = (wgu_ref, wd_ref, win_ref)
    exported = (wgu_out, wd_out, win_out)

    @pl.when(s == 0)
    def _():
        _weights_to_bf16((wgu_hbm, wd_hbm, win_hbm), resident, exported, stage, load_sems, out_sems)
        zbuf[...] = jnp.zeros(zbuf.shape, F32)
        gbuf[...] = jnp.zeros(gbuf.shape, F32)

    @pl.when(s == steps)
    def _():
        for k in range(len(resident)):
            _weight_export(k, resident, exported, out_sems).wait()

    @pl.when(tl == 0)
    def _():
        cext[...] = jnp.zeros(cext.shape, F32)
        mcar[...] = jnp.zeros(mcar.shape, F32)

    def mlstm_pieces(g):
        rs = slice(g * GROUP, (g + 1) * GROUP)

        def load_z(which, h):
            c0 = which * D_MLSTM + h * HEAD_DIM
            return zbuf[rs, c0:c0 + HEAD_DIM]

        def store_h(h, val):
            h_ref[0, rs, h * HEAD_DIM:(h + 1) * HEAD_DIM] = val.astype(BF16)

        return _mlstm_group(gbuf[rs, 0:LANES], gbuf[rs, LANES:D_GATE], load_z, store_h, mg_ref, cext, mcar)

    def mixer_pieces():
        for g in range(tt // GROUP):
            yield from mlstm_pieces(g)

    pieces = mixer_pieces()
    costs = ([10] + [5] * N_HEADS) * (tt // GROUP)
    glu_in = []

    def store_z(c0, zc):
        if c0 < Q_OFF:
            glu_in.append(zc)
            if len(glu_in) == 2:
                u_ref[0] = glu_in[0] * jax.nn.sigmoid(glu_in[1])
        else:
            if c0 == Q_OFF:
                _run(pieces)
            zbuf[:, c0 - Q_OFF:c0 - Q_OFF + D_CONV] = zc

    seq_id = jnp.minimum(s, steps - 1) // nt
    x1, gates = _ffn1_inproj(x_ref[...], ada_ref[pl.ds(seq_id, 1), :], g1_ref, wgu_ref, wd_ref, g2_ref,
                             win_ref, bin_ref, wg_ref, bg_ref, hid_ref, store_z,
                             _piece_feeder(pieces, costs, [1.0] * N_FF_CHUNKS))
    x1_ref[...] = x1
    gbuf[...] = gates

    @pl.when(tl == nt - 1)
    def _():
        for h in range(N_HEADS):
            c_ref[0, h] = cext[h, :, :HEAD_DIM]
            n_ref[0, h:h + 1, :] = cext[h, :, HEAD_DIM:].T[0:1, :]
        m_ref[0] = mcar[...]


def _front(x, ada, p, *, tt):
    bsz, seq, _ = x.shape
    nt = seq // tt
    steps = bsz * nt
    cur = lambda s: jnp.minimum(s, steps - 1)
    lag = lambda s: jnp.maximum(s - 1, 0)
    big = [p["wgu1"], p["wd1"], p["win"]]
    big_shapes = [(D_MODEL, 2 * D_FF), (D_FF, D_MODEL), (D_MODEL, D_Z)]
    small = [p["g1"], p["g2"], p["b_main"], p["wg"], p["b_gate"], p["mn_g"]]
    any_spec = pl.BlockSpec(memory_space=pl.ANY)
    return pl.pallas_call(
        functools.partial(_front_kernel, tt=tt, nt=nt, steps=steps),
        grid=(steps + 1,),
        in_specs=[
            pl.BlockSpec((1, tt, D_MODEL), lambda s: (cur(s) // nt, cur(s) % nt, 0)),
            _const_spec(ada.shape),
        ] + [any_spec] * len(big) + [_const_spec(w.shape) for w in small],
        out_specs=[
            pl.BlockSpec((1, tt, D_MODEL), lambda s: (cur(s) // nt, cur(s) % nt, 0)),
            pl.BlockSpec((1, tt, D_CONV), lambda s: (cur(s) // nt, cur(s) % nt, 0)),
            pl.BlockSpec((1, tt, D_MLSTM), lambda s: (lag(s) // nt, lag(s) % nt, 0)),
            pl.BlockSpec((1, N_HEADS, HEAD_DIM, HEAD_DIM), lambda s: (lag(s) // nt, 0, 0, 0)),
            pl.BlockSpec((1, N_HEADS, HEAD_DIM), lambda s: (lag(s) // nt, 0, 0)),
            pl.BlockSpec((1, 1, LANES), lambda s: (lag(s) // nt, 0, 0)),
        ] + [any_spec] * len(big),
        out_shape=[
            jax.ShapeDtypeStruct(x.shape, F32),
            jax.ShapeDtypeStruct((bsz, seq, D_CONV), F32),
            jax.ShapeDtypeStruct((bsz, seq, D_MLSTM), BF16),
            jax.ShapeDtypeStruct((bsz, N_HEADS, HEAD_DIM, HEAD_DIM), F32),
            jax.ShapeDtypeStruct((bsz, N_HEADS, HEAD_DIM), F32),
            jax.ShapeDtypeStruct((bsz, 1, LANES), F32),
        ] + [jax.ShapeDtypeStruct(shape, BF16) for shape in big_shapes],
        scratch_shapes=[
            pltpu.VMEM((tt, D_FF), BF16),
            pltpu.VMEM((tt, 4 * D_MLSTM), F32),
            pltpu.VMEM((tt, D_GATE), F32),
            pltpu.VMEM((N_HEADS, HEAD_DIM, 2 * HEAD_DIM), F32),
            pltpu.VMEM((1, LANES), F32),
        ] + [pltpu.VMEM(shape, BF16) for shape in big_shapes] + [
            pltpu.VMEM((2, CAST_ROWS, 2 * D_FF), F32),
            pltpu.SemaphoreType.DMA((2,)),
            pltpu.SemaphoreType.DMA((len(big),)),
        ],
        compiler_params=_params(1),
        name="front",
    )(x, ada, *big, *small)


def _back_kernel(x_ref, ada_ref, u_ref, h_ref, wgu_hbm, wd_hbm, wout_hbm, g3_ref, gf_ref,
                 cw_ref, cb_ref, lg_ref, lb_ref,
                 o_ref, cs_ref, wgu_out, wd_out, wout_out,
                 hid_ref, ubuf, ush, ycur, ynext, wgu_ref, wd_ref, wout_ref, stage, load_sems, out_sems,
                 *, tt, nt, rt, steps):
    s = pl.program_id(0)
    tc = lax.rem(jnp.minimum(s, steps - 1), nt)
    resident = (wgu_ref, wd_ref, wout_ref)
    exported = (wgu_out, wd_out, wout_out)

    @pl.when(s == 0)
    def _():
        _weights_to_bf16((wgu_hbm, wd_hbm, wout_hbm), resident, exported, stage, load_sems, out_sems)
        ycur[...] = jnp.zeros(ycur.shape, BF16)

    @pl.when(s == steps)
    def _():
        for k in range(len(resident)):
            _weight_export(k, resident, exported, out_sems).wait()

    @pl.when(tc == 0)
    def _():
        ubuf[:, 0:TAIL_PAD, :] = jnp.zeros((1, TAIL_PAD, D_CONV), F32)

    def store_y(b0, r0, val):
        ynext[r0:r0 + rt, :] = val[0].astype(BF16)

    pieces = _conv_group(u_ref[...], ubuf, ush, cw_ref, cb_ref, lg_ref, lb_ref, store_y, lambda tail: None,
                         bt=1, rt=rt)
    costs = [3] + [10] * (SUBLANES - 1) + [3 * rt // SUBLANES] * (tt // rt)

    x = x_ref[...]
    ada = ada_ref[pl.ds(jnp.maximum(s - 1, 0) // nt, 1), :]
    between = _piece_feeder(pieces, costs, [1.0] * (1 + N_FF_CHUNKS))
    mix = _dot(ycur[...], wout_ref[:D_CONV, :]) + _dot(h_ref[0], wout_ref[D_CONV:, :])
    x2 = x + _ada_row(ada, 5) * _tie(mix, between()).reshape(1, tt, D_MODEL)
    xm = _rms(x2, g3_ref[...]) * (1.0 + _ada_row(ada, 7)) + _ada_row(ada, 6)
    ff = _swiglu(xm.reshape(tt, D_MODEL).astype(BF16), wgu_ref, wd_ref, hid_ref, between)
    _run(pieces)
    x3 = x2 + 0.5 * _ada_row(ada, 8) * ff.reshape(1, tt, D_MODEL)
    o_ref[...] = _rms(x3, gf_ref[...])
    ycur[...] = ynext[...]

    @pl.when(jnp.logical_and(tc == nt - 1, s < steps))
    def _():
        cs_ref[...] = ubuf[:, TAIL_LO:TAIL_PAD, :]


def _back(x1, ada, u, h, p, *, tt, rt):
    bsz, seq, _ = x1.shape
    nt = seq // tt
    steps = bsz * nt
    cur = lambda s: jnp.minimum(s, steps - 1)
    lag = lambda s: jnp.maximum(s - 1, 0)
    big = [p["wgu2"], p["wd2"], p["wout"]]
    big_shapes = [(D_MODEL, 2 * D_FF), (D_FF, D_MODEL), (D_MODEL, D_MODEL)]
    small = [p["g3"], p["gf"], p["conv_w"], p["conv_b"], p["cn_g"], p["cn_b"]]
    any_spec = pl.BlockSpec(memory_space=pl.ANY)
    return pl.pallas_call(
        functools.partial(_back_kernel, tt=tt, nt=nt, rt=rt, steps=steps),
        grid=(steps + 1,),
        in_specs=[
            pl.BlockSpec((1, tt, D_MODEL), lambda s: (lag(s) // nt, lag(s) % nt, 0)),
            _const_spec(ada.shape),
            pl.BlockSpec((1, tt, D_CONV), lambda s: (cur(s) // nt, cur(s) % nt, 0)),
            pl.BlockSpec((1, tt, D_MLSTM), lambda s: (lag(s) // nt, lag(s) % nt, 0)),
        ] + [any_spec] * len(big) + [_const_spec(w.shape) for w in small],
        out_specs=[
            pl.BlockSpec((1, tt, D_MODEL), lambda s: (lag(s) // nt, lag(s) % nt, 0)),
            pl.BlockSpec((1, CONV_TAIL, D_CONV), lambda s: (cur(s) // nt, 0, 0)),
        ] + [any_spec] * len(big),
        out_shape=[
            jax.ShapeDtypeStruct(x1.shape, F32),
            jax.ShapeDtypeStruct((bsz, CONV_TAIL, D_CONV), F32),
        ] + [jax.ShapeDtypeStruct(shape, BF16) for shape in big_shapes],
        scratch_shapes=[
            pltpu.VMEM((tt, D_FF), BF16),
            pltpu.VMEM((1, TAIL_PAD + tt, D_CONV), F32),
            pltpu.VMEM((SUBLANES - 1, 1, TAIL_PAD - SUBLANES + tt, D_CONV), F32),
            pltpu.VMEM((tt, D_CONV), BF16),
            pltpu.VMEM((tt, D_CONV), BF16),
        ] + [pltpu.VMEM(shape, BF16) for shape in big_shapes] + [
            pltpu.VMEM((2, CAST_ROWS, 2 * D_FF), F32),
            pltpu.SemaphoreType.DMA((2,)),
            pltpu.SemaphoreType.DMA((len(big),)),
        ],
        compiler_params=_params(1),
        name="back",
    )(x1, ada, u, h, *big, *small)


def _ffn_a_kernel(x_ref, ada_ref, g1_ref, wgu_ref, wd_ref, g2_ref, win_ref, bin_ref, wg_ref, bg_ref,
                  x1_ref, z_ref, gate_ref, hid_ref):
    def store_z(c0, zc):
        z_ref[:, c0:c0 + D_CONV] = zc

    x1, gates = _ffn1_inproj(x_ref[...], ada_ref[...], g1_ref, wgu_ref, wd_ref, g2_ref,
                             win_ref, bin_ref, wg_ref, bg_ref, hid_ref, store_z)
    x1_ref[...] = x1
    gate_ref[...] = gates


def _ffn_a(x, ada, p, *, bb):
    bsz, seq, _ = x.shape
    n = bb * seq
    weights = [p["g1"], p["wgu1"], p["wd1"], p["g2"], p["win"], p["b_main"], p["wg"], p["b_gate"]]
    return pl.pallas_call(
        _ffn_a_kernel,
        grid=(bsz // bb,),
        in_specs=[
            pl.BlockSpec((bb, seq, D_MODEL), lambda b: (b, 0, 0)),
            pl.BlockSpec((bb, N_ADA * D_MODEL), lambda b: (b, 0)),
        ] + [_const_spec(w.shape) for w in weights],
        out_specs=[
            pl.BlockSpec((bb, seq, D_MODEL), lambda b: (b, 0, 0)),
            pl.BlockSpec((n, D_Z), lambda b: (b, 0)),
            pl.BlockSpec((n, D_GATE), lambda b: (b, 0)),
        ],
        out_shape=[
            jax.ShapeDtypeStruct(x.shape, F32),
            jax.ShapeDtypeStruct((bsz * seq, D_Z), F32),
            jax.ShapeDtypeStruct((bsz * seq, D_GATE), F32),
        ],
        scratch_shapes=[pltpu.VMEM((n, D_FF), BF16)],
        compiler_params=_params(1),
        name="ffn_a",
    )(x, ada, *weights)


def _conv_kernel(a_ref, b_ref, cs_ref, w_ref, cb_ref, lg_ref, lb_ref, y_ref, so_ref, ubuf, ush, *, bt, rt):
    ubuf[:, TAIL_LO:TAIL_PAD, :] = cs_ref[...]

    def store_y(b0, r0, val):
        y_ref[b0:b0 + bt, r0:r0 + rt, :] = val

    def store_tail(tail):
        so_ref[...] = tail

    u = a_ref[...] * jax.nn.sigmoid(b_ref[...])
    _run(_conv_group(u, ubuf, ush, w_ref, cb_ref, lg_ref, lb_ref, store_y, store_tail, bt=bt, rt=rt))


def _conv(z3, state, p, *, bb, bt):
    bsz, seq, _ = z3.shape
    weights = [p["conv_w"], p["conv_b"], p["cn_g"], p["cn_b"]]
    return pl.pallas_call(
        functools.partial(_conv_kernel, bt=bt, rt=seq),
        grid=(bsz // bb,),
        in_specs=[
            pl.BlockSpec((bb, seq, D_CONV), lambda b: (b, 0, 0)),
            pl.BlockSpec((bb, seq, D_CONV), lambda b: (b, 0, 1)),
            pl.BlockSpec((bb, CONV_TAIL, D_CONV), lambda b: (b, 0, 0)),
        ] + [_const_spec(w.shape) for w in weights],
        out_specs=[
            pl.BlockSpec((bb, seq, D_CONV), lambda b: (b, 0, 0)),
            pl.BlockSpec((bb, CONV_TAIL, D_CONV), lambda b: (b, 0, 0)),
        ],
        out_shape=[
            jax.ShapeDtypeStruct((bsz, seq, D_CONV), F32),
            jax.ShapeDtypeStruct((bsz, CONV_TAIL, D_CONV), F32),
        ],
        scratch_shapes=[
            pltpu.VMEM((bb, TAIL_PAD + seq, D_CONV), F32),
            pltpu.VMEM((SUBLANES - 1, bb, TAIL_PAD - SUBLANES + seq, D_CONV), F32),
        ],
        compiler_params=_params(1),
        name="conv",
    )(z3, z3, state, *weights)


def _mlstm_sample_kernel(q_ref, k_ref, v_ref, o_ref, gt_ref, mg_ref, c0_ref, n0_ref, m0_ref,
                         h_ref, c_ref, n_ref, m_ref, *, nseq, ls):
    row = lax.broadcasted_iota(jnp.int32, (GROUP, GROUP), 0)
    col = lax.broadcasted_iota(jnp.int32, (GROUP, GROUP), 1)
    shift = ls.bit_length() - 1
    seg_r = lax.shift_right_logical(row, shift)
    seg_c = lax.shift_right_logical(col, shift)
    causal = jnp.logical_and(col <= row, seg_r == seg_c)
    segtril = causal.astype(F32)
    lastsel = (col == seg_r * ls + (ls - 1)).astype(F32)

    gates = gt_ref[...].reshape(GROUP, D_GATE)
    gi = gates[:, :LANES]
    cum = _dot_exact(segtril, _log_sigmoid(gates[:, LANES:]))
    cumlast = _dot_exact(lastsel, cum)
    cum_t = cum.T
    gi_t = gi.T
    mprev = m0_ref[...].reshape(GROUP, LANES)
    qg = q_ref[...].reshape(GROUP, D_MLSTM)
    kg = k_ref[...].reshape(GROUP, D_MLSTM) * (HEAD_DIM ** -0.5)
    vg = v_ref[...].reshape(GROUP, D_MLSTM)
    og = o_ref[...].reshape(GROUP, D_MLSTM)

    m_all = jnp.zeros((GROUP, LANES), F32)
    lane = lax.broadcasted_iota(jnp.int32, (GROUP, LANES), 1)
    hs = []
    for h in range(N_HEADS):
        sl = slice(h * HEAD_DIM, (h + 1) * HEAD_DIM)
        cum_col = cum[:, h:h + 1]
        dmat = jnp.where(causal, cum_col - cum_t[h:h + 1, :] + gi_t[h:h + 1, :], -jnp.inf)
        inter = cum_col + mprev[:, h:h + 1]
        m_col = jnp.maximum(inter, jnp.max(dmat, axis=-1, keepdims=True))
        w_intra = jnp.exp(dmat - m_col)
        w_inter = jnp.exp(inter - m_col)
        qh = qg[:, sl]
        qb = qh.astype(BF16)
        s = lax.dot_general(qb, kg[:, sl].astype(BF16), NT_DIMS, preferred_element_type=F32)
        p = s * w_intra
        num = _dot(p.astype(BF16), vg[:, sl].astype(BF16))
        den = jnp.sum(p, axis=-1, keepdims=True)
        qc = jnp.zeros((GROUP, HEAD_DIM), F32)
        qn = jnp.zeros((GROUP, 1), F32)
        for sq in range(nseq):
            qc = jnp.where(seg_r == sq, _dot(qb, c0_ref[sq, h].astype(BF16)), qc)
            qn = jnp.where(seg_r[:, 0:1] == sq,
                           jnp.sum(qh * n0_ref[sq, h:h + 1, :], axis=-1, keepdims=True), qn)
        num = num + w_inter * qc
        den = den + w_inter * qn
        hh = num / jnp.maximum(jnp.abs(den), jnp.exp(-m_col))
        hh = hh * lax.rsqrt(jnp.mean(hh * hh, axis=-1, keepdims=True) + EPS) * mg_ref[:, sl]
        hs.append(jax.nn.sigmoid(og[:, sl]) * hh)
        m_all = jnp.where(lane == h, m_col, m_all)

    h_ref[...] = jnp.concatenate(hs, axis=-1).reshape(h_ref.shape)
    m_ref[...] = m_all.reshape(m_ref.shape)

    mnew = _dot_exact(lastsel, m_all)
    for h in range(N_HEADS):
        sl = slice(h * HEAD_DIM, (h + 1) * HEAD_DIM)
        last_col = cumlast[:, h:h + 1]
        mnew_col = mnew[:, h:h + 1]
        w_k = jnp.exp(last_col - cum[:, h:h + 1] + gi[:, h:h + 1] - mnew_col)
        decay = jnp.exp(last_col + mprev[:, h:h + 1] - mnew_col)
        kw = kg[:, sl] * w_k
        kwb = kw.astype(BF16)
        vh = vg[:, sl]
        for sq in range(nseq):
            vs = jnp.where(seg_r == sq, vh, 0.0)
            kws = jnp.where(seg_r == sq, kw, 0.0)
            dc = lax.dot_general(kwb, vs.astype(BF16), TN_DIMS, preferred_element_type=F32)
            dn = jnp.sum(kws, axis=0, keepdims=True)
            dec = decay[sq * ls:sq * ls + 1, :]
            c_ref[sq, h] = dec * c0_ref[sq, h] + dc
            n_ref[sq, h:h + 1, :] = dec * n0_ref[sq, h:h + 1, :] + dn


def _mlstm_sample(z3, gates3, mg, c0, n0, m_tok, *, bb):
    bsz, seq, _ = z3.shape
    assert bb * seq == GROUP
    zspec = lambda k: pl.BlockSpec((bb, seq, D_MLSTM), lambda b: (b, 0, k))
    c_spec = pl.BlockSpec((bb, N_HEADS, HEAD_DIM, HEAD_DIM), lambda b: (b, 0, 0, 0))
    n_spec = pl.BlockSpec((bb, N_HEADS, HEAD_DIM), lambda b: (b, 0, 0))
    m_spec = pl.BlockSpec((bb, seq, LANES), lambda b: (b, 0, 0))
    return pl.pallas_call(
        functools.partial(_mlstm_sample_kernel, nseq=bb, ls=seq),
        grid=(bsz // bb,),
        in_specs=[zspec(2), zspec(3), zspec(4), zspec(5),
                  pl.BlockSpec((bb, seq, D_GATE), lambda b: (b, 0, 0)),
                  _const_spec(mg.shape),
                  c_spec, n_spec, m_spec],
        out_specs=[pl.BlockSpec((bb, seq, D_MLSTM), lambda b: (b, 0, 0)), c_spec, n_spec, m_spec],
        out_shape=[
            jax.ShapeDtypeStruct((bsz, seq, D_MLSTM), F32),
            jax.ShapeDtypeStruct((bsz, N_HEADS, HEAD_DIM, HEAD_DIM), F32),
            jax.ShapeDtypeStruct((bsz, N_HEADS, HEAD_DIM), F32),
            jax.ShapeDtypeStruct((bsz, seq, LANES), F32),
        ],
        compiler_params=_params(1),
        name="mlstm_sample",
    )(z3, z3, z3, z3, gates3, mg, c0, n0, m_tok)


def _ffn_b_kernel(x_ref, ada_ref, y_ref, h_ref, wout_ref, g3_ref, wgu_ref, wd_ref, gf_ref, o_ref, hid_ref):
    x = x_ref[...]
    bb, tt, _ = x.shape
    n = bb * tt
    ada = ada_ref[...]
    mix = (_dot(y_ref[...].reshape(n, D_CONV).astype(BF16), wout_ref[:D_CONV, :])
           + _dot(h_ref[...].reshape(n, D_MLSTM).astype(BF16), wout_ref[D_CONV:, :]))
    x2 = x + _ada_row(ada, 5) * mix.reshape(bb, tt, D_MODEL)
    xm = _rms(x2, g3_ref[...]) * (1.0 + _ada_row(ada, 7)) + _ada_row(ada, 6)
    ff = _swiglu(xm.reshape(n, D_MODEL).astype(BF16), wgu_ref, wd_ref, hid_ref)
    x3 = x2 + 0.5 * _ada_row(ada, 8) * ff.reshape(bb, tt, D_MODEL)
    o_ref[...] = _rms(x3, gf_ref[...])


def _ffn_b(x, ada, y, h, p, *, bb):
    bsz, seq, _ = x.shape
    tok = lambda w: pl.BlockSpec((bb, seq, w), lambda b: (b, 0, 0))
    weights = [p["wout"], p["g3"], p["wgu2"], p["wd2"], p["gf"]]
    return pl.pallas_call(
        _ffn_b_kernel,
        grid=(bsz // bb,),
        in_specs=[tok(D_MODEL), pl.BlockSpec((bb, N_ADA * D_MODEL), lambda b: (b, 0)), tok(D_CONV), tok(D_MLSTM)]
        + [_const_spec(w.shape) for w in weights],
        out_specs=tok(D_MODEL),
        out_shape=jax.ShapeDtypeStruct(x.shape, F32),
        scratch_shapes=[pltpu.VMEM((bb * seq, D_FF), BF16)],
        compiler_params=_params(1),
        name="ffn_b",
    )(x, ada, y, h, *weights)


def _prompt_trunk(x, ada, p):
    x1, u, h, c_new, n_new, m_row, wgu1, wd1, win = _front(x, ada, p, tt=512)
    out, conv_new, wgu2, wd2, wout = _back(x1, ada, u, h, p, tt=512, rt=16)
    bf16_weights = dict(wgu1=wgu1, wd1=wd1, win=win, wgu2=wgu2, wd2=wd2, wout=wout)
    return (out, conv_new[None], c_new[None], n_new[None], m_row[None, :, 0, :N_HEADS]), bf16_weights


def _sample_trunk(x, ada, states, p):
    bsz, seq, _ = x.shape
    conv_state, c0, n0, m0 = states
    x1, z, gates = _ffn_a(x, ada, p, bb=32)
    z3 = z.reshape(bsz, seq, D_Z)
    gates3 = gates.reshape(bsz, seq, D_GATE)
    y, conv_new = _conv(z3, conv_state, p, bb=32, bt=8)
    m_tok = jnp.broadcast_to(
        jnp.pad(m0, ((0, 0), (0, LANES - N_HEADS)))[:, None, :], (bsz, seq, LANES))
    h, c_new, n_new, m_tok_new = _mlstm_sample(z3, gates3, p["mn_g"], c0, n0, m_tok, bb=GROUP // seq)
    out = _ffn_b(x1, ada, y, h, p, bb=64)
    return out, conv_new[None], c_new[None], n_new[None], m_tok_new[None, :, seq - 1, :N_HEADS]


def _gate_cols(a):
    pad = ((0, 0), (0, LANES - N_HEADS))
    return jnp.concatenate([jnp.pad(a[:, :N_HEADS], pad), jnp.pad(a[:, N_HEADS:], pad)], axis=1)


def kernel(x_prompt, x_sample, c_prompt, c_sample, state_conv, state_C, state_n, state_m, w_ada, b_ada, norm_ffn1, ffn1_w_gu, ffn1_w_down, norm_mix, w_in, b_in, conv_w, conv_b, conv_norm_g, conv_norm_b, mlstm_norm_g, w_out, norm_ffn2, ffn2_w_gu, ffn2_w_down, norm_final):
    assert w_ada.shape[0] == 1, "single layer"
    p = {
        "g1": norm_ffn1, "g2": norm_mix, "g3": norm_ffn2, "gf": norm_final[None],
        "wgu1": ffn1_w_gu[0], "wd1": ffn1_w_down[0], "wgu2": ffn2_w_gu[0], "wd2": ffn2_w_down[0],
        "win": w_in[0], "wout": w_out[0],
        "wg": _gate_cols(w_in[0][:, D_Z:]).astype(BF16),
        "b_main": b_in,
        "b_gate": _gate_cols(b_in[:, D_Z:]),
        "conv_w": jnp.pad(conv_w[0], ((0, TAIL_PAD - CONV_WIDTH), (0, 0))),
        "conv_b": conv_b, "cn_g": conv_norm_g, "cn_b": conv_norm_b, "mn_g": mlstm_norm_g,
    }
    bs = x_sample.shape[0]
    ada = _ada(jnp.concatenate([c_sample, c_prompt], axis=0), w_ada[0], b_ada)
    (yp, conv_p, c_p, n_p, m_p), bf16_weights = _prompt_trunk(x_prompt, ada[bs:], p)
    ys, conv_s, c_s, n_s, m_s = _sample_trunk(
        x_sample, ada, (state_conv[0], state_C[0], state_n[0], state_m[0]), {**p, **bf16_weights})
    return (yp, ys, conv_p, c_p, n_p, m_p, conv_s, c_s, n_s, m_s)
```

```python
import functools

import jax
import jax.numpy as jnp
from jax import lax
from jax.experimental import pallas as pl
from jax.experimental.pallas import tpu as pltpu

F32 = jnp.float32
BF16 = jnp.bfloat16

D_MODEL = 1024
D_CONV = 512
D_MLSTM = 512
N_HEADS = 4
HEAD_DIM = 128
CONV_WIDTH = 31
CONV_TAIL = CONV_WIDTH - 1
D_FF = 2816
N_ADA = 9
EPS = 1e-6
D_Z = 2 * D_CONV + 4 * D_MLSTM
Q_OFF = 2 * D_CONV
LANES = 128
SUBLANES = 8
D_GATE = 2 * LANES
FF_CHUNK = 256
N_FF_CHUNKS = D_FF // FF_CHUNK
GROUP = 128
TAIL_PAD = 32
TAIL_LO = TAIL_PAD - CONV_TAIL
TIE_PARTS = 4
CAST_SLOTS = 3
CAST_ROWS_WIDE = 64
CAST_ROWS_NARROW = 128
VMEM_LIMIT = 56 * 1024 * 1024
NT_DIMS = (((1,), (1,)), ((), ()))
TN_DIMS = (((0,), (0,)), ((), ()))


def _dot(a, b):
    return jnp.dot(a, b, preferred_element_type=F32)


def _dot_exact(a, b):
    return jnp.dot(a, b, preferred_element_type=F32, precision=lax.Precision.HIGHEST)


def _rms(x, g):
    ms = jnp.mean(x * x, axis=-1, keepdims=True)
    return x * lax.rsqrt(ms + EPS) * g


def _ada_row(ada, k):
    return ada[:, k * D_MODEL:(k + 1) * D_MODEL][:, None, :]


def _log_sigmoid(x):
    return jnp.minimum(x, 0.0) - jnp.log(1.0 + jnp.exp(-jnp.abs(x)))


def _const_spec(shape):
    nd = len(shape)
    return pl.BlockSpec(shape, lambda *_: (0,) * nd, pipeline_mode=pl.Buffered(1))


def _params(n_grid):
    return pltpu.CompilerParams(dimension_semantics=("arbitrary",) * n_grid, vmem_limit_bytes=VMEM_LIMIT)


def _ada_kernel(c_ref, w_ref, b_ref, o_ref):
    c = c_ref[...]
    s = (c * jax.nn.sigmoid(c)).astype(BF16)
    o_ref[...] = _dot(s, w_ref[...].astype(BF16)) + b_ref[...]


def _ada(c_all, w_ada, b_ada):
    n = c_all.shape[0]
    tile = D_MODEL
    return pl.pallas_call(
        _ada_kernel,
        grid=(N_ADA,),
        in_specs=[
            pl.BlockSpec((n, D_MODEL), lambda j: (0, 0)),
            pl.BlockSpec((D_MODEL, tile), lambda j: (0, j)),
            pl.BlockSpec((1, tile), lambda j: (0, j)),
        ],
        out_specs=pl.BlockSpec((n, tile), lambda j: (0, j)),
        out_shape=jax.ShapeDtypeStruct((n, N_ADA * D_MODEL), F32),
        compiler_params=_params(1),
        name="ada",
    )(c_all, w_ada, b_ada)


def _tie(value, zeros):
    rb = value.shape[0] // TIE_PARTS
    packing = 4 // value.dtype.itemsize
    reps = (rb // (SUBLANES * packing), value.shape[1] // LANES)

    def add(block, zero):
        if zero is None:
            return block
        zero = jnp.concatenate([zero] * packing, axis=0).astype(value.dtype)
        return block + jnp.tile(zero, reps)

    return jnp.concatenate([add(value[k * rb:(k + 1) * rb], z) for k, z in enumerate(zeros)], axis=0)


def _swiglu(xm, wgu_ref, wd_ref, hid_ref, between=None):
    for j in range(N_FF_CHUNKS):
        lo = j * FF_CHUNK
        zeros = None if between is None else between()
        gate = _dot(xm, wgu_ref[:, lo:lo + FF_CHUNK])
        up = _dot(xm, wgu_ref[:, D_FF + lo:D_FF + lo + FF_CHUNK])
        hidden = gate * jax.nn.sigmoid(gate) * up
        if zeros is not None:
            hidden = _tie(hidden, zeros)
        hid_ref[:, lo:lo + FF_CHUNK] = hidden.astype(BF16)
    return _dot(hid_ref[...], wd_ref[...])


def _ffn1_inproj(x, ada, g1_ref, wgu_ref, wd_ref, g2_ref, win_ref, bin_ref, wg_ref, bg_ref, hid_ref,
                 store_z, between=None):
    bb, tt, _ = x.shape
    n = bb * tt
    xm = _rms(x, g1_ref[...]) * (1.0 + _ada_row(ada, 1)) + _ada_row(ada, 0)
    ff = _swiglu(xm.reshape(n, D_MODEL).astype(BF16), wgu_ref, wd_ref, hid_ref, between)
    x1 = x + 0.5 * _ada_row(ada, 2) * ff.reshape(bb, tt, D_MODEL)
    hm = _rms(x1, g2_ref[...]) * (1.0 + _ada_row(ada, 4)) + _ada_row(ada, 3)
    hm = hm.reshape(n, D_MODEL).astype(BF16)
    for c0 in range(0, D_Z, D_CONV):
        store_z(c0, _dot(hm, win_ref[:, c0:c0 + D_CONV]) + bin_ref[:, c0:c0 + D_CONV])
    return x1, _dot(hm, wg_ref[...]) + bg_ref[...]


def _run(steps):
    for _ in steps:
        pass


def _all_bits(*arrays):
    acc = None
    for a in arrays:
        bits = lax.bitcast_convert_type(a.reshape(-1, a.shape[-1]), jnp.uint32)
        for r in range(0, bits.shape[0], SUBLANES):
            for c in range(0, bits.shape[1], LANES):
                v = bits[r:r + SUBLANES, c:c + LANES]
                acc = v if acc is None else acc | v
    return acc


def _conv_group(u, ubuf, ush, w_ref, cb_ref, lg_ref, lb_ref, store_y, store_tail, *, bt, rt):
    bb, tt, _ = u.shape
    sh_rows = TAIL_PAD - SUBLANES + tt
    ubuf[:, TAIL_PAD:TAIL_PAD + tt, :] = u
    yield None
    for r in range(1, SUBLANES):
        ush[r - 1] = ubuf[:, r:r + sh_rows, :]
        yield None
    for b0 in range(0, bb, bt):
        for r0 in range(0, tt, rt):
            acc = jnp.broadcast_to(cb_ref[...].reshape(1, 1, D_CONV), (bt, rt, D_CONV))
            for j in range(CONV_WIDTH):
                q, r = divmod(TAIL_LO + j, SUBLANES)
                p0 = r0 + q * SUBLANES
                if r == 0:
                    win = ubuf[b0:b0 + bt, p0:p0 + rt, :]
                else:
                    win = ush[r - 1, b0:b0 + bt, p0:p0 + rt, :]
                acc = acc + w_ref[j:j + 1, :].reshape(1, 1, D_CONV) * win
            mu = jnp.mean(acc, axis=-1, keepdims=True)
            xc = acc - mu
            var = jnp.mean(xc * xc, axis=-1, keepdims=True)
            yn = xc * lax.rsqrt(var + EPS) * lg_ref[...].reshape(1, 1, D_CONV) + lb_ref[...].reshape(1, 1, D_CONV)
            out = yn * jax.nn.sigmoid(yn)
            store_y(b0, r0, out)
            yield _all_bits(out)
    tail = ubuf[:, TAIL_LO + tt:TAIL_PAD + tt, :]
    ubuf[:, TAIL_LO:TAIL_PAD, :] = tail
    store_tail(tail)


def _scan_rows(x, op, ident):
    row = lax.broadcasted_iota(jnp.int32, x.shape, 0)
    s = 1
    while s < x.shape[0]:
        x = op(x, jnp.where(row >= s, pltpu.roll(x, s, axis=0), ident))
        s *= 2
    return x


def _mlstm_group(gi, gf, load_z, store_h, mg_ref, cext, mcar):
    row = lax.broadcasted_iota(jnp.int32, (GROUP, GROUP), 0)
    col = lax.broadcasted_iota(jnp.int32, (GROUP, GROUP), 1)
    causal = col <= row
    ones_b = jnp.ones((GROUP, HEAD_DIM), BF16)
    cum = _scan_rows(_log_sigmoid(gf), jnp.add, 0.0)
    gv = gi - cum
    mx = _scan_rows(gv, jnp.maximum, -jnp.inf)
    mprev = mcar[...]
    mm = jnp.maximum(mprev, mx)
    mcol = cum + mm
    mm_last = mm[GROUP - 1:GROUP, :]
    w_k = jnp.exp(gv - mm_last)
    decay = jnp.exp(mprev - mm_last)
    mcar[...] = mcol[GROUP - 1:GROUP, :]
    gv_t = gv.T
    yield _all_bits(gv_t[0:SUBLANES, :])
    for h in range(N_HEADS):
        sl = slice(h * HEAD_DIM, (h + 1) * HEAD_DIM)
        mm_b = jnp.broadcast_to(mm[:, h:h + 1], (GROUP, GROUP))
        mcol_b = jnp.broadcast_to(mcol[:, h:h + 1], (GROUP, GROUP))
        w_intra = jnp.exp(jnp.where(causal, gv_t[h:h + 1, :] - mm_b, -jnp.inf))
        w_inter = jnp.exp(mprev[:, h:h + 1] - mm_b)
        qb = load_z(0, h).astype(BF16)
        kh = load_z(1, h) * (HEAD_DIM ** -0.5)
        v_ext = jnp.concatenate([load_z(2, h).astype(BF16), ones_b], axis=1)
        s = lax.dot_general(qb, kh.astype(BF16), NT_DIMS, preferred_element_type=F32)
        pv = _dot((s * w_intra).astype(BF16), v_ext)
        cprev = cext[h]
        qc = _dot(qb, cprev.astype(BF16))
        num = pv[:, :HEAD_DIM] + w_inter * qc[:, :HEAD_DIM]
        den = pv[:, HEAD_DIM:] + w_inter * qc[:, HEAD_DIM:]
        hh = num / jnp.maximum(jnp.abs(den), jnp.exp(-mcol_b))
        hh = hh * lax.rsqrt(jnp.mean(hh * hh, axis=-1, keepdims=True) + EPS) * mg_ref[:, sl]
        out = jax.nn.sigmoid(load_z(3, h)) * hh
        store_h(h, out)
        kw = (kh * w_k[:, h:h + 1]).astype(BF16)
        dc = lax.dot_general(kw, v_ext, TN_DIMS, preferred_element_type=F32)
        cnew = decay[:, h:h + 1] * cprev + dc
        cext[h] = cnew
        yield _all_bits(out, cnew)


def _piece_feeder(pieces, costs, call_weights):
    point_weights = [w / TIE_PARTS for w in call_weights for _ in range(TIE_PARTS)]
    done = [0, 0]

    def one_point():
        done[0] += 1
        target = sum(costs) * sum(point_weights[:done[0]]) / sum(point_weights)
        bits = None
        while done[1] < len(costs) and sum(costs[:done[1]]) + costs[done[1]] / 2 <= target:
            piece_bits = next(pieces)
            if piece_bits is not None:
                bits = piece_bits if bits is None else bits | piece_bits
            done[1] += 1
        if bits is None:
            return None
        zero_bits = lax.shift_right_logical(lax.shift_right_logical(bits, jnp.uint32(16)), jnp.uint32(16))
        return lax.bitcast_convert_type(zero_bits, F32)

    return lambda: [one_point() for _ in range(TIE_PARTS)]


def _weight_export(k, vmem_refs, out_refs, sems):
    return pltpu.make_async_copy(vmem_refs[k], out_refs[k], sems.at[k])


def _cast_scratch(n_exports):
    return [
        pltpu.VMEM((CAST_SLOTS, CAST_ROWS_WIDE, 2 * D_FF), F32),
        pltpu.VMEM((CAST_SLOTS, CAST_ROWS_NARROW, D_MODEL), F32),
        pltpu.SemaphoreType.DMA((CAST_SLOTS,)),
        pltpu.SemaphoreType.DMA((CAST_SLOTS,)),
        pltpu.SemaphoreType.DMA((n_exports,)),
    ]


def _stream_rows(src, n_chunks, stage, sems, consume):
    depth, chunk_rows, cols = stage.shape

    def load(i):
        return pltpu.make_async_copy(src.at[pl.ds(i * chunk_rows, chunk_rows), pl.ds(0, cols)],
                                     stage.at[i % depth], sems.at[i % depth])

    for i in range(min(depth - 1, n_chunks)):
        load(i).start()
    for i in range(n_chunks):
        if i + depth - 1 < n_chunks:
            load(i + depth - 1).start()
        load(i).wait()
        consume(i, stage[i % depth])


def _weights_to_bf16(wgu_hbm, wd_hbm, third_hbm, wgu_ref, wd_ref, third_ref, third_is_transposed,
                     stage_wide, stage_narrow, sems_wide, sems_narrow):
    wide_rows, narrow_rows = stage_wide.shape[1], stage_narrow.shape[1]

    def put_wgu(i, chunk):
        wgu_ref[i * wide_rows:(i + 1) * wide_rows, :] = chunk.astype(BF16)

    def put_wd(i, chunk):
        wd_ref[i * narrow_rows:(i + 1) * narrow_rows, :] = chunk.astype(BF16)

    def put_third(i, chunk):
        if third_is_transposed:
            third_ref[:, i * narrow_rows:(i + 1) * narrow_rows] = chunk.T.astype(BF16)
        else:
            third_ref[i * narrow_rows:(i + 1) * narrow_rows, :] = chunk.astype(BF16)

    third_rows = third_ref.shape[1] if third_is_transposed else third_ref.shape[0]
    _stream_rows(wgu_hbm, wgu_ref.shape[0] // wide_rows, stage_wide, sems_wide, put_wgu)
    _stream_rows(wd_hbm, wd_ref.shape[0] // narrow_rows, stage_narrow, sems_narrow, put_wd)
    _stream_rows(third_hbm, third_rows // narrow_rows, stage_narrow, sems_narrow, put_third)


def _front_kernel(x_ref, ada_ref, wgu_hbm, wd_hbm, win_hbm, g1_ref, g2_ref, bin_ref, wg_ref, bg_ref, mg_ref,
                  x1_ref, u_ref, h_ref, c_ref, n_ref, m_ref, wgu_out, wd_out, win_out,
                  hid_ref, zbuf, gbuf, cext, mcar, wgu_ref, wd_ref, win_ref,
                  stage_wide, stage_narrow, sems_wide, sems_narrow, out_sems, *, tt, nt, steps):
    s = pl.program_id(0)
    tl = lax.rem(jnp.maximum(s - 1, 0), nt)
    resident = (wgu_ref, wd_ref, win_ref)
    exported = (wgu_out, wd_out, win_out)

    @pl.when(s == 0)
    def _():
        _weights_to_bf16(wgu_hbm, wd_hbm, win_hbm, wgu_ref, wd_ref, win_ref, True,
                         stage_wide, stage_narrow, sems_wide, sems_narrow)
        for k in range(len(resident)):
            _weight_export(k, resident, exported, out_sems).start()
        zbuf[...] = jnp.zeros(zbuf.shape, F32)
        gbuf[...] = jnp.zeros(gbuf.shape, F32)

    @pl.when(s == steps)
    def _():
        for k in range(len(resident)):
            _weight_export(k, resident, exported, out_sems).wait()

    @pl.when(tl == 0)
    def _():
        cext[...] = jnp.zeros(cext.shape, F32)
        mcar[...] = jnp.zeros(mcar.shape, F32)

    def mlstm_pieces(g):
        rs = slice(g * GROUP, (g + 1) * GROUP)

        def load_z(which, h):
            c0 = which * D_MLSTM + h * HEAD_DIM
            return zbuf[rs, c0:c0 + HEAD_DIM]

        def store_h(h, val):
            h_ref[0, rs, h * HEAD_DIM:(h + 1) * HEAD_DIM] = val.astype(BF16)

        return _mlstm_group(gbuf[rs, 0:LANES], gbuf[rs, LANES:D_GATE], load_z, store_h, mg_ref, cext, mcar)

    def mixer_pieces():
        for g in range(tt // GROUP):
            yield from mlstm_pieces(g)

    pieces = mixer_pieces()
    costs = ([10] + [5] * N_HEADS) * (tt // GROUP)
    glu_in = []

    def store_z(c0, zc):
        if c0 < Q_OFF:
            glu_in.append(zc)
            if len(glu_in) == 2:
                u_ref[0] = glu_in[0] * jax.nn.sigmoid(glu_in[1])
        else:
            if c0 == Q_OFF:
                _run(pieces)
            zbuf[:, c0 - Q_OFF:c0 - Q_OFF + D_CONV] = zc

    seq_id = jnp.minimum(s, steps - 1) // nt
    x1, gates = _ffn1_inproj(x_ref[...], ada_ref[pl.ds(seq_id, 1), :], g1_ref, wgu_ref, wd_ref, g2_ref,
                             win_ref, bin_ref, wg_ref, bg_ref, hid_ref, store_z,
                             _piece_feeder(pieces, costs, [1.0] * N_FF_CHUNKS))
    x1_ref[...] = x1
    gbuf[...] = gates

    @pl.when(tl == nt - 1)
    def _():
        for h in range(N_HEADS):
            c_ref[0, h] = cext[h, :, :HEAD_DIM]
            n_ref[0, h:h + 1, :] = cext[h, :, HEAD_DIM:].T[0:1, :]
        m_ref[0] = mcar[...]


def _front(x, ada, p, *, tt):
    bsz, seq, _ = x.shape
    nt = seq // tt
    steps = bsz * nt
    cur = lambda s: jnp.minimum(s, steps - 1)
    lag = lambda s: jnp.maximum(s - 1, 0)
    big = [p["wgu1"], p["wd1"], p["win"]]
    big_shapes = [(D_MODEL, 2 * D_FF), (D_FF, D_MODEL), (D_MODEL, D_Z)]
    small = [p["g1"], p["g2"], p["b_main"], p["wg"], p["b_gate"], p["mn_g"]]
    any_spec = pl.BlockSpec(memory_space=pl.ANY)
    return pl.pallas_call(
        functools.partial(_front_kernel, tt=tt, nt=nt, steps=steps),
        grid=(steps + 1,),
        in_specs=[
            pl.BlockSpec((1, tt, D_MODEL), lambda s: (cur(s) // nt, cur(s) % nt, 0)),
            _const_spec(ada.shape),
        ] + [any_spec] * len(big) + [_const_spec(w.shape) for w in small],
        out_specs=[
            pl.BlockSpec((1, tt, D_MODEL), lambda s: (cur(s) // nt, cur(s) % nt, 0)),
            pl.BlockSpec((1, tt, D_CONV), lambda s: (cur(s) // nt, cur(s) % nt, 0)),
            pl.BlockSpec((1, tt, D_MLSTM), lambda s: (lag(s) // nt, lag(s) % nt, 0)),
            pl.BlockSpec((1, N_HEADS, HEAD_DIM, HEAD_DIM), lambda s: (lag(s) // nt, 0, 0, 0)),
            pl.BlockSpec((1, N_HEADS, HEAD_DIM), lambda s: (lag(s) // nt, 0, 0)),
            pl.BlockSpec((1, 1, LANES), lambda s: (lag(s) // nt, 0, 0)),
        ] + [any_spec] * len(big),
        out_shape=[
            jax.ShapeDtypeStruct(x.shape, F32),
            jax.ShapeDtypeStruct((bsz, seq, D_CONV), F32),
            jax.ShapeDtypeStruct((bsz, seq, D_MLSTM), BF16),
            jax.ShapeDtypeStruct((bsz, N_HEADS, HEAD_DIM, HEAD_DIM), F32),
            jax.ShapeDtypeStruct((bsz, N_HEADS, HEAD_DIM), F32),
            jax.ShapeDtypeStruct((bsz, 1, LANES), F32),
        ] + [jax.ShapeDtypeStruct(shape, BF16) for shape in big_shapes],
        scratch_shapes=[
            pltpu.VMEM((tt, D_FF), BF16),
            pltpu.VMEM((tt, 4 * D_MLSTM), F32),
            pltpu.VMEM((tt, D_GATE), F32),
            pltpu.VMEM((N_HEADS, HEAD_DIM, 2 * HEAD_DIM), F32),
            pltpu.VMEM((1, LANES), F32),
        ] + [pltpu.VMEM(shape, BF16) for shape in big_shapes] + _cast_scratch(len(big)),
        compiler_params=_params(1),
        name="front",
    )(x, ada, *big, *small)


def _back_kernel(x_ref, ada_ref, u_ref, h_ref, wgu_hbm, wd_hbm, wout_hbm, g3_ref, gf_ref,
                 cw_ref, cb_ref, lg_ref, lb_ref,
                 o_ref, cs_ref, wgu_out, wd_out, wout_out,
                 hid_ref, ubuf, ush, ycur, ynext, wgu_ref, wd_ref, wout_ref,
                 stage_wide, stage_narrow, sems_wide, sems_narrow, out_sems, *, tt, nt, rt, steps):
    s = pl.program_id(0)
    tc = lax.rem(jnp.minimum(s, steps - 1), nt)
    resident = (wgu_ref, wd_ref, wout_ref)
    exported = (wgu_out, wd_out, wout_out)

    @pl.when(s == 0)
    def _():
        _weights_to_bf16(wgu_hbm, wd_hbm, wout_hbm, wgu_ref, wd_ref, wout_ref, False,
                         stage_wide, stage_narrow, sems_wide, sems_narrow)
        for k in range(len(resident)):
            _weight_export(k, resident, exported, out_sems).start()
        ycur[...] = jnp.zeros(ycur.shape, BF16)

    @pl.when(s == steps)
    def _():
        for k in range(len(resident)):
            _weight_export(k, resident, exported, out_sems).wait()

    @pl.when(tc == 0)
    def _():
        ubuf[:, 0:TAIL_PAD, :] = jnp.zeros((1, TAIL_PAD, D_CONV), F32)

    def store_y(b0, r0, val):
        ynext[r0:r0 + rt, :] = val[0].astype(BF16)

    pieces = _conv_group(u_ref[...], ubuf, ush, cw_ref, cb_ref, lg_ref, lb_ref, store_y, lambda tail: None,
                         bt=1, rt=rt)
    costs = [3] + [10] * (SUBLANES - 1) + [3 * rt // SUBLANES] * (tt // rt)

    x = x_ref[...]
    ada = ada_ref[pl.ds(jnp.maximum(s - 1, 0) // nt, 1), :]
    between = _piece_feeder(pieces, costs, [1.0] * (1 + N_FF_CHUNKS))
    mix = _dot(ycur[...], wout_ref[:D_CONV, :]) + _dot(h_ref[0], wout_ref[D_CONV:, :])
    x2 = x + _ada_row(ada, 5) * _tie(mix, between()).reshape(1, tt, D_MODEL)
    xm = _rms(x2, g3_ref[...]) * (1.0 + _ada_row(ada, 7)) + _ada_row(ada, 6)
    ff = _swiglu(xm.reshape(tt, D_MODEL).astype(BF16), wgu_ref, wd_ref, hid_ref, between)
    _run(pieces)
    x3 = x2 + 0.5 * _ada_row(ada, 8) * ff.reshape(1, tt, D_MODEL)
    o_ref[...] = _rms(x3, gf_ref[...])
    ycur[...] = ynext[...]

    @pl.when(jnp.logical_and(tc == nt - 1, s < steps))
    def _():
        cs_ref[...] = ubuf[:, TAIL_LO:TAIL_PAD, :]


def _back(x1, ada, u, h, p, *, tt, rt):
    bsz, seq, _ = x1.shape
    nt = seq // tt
    steps = bsz * nt
    cur = lambda s: jnp.minimum(s, steps - 1)
    lag = lambda s: jnp.maximum(s - 1, 0)
    big = [p["wgu2"], p["wd2"], p["wout"]]
    big_shapes = [(D_MODEL, 2 * D_FF), (D_FF, D_MODEL), (D_MODEL, D_MODEL)]
    small = [p["g3"], p["gf"], p["conv_w"], p["conv_b"], p["cn_g"], p["cn_b"]]
    any_spec = pl.BlockSpec(memory_space=pl.ANY)
    return pl.pallas_call(
        functools.partial(_back_kernel, tt=tt, nt=nt, rt=rt, steps=steps),
        grid=(steps + 1,),
        in_specs=[
            pl.BlockSpec((1, tt, D_MODEL), lambda s: (lag(s) // nt, lag(s) % nt, 0)),
            _const_spec(ada.shape),
            pl.BlockSpec((1, tt, D_CONV), lambda s: (cur(s) // nt, cur(s) % nt, 0)),
            pl.BlockSpec((1, tt, D_MLSTM), lambda s: (lag(s) // nt, lag(s) % nt, 0)),
        ] + [any_spec] * len(big) + [_const_spec(w.shape) for w in small],
        out_specs=[
            pl.BlockSpec((1, tt, D_MODEL), lambda s: (lag(s) // nt, lag(s) % nt, 0)),
            pl.BlockSpec((1, CONV_TAIL, D_CONV), lambda s: (cur(s) // nt, 0, 0)),
        ] + [any_spec] * len(big),
        out_shape=[
            jax.ShapeDtypeStruct(x1.shape, F32),
            jax.ShapeDtypeStruct((bsz, CONV_TAIL, D_CONV), F32),
        ] + [jax.ShapeDtypeStruct(shape, BF16) for shape in big_shapes],
        scratch_shapes=[
            pltpu.VMEM((tt, D_FF), BF16),
            pltpu.VMEM((1, TAIL_PAD + tt, D_CONV), F32),
            pltpu.VMEM((SUBLANES - 1, 1, TAIL_PAD - SUBLANES + tt, D_CONV), F32),
            pltpu.VMEM((tt, D_CONV), BF16),
            pltpu.VMEM((tt, D_CONV), BF16),
        ] + [pltpu.VMEM(shape, BF16) for shape in big_shapes] + _cast_scratch(len(big)),
        compiler_params=_params(1),
        name="back",
    )(x1, ada, u, h, *big, *small)


def _ffn_a_kernel(x_ref, ada_ref, g1_ref, wgu_ref, wd_ref, g2_ref, win_ref, bin_ref, wg_ref, bg_ref,
                  x1_ref, z_ref, gate_ref, hid_ref):
    def store_z(c0, zc):
        z_ref[:, c0:c0 + D_CONV] = zc

    x1, gates = _ffn1_inproj(x_ref[...], ada_ref[...], g1_ref, wgu_ref, wd_ref, g2_ref,
                             win_ref, bin_ref, wg_ref, bg_ref, hid_ref, store_z)
    x1_ref[...] = x1
    gate_ref[...] = gates


def _ffn_a(x, ada, p, *, bb):
    bsz, seq, _ = x.shape
    n = bb * seq
    weights = [p["g1"], p["wgu1"], p["wd1"], p["g2"], p["win"], p["b_main"], p["wg"], p["b_gate"]]
    return pl.pallas_call(
        _ffn_a_kernel,
        grid=(bsz // bb,),
        in_specs=[
            pl.BlockSpec((bb, seq, D_MODEL), lambda b: (b, 0, 0)),
            pl.BlockSpec((bb, N_ADA * D_MODEL), lambda b: (b, 0)),
        ] + [_const_spec(w.shape) for w in weights],
        out_specs=[
            pl.BlockSpec((bb, seq, D_MODEL), lambda b: (b, 0, 0)),
            pl.BlockSpec((n, D_Z), lambda b: (b, 0)),
            pl.BlockSpec((n, D_GATE), lambda b: (b, 0)),
        ],
        out_shape=[
            jax.ShapeDtypeStruct(x.shape, F32),
            jax.ShapeDtypeStruct((bsz * seq, D_Z), F32),
            jax.ShapeDtypeStruct((bsz * seq, D_GATE), F32),
        ],
        scratch_shapes=[pltpu.VMEM((n, D_FF), BF16)],
        compiler_params=_params(1),
        name="ffn_a",
    )(x, ada, *weights)


def _conv_kernel(a_ref, b_ref, cs_ref, w_ref, cb_ref, lg_ref, lb_ref, y_ref, so_ref, ubuf, ush, *, bt, rt):
    ubuf[:, TAIL_LO:TAIL_PAD, :] = cs_ref[...]

    def store_y(b0, r0, val):
        y_ref[b0:b0 + bt, r0:r0 + rt, :] = val

    def store_tail(tail):
        so_ref[...] = tail

    u = a_ref[...] * jax.nn.sigmoid(b_ref[...])
    _run(_conv_group(u, ubuf, ush, w_ref, cb_ref, lg_ref, lb_ref, store_y, store_tail, bt=bt, rt=rt))


def _conv(z3, state, p, *, bb, bt):
    bsz, seq, _ = z3.shape
    weights = [p["conv_w"], p["conv_b"], p["cn_g"], p["cn_b"]]
    return pl.pallas_call(
        functools.partial(_conv_kernel, bt=bt, rt=seq),
        grid=(bsz // bb,),
        in_specs=[
            pl.BlockSpec((bb, seq, D_CONV), lambda b: (b, 0, 0)),
            pl.BlockSpec((bb, seq, D_CONV), lambda b: (b, 0, 1)),
            pl.BlockSpec((bb, CONV_TAIL, D_CONV), lambda b: (b, 0, 0)),
        ] + [_const_spec(w.shape) for w in weights],
        out_specs=[
            pl.BlockSpec((bb, seq, D_CONV), lambda b: (b, 0, 0)),
            pl.BlockSpec((bb, CONV_TAIL, D_CONV), lambda b: (b, 0, 0)),
        ],
        out_shape=[
            jax.ShapeDtypeStruct((bsz, seq, D_CONV), F32),
            jax.ShapeDtypeStruct((bsz, CONV_TAIL, D_CONV), F32),
        ],
        scratch_shapes=[
            pltpu.VMEM((bb, TAIL_PAD + seq, D_CONV), F32),
            pltpu.VMEM((SUBLANES - 1, bb, TAIL_PAD - SUBLANES + seq, D_CONV), F32),
        ],
        compiler_params=_params(1),
        name="conv",
    )(z3, z3, state, *weights)


def _mlstm_sample_kernel(q_ref, k_ref, v_ref, o_ref, gt_ref, mg_ref, c0_ref, n0_ref, m0_ref,
                         h_ref, c_ref, n_ref, m_ref, *, nseq, ls):
    row = lax.broadcasted_iota(jnp.int32, (GROUP, GROUP), 0)
    col = lax.broadcasted_iota(jnp.int32, (GROUP, GROUP), 1)
    shift = ls.bit_length() - 1
    seg_r = lax.shift_right_logical(row, shift)
    seg_c = lax.shift_right_logical(col, shift)
    causal = jnp.logical_and(col <= row, seg_r == seg_c)
    segtril = causal.astype(F32)
    lastsel = (col == seg_r * ls + (ls - 1)).astype(F32)

    gates = gt_ref[...].reshape(GROUP, D_GATE)
    gi = gates[:, :LANES]
    cum = _dot_exact(segtril, _log_sigmoid(gates[:, LANES:]))
    cumlast = _dot_exact(lastsel, cum)
    cum_t = cum.T
    gi_t = gi.T
    mprev = m0_ref[...].reshape(GROUP, LANES)
    qg = q_ref[...].reshape(GROUP, D_MLSTM)
    kg = k_ref[...].reshape(GROUP, D_MLSTM) * (HEAD_DIM ** -0.5)
    vg = v_ref[...].reshape(GROUP, D_MLSTM)
    og = o_ref[...].reshape(GROUP, D_MLSTM)

    m_all = jnp.zeros((GROUP, LANES), F32)
    lane = lax.broadcasted_iota(jnp.int32, (GROUP, LANES), 1)
    hs = []
    for h in range(N_HEADS):
        sl = slice(h * HEAD_DIM, (h + 1) * HEAD_DIM)
        cum_col = cum[:, h:h + 1]
        dmat = jnp.where(causal, cum_col - cum_t[h:h + 1, :] + gi_t[h:h + 1, :], -jnp.inf)
        inter = cum_col + mprev[:, h:h + 1]
        m_col = jnp.maximum(inter, jnp.max(dmat, axis=-1, keepdims=True))
        w_intra = jnp.exp(dmat - m_col)
        w_inter = jnp.exp(inter - m_col)
        qh = qg[:, sl]
        qb = qh.astype(BF16)
        s = lax.dot_general(qb, kg[:, sl].astype(BF16), NT_DIMS, preferred_element_type=F32)
        p = s * w_intra
        num = _dot(p.astype(BF16), vg[:, sl].astype(BF16))
        den = jnp.sum(p, axis=-1, keepdims=True)
        qc = jnp.zeros((GROUP, HEAD_DIM), F32)
        qn = jnp.zeros((GROUP, 1), F32)
        for sq in range(nseq):
            qc = jnp.where(seg_r == sq, _dot(qb, c0_ref[sq, h].astype(BF16)), qc)
            qn = jnp.where(seg_r[:, 0:1] == sq,
                           jnp.sum(qh * n0_ref[sq, h:h + 1, :], axis=-1, keepdims=True), qn)
        num = num + w_inter * qc
        den = den + w_inter * qn
        hh = num / jnp.maximum(jnp.abs(den), jnp.exp(-m_col))
        hh = hh * lax.rsqrt(jnp.mean(hh * hh, axis=-1, keepdims=True) + EPS) * mg_ref[:, sl]
        hs.append(jax.nn.sigmoid(og[:, sl]) * hh)
        m_all = jnp.where(lane == h, m_col, m_all)

    h_ref[...] = jnp.concatenate(hs, axis=-1).reshape(h_ref.shape)
    m_ref[...] = m_all.reshape(m_ref.shape)

    mnew = _dot_exact(lastsel, m_all)
    for h in range(N_HEADS):
        sl = slice(h * HEAD_DIM, (h + 1) * HEAD_DIM)
        last_col = cumlast[:, h:h + 1]
        mnew_col = mnew[:, h:h + 1]
        w_k = jnp.exp(last_col - cum[:, h:h + 1] + gi[:, h:h + 1] - mnew_col)
        decay = jnp.exp(last_col + mprev[:, h:h + 1] - mnew_col)
        kw = kg[:, sl] * w_k
        kwb = kw.astype(BF16)
        vh = vg[:, sl]
        for sq in range(nseq):
            vs = jnp.where(seg_r == sq, vh, 0.0)
            kws = jnp.where(seg_r == sq, kw, 0.0)
            dc = lax.dot_general(kwb, vs.astype(BF16), TN_DIMS, preferred_element_type=F32)
            dn = jnp.sum(kws, axis=0, keepdims=True)
            dec = decay[sq * ls:sq * ls + 1, :]
            c_ref[sq, h] = dec * c0_ref[sq, h] + dc
            n_ref[sq, h:h + 1, :] = dec * n0_ref[sq, h:h + 1, :] + dn


def _mlstm_sample(z3, gates3, mg, c0, n0, m_tok, *, bb):
    bsz, seq, _ = z3.shape
    assert bb * seq == GROUP
    zspec = lambda k: pl.BlockSpec((bb, seq, D_MLSTM), lambda b: (b, 0, k))
    c_spec = pl.BlockSpec((bb, N_HEADS, HEAD_DIM, HEAD_DIM), lambda b: (b, 0, 0, 0))
    n_spec = pl.BlockSpec((bb, N_HEADS, HEAD_DIM), lambda b: (b, 0, 0))
    m_spec = pl.BlockSpec((bb, seq, LANES), lambda b: (b, 0, 0))
    return pl.pallas_call(
        functools.partial(_mlstm_sample_kernel, nseq=bb, ls=seq),
        grid=(bsz // bb,),
        in_specs=[zspec(2), zspec(3), zspec(4), zspec(5),
                  pl.BlockSpec((bb, seq, D_GATE), lambda b: (b, 0, 0)),
                  _const_spec(mg.shape),
                  c_spec, n_spec, m_spec],
        out_specs=[pl.BlockSpec((bb, seq, D_MLSTM), lambda b: (b, 0, 0)), c_spec, n_spec, m_spec],
        out_shape=[
            jax.ShapeDtypeStruct((bsz, seq, D_MLSTM), F32),
            jax.ShapeDtypeStruct((bsz, N_HEADS, HEAD_DIM, HEAD_DIM), F32),
            jax.ShapeDtypeStruct((bsz, N_HEADS, HEAD_DIM), F32),
            jax.ShapeDtypeStruct((bsz, seq, LANES), F32),
        ],
        compiler_params=_params(1),
        name="mlstm_sample",
    )(z3, z3, z3, z3, gates3, mg, c0, n0, m_tok)


def _ffn_b_kernel(x_ref, ada_ref, y_ref, h_ref, wout_ref, g3_ref, wgu_ref, wd_ref, gf_ref, o_ref, hid_ref):
    x = x_ref[...]
    bb, tt, _ = x.shape
    n = bb * tt
    ada = ada_ref[...]
    mix = (_dot(y_ref[...].reshape(n, D_CONV).astype(BF16), wout_ref[:D_CONV, :])
           + _dot(h_ref[...].reshape(n, D_MLSTM).astype(BF16), wout_ref[D_CONV:, :]))
    x2 = x + _ada_row(ada, 5) * mix.reshape(bb, tt, D_MODEL)
    xm = _rms(x2, g3_ref[...]) * (1.0 + _ada_row(ada, 7)) + _ada_row(ada, 6)
    ff = _swiglu(xm.reshape(n, D_MODEL).astype(BF16), wgu_ref, wd_ref, hid_ref)
    x3 = x2 + 0.5 * _ada_row(ada, 8) * ff.reshape(bb, tt, D_MODEL)
    o_ref[...] = _rms(x3, gf_ref[...])


def _ffn_b(x, ada, y, h, p, *, bb):
    bsz, seq, _ = x.shape
    tok = lambda w: pl.BlockSpec((bb, seq, w), lambda b: (b, 0, 0))
    weights = [p["wout"], p["g3"], p["wgu2"], p["wd2"], p["gf"]]
    return pl.pallas_call(
        _ffn_b_kernel,
        grid=(bsz // bb,),
        in_specs=[tok(D_MODEL), pl.BlockSpec((bb, N_ADA * D_MODEL), lambda b: (b, 0)), tok(D_CONV), tok(D_MLSTM)]
        + [_const_spec(w.shape) for w in weights],
        out_specs=tok(D_MODEL),
        out_shape=jax.ShapeDtypeStruct(x.shape, F32),
        scratch_shapes=[pltpu.VMEM((bb * seq, D_FF), BF16)],
        compiler_params=_params(1),
        name="ffn_b",
    )(x, ada, y, h, *weights)


def _prompt_trunk(x, ada, p):
    x1, u, h, c_new, n_new, m_row, wgu1, wd1, win = _front(x, ada, p, tt=512)
    out, conv_new, wgu2, wd2, wout = _back(x1, ada, u, h, p, tt=512, rt=16)
    bf16_weights = dict(wgu1=wgu1, wd1=wd1, win=win, wgu2=wgu2, wd2=wd2, wout=wout)
    return (out, conv_new[None], c_new[None], n_new[None], m_row[None, :, 0, :N_HEADS]), bf16_weights


def _sample_trunk(x, ada, states, p):
    bsz, seq, _ = x.shape
    conv_state, c0, n0, m0 = states
    x1, z, gates = _ffn_a(x, ada, p, bb=32)
    z3 = z.reshape(bsz, seq, D_Z)
    gates3 = gates.reshape(bsz, seq, D_GATE)
    y, conv_new = _conv(z3, conv_state, p, bb=32, bt=8)
    m_tok = jnp.broadcast_to(
        jnp.pad(m0, ((0, 0), (0, LANES - N_HEADS)))[:, None, :], (bsz, seq, LANES))
    h, c_new, n_new, m_tok_new = _mlstm_sample(z3, gates3, p["mn_g"], c0, n0, m_tok, bb=GROUP // seq)
    out = _ffn_b(x1, ada, y, h, p, bb=64)
    return out, conv_new[None], c_new[None], n_new[None], m_tok_new[None, :, seq - 1, :N_HEADS]


def _gate_cols(a):
    pad = ((0, 0), (0, LANES - N_HEADS))
    return jnp.concatenate([jnp.pad(a[:, :N_HEADS], pad), jnp.pad(a[:, N_HEADS:], pad)], axis=1)


def kernel(x_prompt, x_sample, c_prompt, c_sample, state_conv, state_C, state_n, state_m, w_ada, b_ada, norm_ffn1, ffn1_w_gu, ffn1_w_down, norm_mix, w_in, b_in, conv_w, conv_b, conv_norm_g, conv_norm_b, mlstm_norm_g, w_out, norm_ffn2, ffn2_w_gu, ffn2_w_down, norm_final):
    assert w_ada.shape[0] == 1, "single layer"
    w_in_t = jnp.transpose(w_in[0])
    p = {
        "g1": norm_ffn1, "g2": norm_mix, "g3": norm_ffn2, "gf": norm_final[None],
        "wgu1": ffn1_w_gu[0], "wd1": ffn1_w_down[0], "wgu2": ffn2_w_gu[0], "wd2": ffn2_w_down[0],
        "win": w_in_t, "wout": w_out[0],
        "wg": _gate_cols(jnp.transpose(w_in_t[D_Z:])).astype(BF16),
        "b_main": b_in,
        "b_gate": _gate_cols(b_in[:, D_Z:]),
        "conv_w": jnp.pad(conv_w[0], ((0, TAIL_PAD - CONV_WIDTH), (0, 0))),
        "conv_b": conv_b, "cn_g": conv_norm_g, "cn_b": conv_norm_b, "mn_g": mlstm_norm_g,
    }
    bs = x_sample.shape[0]
    ada = _ada(jnp.concatenate([c_sample, c_prompt], axis=0), w_ada[0], b_ada)
    (yp, conv_p, c_p, n_p, m_p), bf16_weights = _prompt_trunk(x_prompt, ada[bs:], p)
    ys, conv_s, c_s, n_s, m_s = _sample_trunk(
        x_sample, ada, (state_conv[0], state_C[0], state_n[0], state_m[0]), {**p, **bf16_weights})
    return (yp, ys, conv_p, c_p, n_p, m_p, conv_s, c_s, n_s, m_s)
```

```python
import functools

import jax
import jax.numpy as jnp
from jax import lax
from jax.experimental import pallas as pl
from jax.experimental.pallas import tpu as pltpu

F32 = jnp.float32
BF16 = jnp.bfloat16

D_MODEL = 1024
D_CONV = 512
D_MLSTM = 512
N_HEADS = 4
HEAD_DIM = 128
CONV_WIDTH = 31
CONV_TAIL = CONV_WIDTH - 1
D_FF = 2816
N_ADA = 9
EPS = 1e-6
D_Z = 2 * D_CONV + 4 * D_MLSTM
D_IN = D_Z + 2 * N_HEADS
Q_OFF = 2 * D_CONV
LANES = 128
SUBLANES = 8
D_GATE = 2 * LANES
FF_CHUNK = 256
N_FF_CHUNKS = D_FF // FF_CHUNK
GROUP = 128
TAIL_PAD = 32
TAIL_LO = TAIL_PAD - CONV_TAIL
TIE_PARTS = 4
CAST_SLOTS = 3
CAST_ROWS_WIDE = 64
CAST_ROWS_NARROW = 128
VMEM_LIMIT = 56 * 1024 * 1024
NT_DIMS = (((1,), (1,)), ((), ()))
TN_DIMS = (((0,), (0,)), ((), ()))


def _dot(a, b):
    return jnp.dot(a, b, preferred_element_type=F32)


def _dot_exact(a, b):
    return jnp.dot(a, b, preferred_element_type=F32, precision=lax.Precision.HIGHEST)


def _rms(x, g):
    ms = jnp.mean(x * x, axis=-1, keepdims=True)
    return x * lax.rsqrt(ms + EPS) * g


def _ada_row(ada, k):
    return ada[:, k * D_MODEL:(k + 1) * D_MODEL][:, None, :]


def _log_sigmoid(x):
    return jnp.minimum(x, 0.0) - jnp.log(1.0 + jnp.exp(-jnp.abs(x)))


def _const_spec(shape):
    nd = len(shape)
    return pl.BlockSpec(shape, lambda *_: (0,) * nd, pipeline_mode=pl.Buffered(1))


def _params(n_grid):
    return pltpu.CompilerParams(dimension_semantics=("arbitrary",) * n_grid, vmem_limit_bytes=VMEM_LIMIT)


def _ada_kernel(c_ref, w_ref, b_ref, o_ref):
    c = c_ref[...]
    s = (c * jax.nn.sigmoid(c)).astype(BF16)
    o_ref[...] = _dot(s, w_ref[...].astype(BF16)) + b_ref[...]


def _ada(c_all, w_ada, b_ada):
    n = c_all.shape[0]
    tile = D_MODEL
    return pl.pallas_call(
        _ada_kernel,
        grid=(N_ADA,),
        in_specs=[
            pl.BlockSpec((n, D_MODEL), lambda j: (0, 0)),
            pl.BlockSpec((D_MODEL, tile), lambda j: (0, j)),
            pl.BlockSpec((1, tile), lambda j: (0, j)),
        ],
        out_specs=pl.BlockSpec((n, tile), lambda j: (0, j)),
        out_shape=jax.ShapeDtypeStruct((n, N_ADA * D_MODEL), F32),
        compiler_params=_params(1),
        name="ada",
    )(c_all, w_ada, b_ada)


def _tie(value, zeros):
    rb = value.shape[0] // TIE_PARTS
    packing = 4 // value.dtype.itemsize
    reps = (rb // (SUBLANES * packing), value.shape[1] // LANES)

    def add(block, zero):
        if zero is None:
            return block
        zero = jnp.concatenate([zero] * packing, axis=0).astype(value.dtype)
        return block + jnp.tile(zero, reps)

    return jnp.concatenate([add(value[k * rb:(k + 1) * rb], z) for k, z in enumerate(zeros)], axis=0)


def _swiglu(xm, wgu_ref, wd_ref, hid_ref, between=None):
    for j in range(N_FF_CHUNKS):
        lo = j * FF_CHUNK
        zeros = None if between is None else between()
        gate = _dot(xm, wgu_ref[:, lo:lo + FF_CHUNK])
        up = _dot(xm, wgu_ref[:, D_FF + lo:D_FF + lo + FF_CHUNK])
        hidden = gate * jax.nn.sigmoid(gate) * up
        if zeros is not None:
            hidden = _tie(hidden, zeros)
        hid_ref[:, lo:lo + FF_CHUNK] = hidden.astype(BF16)
    return _dot(hid_ref[...], wd_ref[...])


def _ffn1_inproj(x, ada, g1_ref, wgu_ref, wd_ref, g2_ref, win_ref, bin_ref, wg_ref, bg_ref, hid_ref,
                 store_z, between=None):
    bb, tt, _ = x.shape
    n = bb * tt
    xm = _rms(x, g1_ref[...]) * (1.0 + _ada_row(ada, 1)) + _ada_row(ada, 0)
    ff = _swiglu(xm.reshape(n, D_MODEL).astype(BF16), wgu_ref, wd_ref, hid_ref, between)
    x1 = x + 0.5 * _ada_row(ada, 2) * ff.reshape(bb, tt, D_MODEL)
    hm = _rms(x1, g2_ref[...]) * (1.0 + _ada_row(ada, 4)) + _ada_row(ada, 3)
    hm = hm.reshape(n, D_MODEL).astype(BF16)
    for c0 in range(0, D_Z, D_CONV):
        store_z(c0, _dot(hm, win_ref[:, c0:c0 + D_CONV]) + bin_ref[:, c0:c0 + D_CONV])
    return x1, _dot(hm, wg_ref[...]) + bg_ref[...]


def _run(steps):
    for _ in steps:
        pass


def _all_bits(*arrays):
    acc = None
    for a in arrays:
        bits = lax.bitcast_convert_type(a.reshape(-1, a.shape[-1]), jnp.uint32)
        for r in range(0, bits.shape[0], SUBLANES):
            for c in range(0, bits.shape[1], LANES):
                v = bits[r:r + SUBLANES, c:c + LANES]
                acc = v if acc is None else acc | v
    return acc


def _conv_group(u, ubuf, ush, w_ref, cb_ref, lg_ref, lb_ref, store_y, store_tail, *, bt, rt):
    bb, tt, _ = u.shape
    sh_rows = TAIL_PAD - SUBLANES + tt
    ubuf[:, TAIL_PAD:TAIL_PAD + tt, :] = u
    yield None
    for r in range(1, SUBLANES):
        ush[r - 1] = ubuf[:, r:r + sh_rows, :]
        yield None
    for b0 in range(0, bb, bt):
        for r0 in range(0, tt, rt):
            acc = jnp.broadcast_to(cb_ref[...].reshape(1, 1, D_CONV), (bt, rt, D_CONV))
            for j in range(CONV_WIDTH):
                q, r = divmod(TAIL_LO + j, SUBLANES)
                p0 = r0 + q * SUBLANES
                if r == 0:
                    win = ubuf[b0:b0 + bt, p0:p0 + rt, :]
                else:
                    win = ush[r - 1, b0:b0 + bt, p0:p0 + rt, :]
                acc = acc + w_ref[j:j + 1, :].reshape(1, 1, D_CONV) * win
            mu = jnp.mean(acc, axis=-1, keepdims=True)
            xc = acc - mu
            var = jnp.mean(xc * xc, axis=-1, keepdims=True)
            yn = xc * lax.rsqrt(var + EPS) * lg_ref[...].reshape(1, 1, D_CONV) + lb_ref[...].reshape(1, 1, D_CONV)
            out = yn * jax.nn.sigmoid(yn)
            store_y(b0, r0, out)
            yield _all_bits(out)
    tail = ubuf[:, TAIL_LO + tt:TAIL_PAD + tt, :]
    ubuf[:, TAIL_LO:TAIL_PAD, :] = tail
    store_tail(tail)


def _scan_rows(x, op, ident):
    row = lax.broadcasted_iota(jnp.int32, x.shape, 0)
    s = 1
    while s < x.shape[0]:
        x = op(x, jnp.where(row >= s, pltpu.roll(x, s, axis=0), ident))
        s *= 2
    return x


def _mlstm_group(gi, gf, load_z, store_h, mg_ref, cext, mcar):
    row = lax.broadcasted_iota(jnp.int32, (GROUP, GROUP), 0)
    col = lax.broadcasted_iota(jnp.int32, (GROUP, GROUP), 1)
    causal = col <= row
    ones_b = jnp.ones((GROUP, HEAD_DIM), BF16)
    cum = _scan_rows(_log_sigmoid(gf), jnp.add, 0.0)
    gv = gi - cum
    mx = _scan_rows(gv, jnp.maximum, -jnp.inf)
    mprev = mcar[...]
    mm = jnp.maximum(mprev, mx)
    mcol = cum + mm
    mm_last = mm[GROUP - 1:GROUP, :]
    w_k = jnp.exp(gv - mm_last)
    decay = jnp.exp(mprev - mm_last)
    mcar[...] = mcol[GROUP - 1:GROUP, :]
    gv_t = gv.T
    yield _all_bits(gv_t[0:SUBLANES, :])
    for h in range(N_HEADS):
        sl = slice(h * HEAD_DIM, (h + 1) * HEAD_DIM)
        mm_b = jnp.broadcast_to(mm[:, h:h + 1], (GROUP, GROUP))
        mcol_b = jnp.broadcast_to(mcol[:, h:h + 1], (GROUP, GROUP))
        w_intra = jnp.exp(jnp.where(causal, gv_t[h:h + 1, :] - mm_b, -jnp.inf))
        w_inter = jnp.exp(mprev[:, h:h + 1] - mm_b)
        qb = load_z(0, h).astype(BF16)
        kh = load_z(1, h) * (HEAD_DIM ** -0.5)
        v_ext = jnp.concatenate([load_z(2, h).astype(BF16), ones_b], axis=1)
        s = lax.dot_general(qb, kh.astype(BF16), NT_DIMS, preferred_element_type=F32)
        pv = _dot((s * w_intra).astype(BF16), v_ext)
        cprev = cext[h]
        qc = _dot(qb, cprev.astype(BF16))
        num = pv[:, :HEAD_DIM] + w_inter * qc[:, :HEAD_DIM]
        den = pv[:, HEAD_DIM:] + w_inter * qc[:, HEAD_DIM:]
        hh = num / jnp.maximum(jnp.abs(den), jnp.exp(-mcol_b))
        hh = hh * lax.rsqrt(jnp.mean(hh * hh, axis=-1, keepdims=True) + EPS) * mg_ref[:, sl]
        out = jax.nn.sigmoid(load_z(3, h)) * hh
        store_h(h, out)
        kw = (kh * w_k[:, h:h + 1]).astype(BF16)
        dc = lax.dot_general(kw, v_ext, TN_DIMS, preferred_element_type=F32)
        cnew = decay[:, h:h + 1] * cprev + dc
        cext[h] = cnew
        yield _all_bits(out, cnew)


def _piece_feeder(pieces, costs, call_weights):
    point_weights = [w / TIE_PARTS for w in call_weights for _ in range(TIE_PARTS)]
    done = [0, 0]

    def one_point():
        done[0] += 1
        target = sum(costs) * sum(point_weights[:done[0]]) / sum(point_weights)
        bits = None
        while done[1] < len(costs) and sum(costs[:done[1]]) + costs[done[1]] / 2 <= target:
            piece_bits = next(pieces)
            if piece_bits is not None:
                bits = piece_bits if bits is None else bits | piece_bits
            done[1] += 1
        if bits is None:
            return None
        zero_bits = lax.shift_right_logical(lax.shift_right_logical(bits, jnp.uint32(16)), jnp.uint32(16))
        return lax.bitcast_convert_type(zero_bits, F32)

    return lambda: [one_point() for _ in range(TIE_PARTS)]


def _weight_export(k, vmem_refs, out_refs, sems):
    return pltpu.make_async_copy(vmem_refs[k], out_refs[k], sems.at[k])


def _cast_scratch(n_exports):
    return [
        pltpu.VMEM((CAST_SLOTS, CAST_ROWS_WIDE, 2 * D_FF), F32),
        pltpu.VMEM((CAST_SLOTS, CAST_ROWS_NARROW, D_MODEL), F32),
        pltpu.SemaphoreType.DMA((CAST_SLOTS,)),
        pltpu.SemaphoreType.DMA((CAST_SLOTS,)),
        pltpu.SemaphoreType.DMA((n_exports,)),
    ]


def _stream_rows(src, n_chunks, stage, sems, consume):
    depth, chunk_rows, cols = stage.shape

    def load(i):
        return pltpu.make_async_copy(src.at[pl.ds(i * chunk_rows, chunk_rows), pl.ds(0, cols)],
                                     stage.at[i % depth], sems.at[i % depth])

    for i in range(min(depth - 1, n_chunks)):
        load(i).start()
    for i in range(n_chunks):
        if i + depth - 1 < n_chunks:
            load(i + depth - 1).start()
        load(i).wait()
        consume(i, stage[i % depth])


def _weights_to_bf16(wgu_hbm, wd_hbm, third_hbm, wgu_ref, wd_ref, third_ref, third_is_transposed,
                     stage_wide, stage_narrow, sems_wide, sems_narrow):
    wide_rows, narrow_rows = stage_wide.shape[1], stage_narrow.shape[1]

    def put_wgu(i, chunk):
        wgu_ref[i * wide_rows:(i + 1) * wide_rows, :] = chunk.astype(BF16)

    def put_wd(i, chunk):
        wd_ref[i * narrow_rows:(i + 1) * narrow_rows, :] = chunk.astype(BF16)

    def put_third(i, chunk):
        if third_is_transposed:
            third_ref[:, i * narrow_rows:(i + 1) * narrow_rows] = chunk.T.astype(BF16)
        else:
            third_ref[i * narrow_rows:(i + 1) * narrow_rows, :] = chunk.astype(BF16)

    third_rows = third_ref.shape[1] if third_is_transposed else third_ref.shape[0]
    _stream_rows(wgu_hbm, wgu_ref.shape[0] // wide_rows, stage_wide, sems_wide, put_wgu)
    _stream_rows(wd_hbm, wd_ref.shape[0] // narrow_rows, stage_narrow, sems_narrow, put_wd)
    _stream_rows(third_hbm, third_rows // narrow_rows, stage_narrow, sems_narrow, put_third)


def _gate_weights(win_hbm, wg_ref, stage_narrow, sems_narrow):
    rows = stage_narrow.shape[1]
    tail = pltpu.make_async_copy(win_hbm.at[pl.ds(D_IN - rows, rows), :], stage_narrow.at[0], sems_narrow.at[0])
    tail.start()
    tail.wait()
    cols = stage_narrow[0].T
    lane = lax.broadcasted_iota(jnp.int32, cols.shape, 1)
    i_part = jnp.where(lane < N_HEADS, pltpu.roll(cols, 2 * N_HEADS, axis=1), 0.0)
    f_part = jnp.where(lane < N_HEADS, pltpu.roll(cols, N_HEADS, axis=1), 0.0)
    wg_ref[...] = jnp.concatenate([i_part, f_part], axis=1).astype(BF16)


def _front_kernel(x_ref, ada_ref, wgu_hbm, wd_hbm, win_hbm, g1_ref, g2_ref, bin_ref, bg_ref, mg_ref,
                  x1_ref, u_ref, h_ref, c_ref, n_ref, m_ref, wgu_out, wd_out, win_out, wg_out,
                  hid_ref, zbuf, gbuf, cext, mcar, wgu_ref, wd_ref, win_ref, wg_ref,
                  stage_wide, stage_narrow, sems_wide, sems_narrow, out_sems, *, tt, nt, steps):
    s = pl.program_id(0)
    tl = lax.rem(jnp.maximum(s - 1, 0), nt)
    resident = (wgu_ref, wd_ref, win_ref, wg_ref)
    exported = (wgu_out, wd_out, win_out, wg_out)

    @pl.when(s == 0)
    def _():
        _weights_to_bf16(wgu_hbm, wd_hbm, win_hbm, wgu_ref, wd_ref, win_ref, True,
                         stage_wide, stage_narrow, sems_wide, sems_narrow)
        _gate_weights(win_hbm, wg_ref, stage_narrow, sems_narrow)
        for k in range(len(resident)):
            _weight_export(k, resident, exported, out_sems).start()
        zbuf[...] = jnp.zeros(zbuf.shape, F32)
        gbuf[...] = jnp.zeros(gbuf.shape, F32)

    @pl.when(s == steps)
    def _():
        for k in range(len(resident)):
            _weight_export(k, resident, exported, out_sems).wait()

    @pl.when(tl == 0)
    def _():
        cext[...] = jnp.zeros(cext.shape, F32)
        mcar[...] = jnp.zeros(mcar.shape, F32)

    def mlstm_pieces(g):
        rs = slice(g * GROUP, (g + 1) * GROUP)

        def load_z(which, h):
            c0 = which * D_MLSTM + h * HEAD_DIM
            return zbuf[rs, c0:c0 + HEAD_DIM]

        def store_h(h, val):
            h_ref[0, rs, h * HEAD_DIM:(h + 1) * HEAD_DIM] = val.astype(BF16)

        return _mlstm_group(gbuf[rs, 0:LANES], gbuf[rs, LANES:D_GATE], load_z, store_h, mg_ref, cext, mcar)

    def mixer_pieces():
        for g in range(tt // GROUP):
            yield from mlstm_pieces(g)

    pieces = mixer_pieces()
    costs = ([10] + [5] * N_HEADS) * (tt // GROUP)
    glu_in = []

    def store_z(c0, zc):
        if c0 < Q_OFF:
            glu_in.append(zc)
            if len(glu_in) == 2:
                u_ref[0] = glu_in[0] * jax.nn.sigmoid(glu_in[1])
        else:
            if c0 == Q_OFF:
                _run(pieces)
            zbuf[:, c0 - Q_OFF:c0 - Q_OFF + D_CONV] = zc

    seq_id = jnp.minimum(s, steps - 1) // nt
    x1, gates = _ffn1_inproj(x_ref[...], ada_ref[pl.ds(seq_id, 1), :], g1_ref, wgu_ref, wd_ref, g2_ref,
                             win_ref, bin_ref, wg_ref, bg_ref, hid_ref, store_z,
                             _piece_feeder(pieces, costs, [1.0] * N_FF_CHUNKS))
    x1_ref[...] = x1
    gbuf[...] = gates

    @pl.when(tl == nt - 1)
    def _():
        for h in range(N_HEADS):
            c_ref[0, h] = cext[h, :, :HEAD_DIM]
            n_ref[0, h:h + 1, :] = cext[h, :, HEAD_DIM:].T[0:1, :]
        m_ref[0] = mcar[...]


def _front(x, ada, p, *, tt):
    bsz, seq, _ = x.shape
    nt = seq // tt
    steps = bsz * nt
    cur = lambda s: jnp.minimum(s, steps - 1)
    lag = lambda s: jnp.maximum(s - 1, 0)
    big = [p["wgu1"], p["wd1"], p["win"]]
    big_shapes = [(D_MODEL, 2 * D_FF), (D_FF, D_MODEL), (D_MODEL, D_Z), (D_MODEL, D_GATE)]
    small = [p["g1"], p["g2"], p["b_main"], p["b_gate"], p["mn_g"]]
    any_spec = pl.BlockSpec(memory_space=pl.ANY)
    return pl.pallas_call(
        functools.partial(_front_kernel, tt=tt, nt=nt, steps=steps),
        grid=(steps + 1,),
        in_specs=[
            pl.BlockSpec((1, tt, D_MODEL), lambda s: (cur(s) // nt, cur(s) % nt, 0)),
            _const_spec(ada.shape),
        ] + [any_spec] * len(big) + [_const_spec(w.shape) for w in small],
        out_specs=[
            pl.BlockSpec((1, tt, D_MODEL), lambda s: (cur(s) // nt, cur(s) % nt, 0)),
            pl.BlockSpec((1, tt, D_CONV), lambda s: (cur(s) // nt, cur(s) % nt, 0)),
            pl.BlockSpec((1, tt, D_MLSTM), lambda s: (lag(s) // nt, lag(s) % nt, 0)),
            pl.BlockSpec((1, N_HEADS, HEAD_DIM, HEAD_DIM), lambda s: (lag(s) // nt, 0, 0, 0)),
            pl.BlockSpec((1, N_HEADS, HEAD_DIM), lambda s: (lag(s) // nt, 0, 0)),
            pl.BlockSpec((1, 1, LANES), lambda s: (lag(s) // nt, 0, 0)),
        ] + [any_spec] * len(big_shapes),
        out_shape=[
            jax.ShapeDtypeStruct(x.shape, F32),
            jax.ShapeDtypeStruct((bsz, seq, D_CONV), F32),
            jax.ShapeDtypeStruct((bsz, seq, D_MLSTM), BF16),
            jax.ShapeDtypeStruct((bsz, N_HEADS, HEAD_DIM, HEAD_DIM), F32),
            jax.ShapeDtypeStruct((bsz, N_HEADS, HEAD_DIM), F32),
            jax.ShapeDtypeStruct((bsz, 1, LANES), F32),
        ] + [jax.ShapeDtypeStruct(shape, BF16) for shape in big_shapes],
        scratch_shapes=[
            pltpu.VMEM((tt, D_FF), BF16),
            pltpu.VMEM((tt, 4 * D_MLSTM), F32),
            pltpu.VMEM((tt, D_GATE), F32),
            pltpu.VMEM((N_HEADS, HEAD_DIM, 2 * HEAD_DIM), F32),
            pltpu.VMEM((1, LANES), F32),
        ] + [pltpu.VMEM(shape, BF16) for shape in big_shapes] + _cast_scratch(len(big_shapes)),
        compiler_params=_params(1),
        name="front",
    )(x, ada, *big, *small)


def _back_kernel(x_ref, ada_ref, u_ref, h_ref, wgu_hbm, wd_hbm, wout_hbm, g3_ref, gf_ref,
                 cw_ref, cb_ref, lg_ref, lb_ref,
                 o_ref, cs_ref, wgu_out, wd_out, wout_out,
                 hid_ref, ubuf, ush, ycur, ynext, wgu_ref, wd_ref, wout_ref,
                 stage_wide, stage_narrow, sems_wide, sems_narrow, out_sems, *, tt, nt, rt, steps):
    s = pl.program_id(0)
    tc = lax.rem(jnp.minimum(s, steps - 1), nt)
    resident = (wgu_ref, wd_ref, wout_ref)
    exported = (wgu_out, wd_out, wout_out)

    @pl.when(s == 0)
    def _():
        _weights_to_bf16(wgu_hbm, wd_hbm, wout_hbm, wgu_ref, wd_ref, wout_ref, False,
                         stage_wide, stage_narrow, sems_wide, sems_narrow)
        for k in range(len(resident)):
            _weight_export(k, resident, exported, out_sems).start()
        ycur[...] = jnp.zeros(ycur.shape, BF16)

    @pl.when(s == steps)
    def _():
        for k in range(len(resident)):
            _weight_export(k, resident, exported, out_sems).wait()

    @pl.when(tc == 0)
    def _():
        ubuf[:, 0:TAIL_PAD, :] = jnp.zeros((1, TAIL_PAD, D_CONV), F32)

    def store_y(b0, r0, val):
        ynext[r0:r0 + rt, :] = val[0].astype(BF16)

    pieces = _conv_group(u_ref[...], ubuf, ush, cw_ref, cb_ref, lg_ref, lb_ref, store_y, lambda tail: None,
                         bt=1, rt=rt)
    costs = [3] + [10] * (SUBLANES - 1) + [3 * rt // SUBLANES] * (tt // rt)

    x = x_ref[...]
    ada = ada_ref[pl.ds(jnp.maximum(s - 1, 0) // nt, 1), :]
    between = _piece_feeder(pieces, costs, [1.0] * (1 + N_FF_CHUNKS))
    mix = _dot(ycur[...], wout_ref[:D_CONV, :]) + _dot(h_ref[0], wout_ref[D_CONV:, :])
    x2 = x + _ada_row(ada, 5) * _tie(mix, between()).reshape(1, tt, D_MODEL)
    xm = _rms(x2, g3_ref[...]) * (1.0 + _ada_row(ada, 7)) + _ada_row(ada, 6)
    ff = _swiglu(xm.reshape(tt, D_MODEL).astype(BF16), wgu_ref, wd_ref, hid_ref, between)
    _run(pieces)
    x3 = x2 + 0.5 * _ada_row(ada, 8) * ff.reshape(1, tt, D_MODEL)
    o_ref[...] = _rms(x3, gf_ref[...])
    ycur[...] = ynext[...]

    @pl.when(jnp.logical_and(tc == nt - 1, s < steps))
    def _():
        cs_ref[...] = ubuf[:, TAIL_LO:TAIL_PAD, :]


def _back(x1, ada, u, h, p, *, tt, rt):
    bsz, seq, _ = x1.shape
    nt = seq // tt
    steps = bsz * nt
    cur = lambda s: jnp.minimum(s, steps - 1)
    lag = lambda s: jnp.maximum(s - 1, 0)
    big = [p["wgu2"], p["wd2"], p["wout"]]
    big_shapes = [(D_MODEL, 2 * D_FF), (D_FF, D_MODEL), (D_MODEL, D_MODEL)]
    small = [p["g3"], p["gf"], p["conv_w"], p["conv_b"], p["cn_g"], p["cn_b"]]
    any_spec = pl.BlockSpec(memory_space=pl.ANY)
    return pl.pallas_call(
        functools.partial(_back_kernel, tt=tt, nt=nt, rt=rt, steps=steps),
        grid=(steps + 1,),
        in_specs=[
            pl.BlockSpec((1, tt, D_MODEL), lambda s: (lag(s) // nt, lag(s) % nt, 0)),
            _const_spec(ada.shape),
            pl.BlockSpec((1, tt, D_CONV), lambda s: (cur(s) // nt, cur(s) % nt, 0)),
            pl.BlockSpec((1, tt, D_MLSTM), lambda s: (lag(s) // nt, lag(s) % nt, 0)),
        ] + [any_spec] * len(big) + [_const_spec(w.shape) for w in small],
        out_specs=[
            pl.BlockSpec((1, tt, D_MODEL), lambda s: (lag(s) // nt, lag(s) % nt, 0)),
            pl.BlockSpec((1, CONV_TAIL, D_CONV), lambda s: (cur(s) // nt, 0, 0)),
        ] + [any_spec] * len(big_shapes),
        out_shape=[
            jax.ShapeDtypeStruct(x1.shape, F32),
            jax.ShapeDtypeStruct((bsz, CONV_TAIL, D_CONV), F32),
        ] + [jax.ShapeDtypeStruct(shape, BF16) for shape in big_shapes],
        scratch_shapes=[
            pltpu.VMEM((tt, D_FF), BF16),
            pltpu.VMEM((1, TAIL_PAD + tt, D_CONV), F32),
            pltpu.VMEM((SUBLANES - 1, 1, TAIL_PAD - SUBLANES + tt, D_CONV), F32),
            pltpu.VMEM((tt, D_CONV), BF16),
            pltpu.VMEM((tt, D_CONV), BF16),
        ] + [pltpu.VMEM(shape, BF16) for shape in big_shapes] + _cast_scratch(len(big_shapes)),
        compiler_params=_params(1),
        name="back",
    )(x1, ada, u, h, *big, *small)


def _ffn_a_kernel(x_ref, ada_ref, g1_ref, wgu_ref, wd_ref, g2_ref, win_ref, bin_ref, wg_ref, bg_ref,
                  x1_ref, z_ref, gate_ref, hid_ref):
    def store_z(c0, zc):
        z_ref[:, c0:c0 + D_CONV] = zc

    x1, gates = _ffn1_inproj(x_ref[...], ada_ref[...], g1_ref, wgu_ref, wd_ref, g2_ref,
                             win_ref, bin_ref, wg_ref, bg_ref, hid_ref, store_z)
    x1_ref[...] = x1
    gate_ref[...] = gates


def _ffn_a(x, ada, p, *, bb):
    bsz, seq, _ = x.shape
    n = bb * seq
    weights = [p["g1"], p["wgu1"], p["wd1"], p["g2"], p["win"], p["b_main"], p["wg"], p["b_gate"]]
    return pl.pallas_call(
        _ffn_a_kernel,
        grid=(bsz // bb,),
        in_specs=[
            pl.BlockSpec((bb, seq, D_MODEL), lambda b: (b, 0, 0)),
            pl.BlockSpec((bb, N_ADA * D_MODEL), lambda b: (b, 0)),
        ] + [_const_spec(w.shape) for w in weights],
        out_specs=[
            pl.BlockSpec((bb, seq, D_MODEL), lambda b: (b, 0, 0)),
            pl.BlockSpec((n, D_Z), lambda b: (b, 0)),
            pl.BlockSpec((n, D_GATE), lambda b: (b, 0)),
        ],
        out_shape=[
            jax.ShapeDtypeStruct(x.shape, F32),
            jax.ShapeDtypeStruct((bsz * seq, D_Z), F32),
            jax.ShapeDtypeStruct((bsz * seq, D_GATE), F32),
        ],
        scratch_shapes=[pltpu.VMEM((n, D_FF), BF16)],
        compiler_params=_params(1),
        name="ffn_a",
    )(x, ada, *weights)


def _conv_kernel(a_ref, b_ref, cs_ref, w_ref, cb_ref, lg_ref, lb_ref, y_ref, so_ref, full):
    seq = a_ref.shape[1]
    full[0:CONV_TAIL] = cs_ref[...]
    for t in range(seq):
        full[CONV_TAIL + t] = a_ref[:, t, :] * jax.nn.sigmoid(b_ref[:, t, :])
    for t in range(seq):
        acc = jnp.broadcast_to(cb_ref[...], full.shape[1:])
        for j in range(CONV_WIDTH):
            acc = acc + w_ref[j:j + 1, :] * full[t + j]
        mu = jnp.mean(acc, axis=-1, keepdims=True)
        xc = acc - mu
        var = jnp.mean(xc * xc, axis=-1, keepdims=True)
        yn = xc * lax.rsqrt(var + EPS) * lg_ref[...] + lb_ref[...]
        y_ref[:, t, :] = yn * jax.nn.sigmoid(yn)
    so_ref[...] = full[seq:seq + CONV_TAIL]


def _conv(z3, state_t, p, *, bb):
    bsz, seq, _ = z3.shape
    weights = [p["conv_w"], p["conv_b"], p["cn_g"], p["cn_b"]]
    state_spec = pl.BlockSpec((CONV_TAIL, bb, D_CONV), lambda b: (0, b, 0))
    return pl.pallas_call(
        _conv_kernel,
        grid=(bsz // bb,),
        in_specs=[
            pl.BlockSpec((bb, seq, D_CONV), lambda b: (b, 0, 0)),
            pl.BlockSpec((bb, seq, D_CONV), lambda b: (b, 0, 1)),
            state_spec,
        ] + [_const_spec(w.shape) for w in weights],
        out_specs=[pl.BlockSpec((bb, seq, D_CONV), lambda b: (b, 0, 0)), state_spec],
        out_shape=[
            jax.ShapeDtypeStruct((bsz, seq, D_CONV), F32),
            jax.ShapeDtypeStruct((CONV_TAIL, bsz, D_CONV), F32),
        ],
        scratch_shapes=[pltpu.VMEM((CONV_TAIL + seq, bb, D_CONV), F32)],
        compiler_params=_params(1),
        name="conv",
    )(z3, z3, state_t, *weights)


def _mlstm_sample_kernel(q_ref, k_ref, v_ref, o_ref, gt_ref, mg_ref, c0_ref, n0_ref, m0_ref,
                         h_ref, c_ref, n_ref, m_ref, *, nseq, ls):
    row = lax.broadcasted_iota(jnp.int32, (GROUP, GROUP), 0)
    col = lax.broadcasted_iota(jnp.int32, (GROUP, GROUP), 1)
    shift = ls.bit_length() - 1
    seg_r = lax.shift_right_logical(row, shift)
    seg_c = lax.shift_right_logical(col, shift)
    causal = jnp.logical_and(col <= row, seg_r == seg_c)
    segtril = causal.astype(F32)
    lastsel = (col == seg_r * ls + (ls - 1)).astype(F32)

    gates = gt_ref[...].reshape(GROUP, D_GATE)
    gi = gates[:, :LANES]
    cum = _dot_exact(segtril, _log_sigmoid(gates[:, LANES:]))
    cumlast = _dot_exact(lastsel, cum)
    cum_t = cum.T
    gi_t = gi.T
    mprev = m0_ref[...].reshape(GROUP, LANES)
    qg = q_ref[...].reshape(GROUP, D_MLSTM)
    kg = k_ref[...].reshape(GROUP, D_MLSTM) * (HEAD_DIM ** -0.5)
    vg = v_ref[...].reshape(GROUP, D_MLSTM)
    og = o_ref[...].reshape(GROUP, D_MLSTM)

    m_all = jnp.zeros((GROUP, LANES), F32)
    lane = lax.broadcasted_iota(jnp.int32, (GROUP, LANES), 1)
    hs = []
    for h in range(N_HEADS):
        sl = slice(h * HEAD_DIM, (h + 1) * HEAD_DIM)
        cum_col = cum[:, h:h + 1]
        dmat = jnp.where(causal, cum_col - cum_t[h:h + 1, :] + gi_t[h:h + 1, :], -jnp.inf)
        inter = cum_col + mprev[:, h:h + 1]
        m_col = jnp.maximum(inter, jnp.max(dmat, axis=-1, keepdims=True))
        w_intra = jnp.exp(dmat - m_col)
        w_inter = jnp.exp(inter - m_col)
        qh = qg[:, sl]
        qb = qh.astype(BF16)
        s = lax.dot_general(qb, kg[:, sl].astype(BF16), NT_DIMS, preferred_element_type=F32)
        p = s * w_intra
        num = _dot(p.astype(BF16), vg[:, sl].astype(BF16))
        den = jnp.sum(p, axis=-1, keepdims=True)
        qc = jnp.zeros((GROUP, HEAD_DIM), F32)
        qn = jnp.zeros((GROUP, 1), F32)
        for sq in range(nseq):
            qc = jnp.where(seg_r == sq, _dot(qb, c0_ref[sq, h].astype(BF16)), qc)
            qn = jnp.where(seg_r[:, 0:1] == sq,
                           jnp.sum(qh * n0_ref[sq, h:h + 1, :], axis=-1, keepdims=True), qn)
        num = num + w_inter * qc
        den = den + w_inter * qn
        hh = num / jnp.maximum(jnp.abs(den), jnp.exp(-m_col))
        hh = hh * lax.rsqrt(jnp.mean(hh * hh, axis=-1, keepdims=True) + EPS) * mg_ref[:, sl]
        hs.append(jax.nn.sigmoid(og[:, sl]) * hh)
        m_all = jnp.where(lane == h, m_col, m_all)

    h_ref[...] = jnp.concatenate(hs, axis=-1).reshape(h_ref.shape)
    m_ref[...] = m_all.reshape(m_ref.shape)

    mnew = _dot_exact(lastsel, m_all)
    for h in range(N_HEADS):
        sl = slice(h * HEAD_DIM, (h + 1) * HEAD_DIM)
        last_col = cumlast[:, h:h + 1]
        mnew_col = mnew[:, h:h + 1]
        w_k = jnp.exp(last_col - cum[:, h:h + 1] + gi[:, h:h + 1] - mnew_col)
        decay = jnp.exp(last_col + mprev[:, h:h + 1] - mnew_col)
        kw = kg[:, sl] * w_k
        kwb = kw.astype(BF16)
        vh = vg[:, sl]
        for sq in range(nseq):
            vs = jnp.where(seg_r == sq, vh, 0.0)
            kws = jnp.where(seg_r == sq, kw, 0.0)
            dc = lax.dot_general(kwb, vs.astype(BF16), TN_DIMS, preferred_element_type=F32)
            dn = jnp.sum(kws, axis=0, keepdims=True)
            dec = decay[sq * ls:sq * ls + 1, :]
            c_ref[sq, h] = dec * c0_ref[sq, h] + dc
            n_ref[sq, h:h + 1, :] = dec * n0_ref[sq, h:h + 1, :] + dn


def _mlstm_sample(z3, gates3, mg, c0, n0, m_tok, *, bb):
    bsz, seq, _ = z3.shape
    assert bb * seq == GROUP
    zspec = lambda k: pl.BlockSpec((bb, seq, D_MLSTM), lambda b: (b, 0, k))
    c_spec = pl.BlockSpec((bb, N_HEADS, HEAD_DIM, HEAD_DIM), lambda b: (b, 0, 0, 0))
    n_spec = pl.BlockSpec((bb, N_HEADS, HEAD_DIM), lambda b: (b, 0, 0))
    m_spec = pl.BlockSpec((bb, seq, LANES), lambda b: (b, 0, 0))
    return pl.pallas_call(
        functools.partial(_mlstm_sample_kernel, nseq=bb, ls=seq),
        grid=(bsz // bb,),
        in_specs=[zspec(2), zspec(3), zspec(4), zspec(5),
                  pl.BlockSpec((bb, seq, D_GATE), lambda b: (b, 0, 0)),
                  _const_spec(mg.shape),
                  c_spec, n_spec, m_spec],
        out_specs=[pl.BlockSpec((bb, seq, D_MLSTM), lambda b: (b, 0, 0)), c_spec, n_spec, m_spec],
        out_shape=[
            jax.ShapeDtypeStruct((bsz, seq, D_MLSTM), F32),
            jax.ShapeDtypeStruct((bsz, N_HEADS, HEAD_DIM, HEAD_DIM), F32),
            jax.ShapeDtypeStruct((bsz, N_HEADS, HEAD_DIM), F32),
            jax.ShapeDtypeStruct((bsz, seq, LANES), F32),
        ],
        compiler_params=_params(1),
        name="mlstm_sample",
    )(z3, z3, z3, z3, gates3, mg, c0, n0, m_tok)


def _ffn_b_kernel(x_ref, ada_ref, y_ref, h_ref, wout_ref, g3_ref, wgu_ref, wd_ref, gf_ref, o_ref, hid_ref):
    x = x_ref[...]
    bb, tt, _ = x.shape
    n = bb * tt
    ada = ada_ref[...]
    mix = (_dot(y_ref[...].reshape(n, D_CONV).astype(BF16), wout_ref[:D_CONV, :])
           + _dot(h_ref[...].reshape(n, D_MLSTM).astype(BF16), wout_ref[D_CONV:, :]))
    x2 = x + _ada_row(ada, 5) * mix.reshape(bb, tt, D_MODEL)
    xm = _rms(x2, g3_ref[...]) * (1.0 + _ada_row(ada, 7)) + _ada_row(ada, 6)
    ff = _swiglu(xm.reshape(n, D_MODEL).astype(BF16), wgu_ref, wd_ref, hid_ref)
    x3 = x2 + 0.5 * _ada_row(ada, 8) * ff.reshape(bb, tt, D_MODEL)
    o_ref[...] = _rms(x3, gf_ref[...])


def _ffn_b(x, ada, y, h, p, *, bb):
    bsz, seq, _ = x.shape
    tok = lambda w: pl.BlockSpec((bb, seq, w), lambda b: (b, 0, 0))
    weights = [p["wout"], p["g3"], p["wgu2"], p["wd2"], p["gf"]]
    return pl.pallas_call(
        _ffn_b_kernel,
        grid=(bsz // bb,),
        in_specs=[tok(D_MODEL), pl.BlockSpec((bb, N_ADA * D_MODEL), lambda b: (b, 0)), tok(D_CONV), tok(D_MLSTM)]
        + [_const_spec(w.shape) for w in weights],
        out_specs=tok(D_MODEL),
        out_shape=jax.ShapeDtypeStruct(x.shape, F32),
        scratch_shapes=[pltpu.VMEM((bb * seq, D_FF), BF16)],
        compiler_params=_params(1),
        name="ffn_b",
    )(x, ada, y, h, *weights)


def _prompt_trunk(x, ada, p):
    x1, u, h, c_new, n_new, m_row, wgu1, wd1, win, wg = _front(x, ada, p, tt=512)
    out, conv_new, wgu2, wd2, wout = _back(x1, ada, u, h, p, tt=512, rt=16)
    bf16_weights = dict(wgu1=wgu1, wd1=wd1, win=win, wg=wg, wgu2=wgu2, wd2=wd2, wout=wout)
    return (out, conv_new[None], c_new[None], n_new[None], m_row[None, :, 0, :N_HEADS]), bf16_weights


def _sample_trunk(x, ada, states, p):
    bsz, seq, _ = x.shape
    conv_state, c0, n0, m0 = states
    x1, z, gates = _ffn_a(x, ada, p, bb=32)
    z3 = z.reshape(bsz, seq, D_Z)
    gates3 = gates.reshape(bsz, seq, D_GATE)
    y, conv_new_t = _conv(z3, jnp.transpose(conv_state, (1, 0, 2)), p, bb=32)
    conv_new = jnp.transpose(conv_new_t, (1, 0, 2))
    m_tok = jnp.broadcast_to(
        jnp.pad(m0, ((0, 0), (0, LANES - N_HEADS)))[:, None, :], (bsz, seq, LANES))
    h, c_new, n_new, m_tok_new = _mlstm_sample(z3, gates3, p["mn_g"], c0, n0, m_tok, bb=GROUP // seq)
    out = _ffn_b(x1, ada, y, h, p, bb=64)
    return out, conv_new[None], c_new[None], n_new[None], m_tok_new[None, :, seq - 1, :N_HEADS]


def _gate_cols(a):
    pad = ((0, 0), (0, LANES - N_HEADS))
    return jnp.concatenate([jnp.pad(a[:, :N_HEADS], pad), jnp.pad(a[:, N_HEADS:], pad)], axis=1)


def kernel(x_prompt, x_sample, c_prompt, c_sample, state_conv, state_C, state_n, state_m, w_ada, b_ada, norm_ffn1, ffn1_w_gu, ffn1_w_down, norm_mix, w_in, b_in, conv_w, conv_b, conv_norm_g, conv_norm_b, mlstm_norm_g, w_out, norm_ffn2, ffn2_w_gu, ffn2_w_down, norm_final):
    assert w_ada.shape[0] == 1, "single layer"
    w_in_t = jnp.transpose(w_in[0])
    p = {
        "g1": norm_ffn1, "g2": norm_mix, "g3": norm_ffn2, "gf": norm_final[None],
        "wgu1": ffn1_w_gu[0], "wd1": ffn1_w_down[0], "wgu2": ffn2_w_gu[0], "wd2": ffn2_w_down[0],
        "win": w_in_t, "wout": w_out[0],
        "b_main": b_in,
        "b_gate": _gate_cols(b_in[:, D_Z:]),
        "conv_w": jnp.pad(conv_w[0], ((0, TAIL_PAD - CONV_WIDTH), (0, 0))),
        "conv_b": conv_b, "cn_g": conv_norm_g, "cn_b": conv_norm_b, "mn_g": mlstm_norm_g,
    }
    bs = x_sample.shape[0]
    ada = _ada(jnp.concatenate([c_sample, c_prompt], axis=0), w_ada[0], b_ada)
    (yp, conv_p, c_p, n_p, m_p), bf16_weights = _prompt_trunk(x_prompt, ada[bs:], p)
    ys, conv_s, c_s, n_s, m_s = _sample_trunk(
        x_sample, ada, (state_conv[0], state_C[0], state_n[0], state_m[0]), {**p, **bf16_weights})
    return (yp, ys, conv_p, c_p, n_p, m_p, conv_s, c_s, n_s, m_s)
```

```python
import functools

import jax
import jax.numpy as jnp
from jax import lax
from jax.experimental import pallas as pl
from jax.experimental.pallas import tpu as pltpu

F32 = jnp.float32
BF16 = jnp.bfloat16

D_MODEL = 1024
D_CONV = 512
D_MLSTM = 512
N_HEADS = 4
HEAD_DIM = 128
CONV_WIDTH = 31
CONV_TAIL = CONV_WIDTH - 1
D_FF = 2816
N_ADA = 9
EPS = 1e-6
D_Z = 2 * D_CONV + 4 * D_MLSTM
D_IN = D_Z + 2 * N_HEADS
Q_OFF = 2 * D_CONV
LANES = 128
SUBLANES = 8
D_GATE = 2 * LANES
FF_CHUNK = 256
N_FF_CHUNKS = D_FF // FF_CHUNK
GROUP = 128
TAIL_PAD = 32
TAIL_LO = TAIL_PAD - CONV_TAIL
TIE_PARTS = 2
FRONT_TIE_WEIGHTS = [0.0, 0.5] + [1.0] * (N_FF_CHUNKS - 2)
CAST_SLOTS = 3
CAST_ROWS_WIDE = 64
CAST_ROWS_NARROW = 128
VMEM_LIMIT = 56 * 1024 * 1024
NT_DIMS = (((1,), (1,)), ((), ()))
TN_DIMS = (((0,), (0,)), ((), ()))


def _dot(a, b):
    return jnp.dot(a, b, preferred_element_type=F32)


def _dot_exact(a, b):
    return jnp.dot(a, b, preferred_element_type=F32, precision=lax.Precision.HIGHEST)


def _rms(x, g):
    ms = jnp.mean(x * x, axis=-1, keepdims=True)
    return x * lax.rsqrt(ms + EPS) * g


def _ada_row(ada, k):
    return ada[:, k * D_MODEL:(k + 1) * D_MODEL][:, None, :]


def _log_sigmoid(x):
    return jnp.minimum(x, 0.0) - jnp.log(1.0 + jnp.exp(-jnp.abs(x)))


def _const_spec(shape):
    nd = len(shape)
    return pl.BlockSpec(shape, lambda *_: (0,) * nd, pipeline_mode=pl.Buffered(1))


def _params(n_grid):
    return pltpu.CompilerParams(dimension_semantics=("arbitrary",) * n_grid, vmem_limit_bytes=VMEM_LIMIT)


def _ada_kernel(c_ref, w_ref, b_ref, o_ref):
    c = c_ref[...]
    s = (c * jax.nn.sigmoid(c)).astype(BF16)
    o_ref[...] = _dot(s, w_ref[...].astype(BF16)) + b_ref[...]


def _ada(c_all, w_ada, b_ada):
    n = c_all.shape[0]
    tile = D_MODEL
    return pl.pallas_call(
        _ada_kernel,
        grid=(N_ADA,),
        in_specs=[
            pl.BlockSpec((n, D_MODEL), lambda j: (0, 0)),
            pl.BlockSpec((D_MODEL, tile), lambda j: (0, j)),
            pl.BlockSpec((1, tile), lambda j: (0, j)),
        ],
        out_specs=pl.BlockSpec((n, tile), lambda j: (0, j)),
        out_shape=jax.ShapeDtypeStruct((n, N_ADA * D_MODEL), F32),
        compiler_params=_params(1),
        name="ada",
    )(c_all, w_ada, b_ada)


def _tie(value, zeros):
    rb = value.shape[0] // TIE_PARTS
    packing = 4 // value.dtype.itemsize
    reps = (rb // (SUBLANES * packing), value.shape[1] // LANES)

    def add(block, zero):
        if zero is None:
            return block
        zero = jnp.concatenate([zero] * packing, axis=0).astype(value.dtype)
        return block + jnp.tile(zero, reps)

    return jnp.concatenate([add(value[k * rb:(k + 1) * rb], z) for k, z in enumerate(zeros)], axis=0)


def _swiglu(xm, wgu_ref, wd_ref, hid_ref, between=None):
    for j in range(N_FF_CHUNKS):
        lo = j * FF_CHUNK
        zeros = None if between is None else between()
        gate = _dot(xm, wgu_ref[:, lo:lo + FF_CHUNK])
        up = _dot(xm, wgu_ref[:, D_FF + lo:D_FF + lo + FF_CHUNK])
        hidden = gate * jax.nn.sigmoid(gate) * up
        if zeros is not None:
            hidden = _tie(hidden, zeros)
        hid_ref[:, lo:lo + FF_CHUNK] = hidden.astype(BF16)
    return _dot(hid_ref[...], wd_ref[...])


def _ffn1_inproj(x, ada, g1_ref, wgu_ref, wd_ref, g2_ref, win_ref, bin_ref, wg_ref, bg_ref, hid_ref,
                 store_z, between=None):
    bb, tt, _ = x.shape
    n = bb * tt
    xm = _rms(x, g1_ref[...]) * (1.0 + _ada_row(ada, 1)) + _ada_row(ada, 0)
    ff = _swiglu(xm.reshape(n, D_MODEL).astype(BF16), wgu_ref, wd_ref, hid_ref, between)
    x1 = x + 0.5 * _ada_row(ada, 2) * ff.reshape(bb, tt, D_MODEL)
    hm = _rms(x1, g2_ref[...]) * (1.0 + _ada_row(ada, 4)) + _ada_row(ada, 3)
    hm = hm.reshape(n, D_MODEL).astype(BF16)
    for c0 in range(0, D_Z, D_CONV):
        store_z(c0, _dot(hm, win_ref[:, c0:c0 + D_CONV]) + bin_ref[:, c0:c0 + D_CONV])
    return x1, _dot(hm, wg_ref[...]) + bg_ref[...]


def _run(steps):
    for _ in steps:
        pass


def _all_bits(*arrays):
    acc = None
    for a in arrays:
        bits = lax.bitcast_convert_type(a.reshape(-1, a.shape[-1]), jnp.uint32)
        for r in range(0, bits.shape[0], SUBLANES):
            for c in range(0, bits.shape[1], LANES):
                v = bits[r:r + SUBLANES, c:c + LANES]
                acc = v if acc is None else acc | v
    return acc


def _conv_group(u, ubuf, ush, w_ref, cb_ref, lg_ref, lb_ref, store_y, store_tail, *, bt, rt):
    bb, tt, _ = u.shape
    sh_rows = TAIL_PAD - SUBLANES + tt
    ubuf[:, TAIL_PAD:TAIL_PAD + tt, :] = u
    yield None
    for r in range(1, SUBLANES):
        ush[r - 1] = ubuf[:, r:r + sh_rows, :]
        yield None
    for b0 in range(0, bb, bt):
        for r0 in range(0, tt, rt):
            acc = jnp.broadcast_to(cb_ref[...].reshape(1, 1, D_CONV), (bt, rt, D_CONV))
            for j in range(CONV_WIDTH):
                q, r = divmod(TAIL_LO + j, SUBLANES)
                p0 = r0 + q * SUBLANES
                if r == 0:
                    win = ubuf[b0:b0 + bt, p0:p0 + rt, :]
                else:
                    win = ush[r - 1, b0:b0 + bt, p0:p0 + rt, :]
                acc = acc + w_ref[j:j + 1, :].reshape(1, 1, D_CONV) * win
            mu = jnp.mean(acc, axis=-1, keepdims=True)
            xc = acc - mu
            var = jnp.mean(xc * xc, axis=-1, keepdims=True)
            yn = xc * lax.rsqrt(var + EPS) * lg_ref[...].reshape(1, 1, D_CONV) + lb_ref[...].reshape(1, 1, D_CONV)
            out = yn * jax.nn.sigmoid(yn)
            store_y(b0, r0, out)
            yield _all_bits(out)
    tail = ubuf[:, TAIL_LO + tt:TAIL_PAD + tt, :]
    ubuf[:, TAIL_LO:TAIL_PAD, :] = tail
    store_tail(tail)


def _scan_rows(x, op, ident):
    row = lax.broadcasted_iota(jnp.int32, x.shape, 0)
    s = 1
    while s < x.shape[0]:
        x = op(x, jnp.where(row >= s, pltpu.roll(x, s, axis=0), ident))
        s *= 2
    return x


def _mlstm_group(gi, gf, load_z, store_h, mg_ref, cext, mcar):
    row = lax.broadcasted_iota(jnp.int32, (GROUP, GROUP), 0)
    col = lax.broadcasted_iota(jnp.int32, (GROUP, GROUP), 1)
    causal = col <= row
    ones_b = jnp.ones((GROUP, HEAD_DIM), BF16)
    cum = _scan_rows(_log_sigmoid(gf), jnp.add, 0.0)
    gv = gi - cum
    mx = _scan_rows(gv, jnp.maximum, -jnp.inf)
    mprev = mcar[...]
    mm = jnp.maximum(mprev, mx)
    mcol = cum + mm
    mm_last = mm[GROUP - 1:GROUP, :]
    w_k = jnp.exp(gv - mm_last)
    decay = jnp.exp(mprev - mm_last)
    mcar[...] = mcol[GROUP - 1:GROUP, :]
    gv_t = gv.T
    yield _all_bits(gv_t[0:SUBLANES, :])
    for h in range(N_HEADS):
        sl = slice(h * HEAD_DIM, (h + 1) * HEAD_DIM)
        mm_b = jnp.broadcast_to(mm[:, h:h + 1], (GROUP, GROUP))
        mcol_b = jnp.broadcast_to(mcol[:, h:h + 1], (GROUP, GROUP))
        w_intra = jnp.exp(jnp.where(causal, gv_t[h:h + 1, :] - mm_b, -jnp.inf))
        w_inter = jnp.exp(mprev[:, h:h + 1] - mm_b)
        qb = load_z(0, h).astype(BF16)
        kh = load_z(1, h) * (HEAD_DIM ** -0.5)
        v_ext = jnp.concatenate([load_z(2, h).astype(BF16), ones_b], axis=1)
        s = lax.dot_general(qb, kh.astype(BF16), NT_DIMS, preferred_element_type=F32)
        pv = _dot((s * w_intra).astype(BF16), v_ext)
        cprev = cext[h]
        qc = _dot(qb, cprev.astype(BF16))
        num = pv[:, :HEAD_DIM] + w_inter * qc[:, :HEAD_DIM]
        den = pv[:, HEAD_DIM:] + w_inter * qc[:, HEAD_DIM:]
        hh = num / jnp.maximum(jnp.abs(den), jnp.exp(-mcol_b))
        hh = hh * lax.rsqrt(jnp.mean(hh * hh, axis=-1, keepdims=True) + EPS) * mg_ref[:, sl]
        out = jax.nn.sigmoid(load_z(3, h)) * hh
        store_h(h, out)
        kw = (kh * w_k[:, h:h + 1]).astype(BF16)
        dc = lax.dot_general(kw, v_ext, TN_DIMS, preferred_element_type=F32)
        cnew = decay[:, h:h + 1] * cprev + dc
        cext[h] = cnew
        yield _all_bits(out, cnew)


def _piece_feeder(pieces, costs, call_weights):
    point_weights = [w / TIE_PARTS for w in call_weights for _ in range(TIE_PARTS)]
    done = [0, 0]

    def one_point():
        done[0] += 1
        target = sum(costs) * sum(point_weights[:done[0]]) / sum(point_weights)
        bits = None
        while done[1] < len(costs) and sum(costs[:done[1]]) + costs[done[1]] / 2 <= target:
            piece_bits = next(pieces)
            if piece_bits is not None:
                bits = piece_bits if bits is None else bits | piece_bits
            done[1] += 1
        if bits is None:
            return None
        zero_bits = lax.shift_right_logical(lax.shift_right_logical(bits, jnp.uint32(16)), jnp.uint32(16))
        return lax.bitcast_convert_type(zero_bits, F32)

    return lambda: [one_point() for _ in range(TIE_PARTS)]


def _weight_export(k, vmem_refs, out_refs, sems):
    return pltpu.make_async_copy(vmem_refs[k], out_refs[k], sems.at[k])


def _cast_scratch(n_exports):
    return [
        pltpu.VMEM((CAST_SLOTS, CAST_ROWS_WIDE, 2 * D_FF), F32),
        pltpu.VMEM((CAST_SLOTS, CAST_ROWS_NARROW, D_MODEL), F32),
        pltpu.SemaphoreType.DMA((CAST_SLOTS,)),
        pltpu.SemaphoreType.DMA((CAST_SLOTS,)),
        pltpu.SemaphoreType.DMA((n_exports,)),
    ]


def _stream_rows(src, n_chunks, stage, sems, consume):
    depth, chunk_rows, cols = stage.shape

    def load(i):
        return pltpu.make_async_copy(src.at[pl.ds(i * chunk_rows, chunk_rows), pl.ds(0, cols)],
                                     stage.at[i % depth], sems.at[i % depth])

    for i in range(min(depth - 1, n_chunks)):
        load(i).start()
    for i in range(n_chunks):
        if i + depth - 1 < n_chunks:
            load(i + depth - 1).start()
        load(i).wait()
        consume(i, stage[i % depth])


def _weights_to_bf16(wgu_hbm, wd_hbm, third_hbm, wgu_ref, wd_ref, third_ref, third_is_transposed,
                     stage_wide, stage_narrow, sems_wide, sems_narrow):
    wide_rows, narrow_rows = stage_wide.shape[1], stage_narrow.shape[1]

    def put_wgu(i, chunk):
        wgu_ref[i * wide_rows:(i + 1) * wide_rows, :] = chunk.astype(BF16)

    def put_wd(i, chunk):
        wd_ref[i * narrow_rows:(i + 1) * narrow_rows, :] = chunk.astype(BF16)

    def put_third(i, chunk):
        if third_is_transposed:
            third_ref[:, i * narrow_rows:(i + 1) * narrow_rows] = chunk.T.astype(BF16)
        else:
            third_ref[i * narrow_rows:(i + 1) * narrow_rows, :] = chunk.astype(BF16)

    third_rows = third_ref.shape[1] if third_is_transposed else third_ref.shape[0]
    _stream_rows(wgu_hbm, wgu_ref.shape[0] // wide_rows, stage_wide, sems_wide, put_wgu)
    _stream_rows(wd_hbm, wd_ref.shape[0] // narrow_rows, stage_narrow, sems_narrow, put_wd)
    _stream_rows(third_hbm, third_rows // narrow_rows, stage_narrow, sems_narrow, put_third)


def _gate_weights(win_hbm, wg_ref, stage_narrow, sems_narrow):
    rows = stage_narrow.shape[1]
    tail = pltpu.make_async_copy(win_hbm.at[pl.ds(D_IN - rows, rows), :], stage_narrow.at[0], sems_narrow.at[0])
    tail.start()
    tail.wait()
    cols = stage_narrow[0].T
    lane = lax.broadcasted_iota(jnp.int32, cols.shape, 1)
    i_part = jnp.where(lane < N_HEADS, pltpu.roll(cols, 2 * N_HEADS, axis=1), 0.0)
    f_part = jnp.where(lane < N_HEADS, pltpu.roll(cols, N_HEADS, axis=1), 0.0)
    wg_ref[...] = jnp.concatenate([i_part, f_part], axis=1).astype(BF16)


def _front_kernel(x_ref, ada_ref, wgu_hbm, wd_hbm, win_hbm, g1_ref, g2_ref, bin_ref, bg_ref, mg_ref,
                  x1_ref, u_ref, h_ref, c_ref, n_ref, m_ref, wgu_out, wd_out, win_out, wg_out,
                  hid_ref, zbuf, gbuf, cext, mcar, wgu_ref, wd_ref, win_ref, wg_ref,
                  stage_wide, stage_narrow, sems_wide, sems_narrow, out_sems, *, tt, nt, steps):
    s = pl.program_id(0)
    tl = lax.rem(jnp.maximum(s - 1, 0), nt)
    resident = (wgu_ref, wd_ref, win_ref, wg_ref)
    exported = (wgu_out, wd_out, win_out, wg_out)

    @pl.when(s == 0)
    def _():
        _weights_to_bf16(wgu_hbm, wd_hbm, win_hbm, wgu_ref, wd_ref, win_ref, True,
                         stage_wide, stage_narrow, sems_wide, sems_narrow)
        _gate_weights(win_hbm, wg_ref, stage_narrow, sems_narrow)
        for k in range(len(resident)):
            _weight_export(k, resident, exported, out_sems).start()
        zbuf[...] = jnp.zeros(zbuf.shape, F32)
        gbuf[...] = jnp.zeros(gbuf.shape, F32)

    @pl.when(s == steps)
    def _():
        for k in range(len(resident)):
            _weight_export(k, resident, exported, out_sems).wait()

    @pl.when(tl == 0)
    def _():
        cext[...] = jnp.zeros(cext.shape, F32)
        mcar[...] = jnp.zeros(mcar.shape, F32)

    def mlstm_pieces(g):
        rs = slice(g * GROUP, (g + 1) * GROUP)

        def load_z(which, h):
            c0 = which * D_MLSTM + h * HEAD_DIM
            return zbuf[rs, c0:c0 + HEAD_DIM]

        def store_h(h, val):
            h_ref[0, rs, h * HEAD_DIM:(h + 1) * HEAD_DIM] = val.astype(BF16)

        return _mlstm_group(gbuf[rs, 0:LANES], gbuf[rs, LANES:D_GATE], load_z, store_h, mg_ref, cext, mcar)

    def mixer_pieces():
        for g in range(tt // GROUP):
            yield from mlstm_pieces(g)

    pieces = mixer_pieces()
    costs = ([10] + [5] * N_HEADS) * (tt // GROUP)
    glu_in = []

    def store_z(c0, zc):
        if c0 < Q_OFF:
            glu_in.append(zc)
            if len(glu_in) == 2:
                u_ref[0] = glu_in[0] * jax.nn.sigmoid(glu_in[1])
        else:
            if c0 == Q_OFF:
                _run(pieces)
            zbuf[:, c0 - Q_OFF:c0 - Q_OFF + D_CONV] = zc

    seq_id = jnp.minimum(s, steps - 1) // nt
    x1, gates = _ffn1_inproj(x_ref[...], ada_ref[pl.ds(seq_id, 1), :], g1_ref, wgu_ref, wd_ref, g2_ref,
                             win_ref, bin_ref, wg_ref, bg_ref, hid_ref, store_z,
                             _piece_feeder(pieces, costs, FRONT_TIE_WEIGHTS))
    x1_ref[...] = x1
    gbuf[...] = gates

    @pl.when(tl == nt - 1)
    def _():
        for h in range(N_HEADS):
            c_ref[0, h] = cext[h, :, :HEAD_DIM]
            n_ref[0, h:h + 1, :] = cext[h, :, HEAD_DIM:].T[0:1, :]
        m_ref[0] = mcar[...]


def _front(x, ada, p, *, tt):
    bsz, seq, _ = x.shape
    nt = seq // tt
    steps = bsz * nt
    cur = lambda s: jnp.minimum(s, steps - 1)
    lag = lambda s: jnp.maximum(s - 1, 0)
    big = [p["wgu1"], p["wd1"], p["win"]]
    big_shapes = [(D_MODEL, 2 * D_FF), (D_FF, D_MODEL), (D_MODEL, D_Z), (D_MODEL, D_GATE)]
    small = [p["g1"], p["g2"], p["b_main"], p["b_gate"], p["mn_g"]]
    any_spec = pl.BlockSpec(memory_space=pl.ANY)
    return pl.pallas_call(
        functools.partial(_front_kernel, tt=tt, nt=nt, steps=steps),
        grid=(steps + 1,),
        in_specs=[
            pl.BlockSpec((1, tt, D_MODEL), lambda s: (cur(s) // nt, cur(s) % nt, 0)),
            _const_spec(ada.shape),
        ] + [any_spec] * len(big) + [_const_spec(w.shape) for w in small],
        out_specs=[
            pl.BlockSpec((1, tt, D_MODEL), lambda s: (cur(s) // nt, cur(s) % nt, 0)),
            pl.BlockSpec((1, tt, D_CONV), lambda s: (cur(s) // nt, cur(s) % nt, 0)),
            pl.BlockSpec((1, tt, D_MLSTM), lambda s: (lag(s) // nt, lag(s) % nt, 0)),
            pl.BlockSpec((1, N_HEADS, HEAD_DIM, HEAD_DIM), lambda s: (lag(s) // nt, 0, 0, 0)),
            pl.BlockSpec((1, N_HEADS, HEAD_DIM), lambda s: (lag(s) // nt, 0, 0)),
            pl.BlockSpec((1, 1, LANES), lambda s: (lag(s) // nt, 0, 0)),
        ] + [any_spec] * len(big_shapes),
        out_shape=[
            jax.ShapeDtypeStruct(x.shape, F32),
            jax.ShapeDtypeStruct((bsz, seq, D_CONV), F32),
            jax.ShapeDtypeStruct((bsz, seq, D_MLSTM), BF16),
            jax.ShapeDtypeStruct((bsz, N_HEADS, HEAD_DIM, HEAD_DIM), F32),
            jax.ShapeDtypeStruct((bsz, N_HEADS, HEAD_DIM), F32),
            jax.ShapeDtypeStruct((bsz, 1, LANES), F32),
        ] + [jax.ShapeDtypeStruct(shape, BF16) for shape in big_shapes],
        scratch_shapes=[
            pltpu.VMEM((tt, D_FF), BF16),
            pltpu.VMEM((tt, 4 * D_MLSTM), F32),
            pltpu.VMEM((tt, D_GATE), F32),
            pltpu.VMEM((N_HEADS, HEAD_DIM, 2 * HEAD_DIM), F32),
            pltpu.VMEM((1, LANES), F32),
        ] + [pltpu.VMEM(shape, BF16) for shape in big_shapes] + _cast_scratch(len(big_shapes)),
        compiler_params=_params(1),
        name="front",
    )(x, ada, *big, *small)


def _back_kernel(x_ref, ada_ref, u_ref, h_ref, wgu_hbm, wd_hbm, wout_hbm, g3_ref, gf_ref,
                 cw_ref, cb_ref, lg_ref, lb_ref,
                 o_ref, cs_ref, wgu_out, wd_out, wout_out,
                 hid_ref, ubuf, ush, ycur, ynext, wgu_ref, wd_ref, wout_ref,
                 stage_wide, stage_narrow, sems_wide, sems_narrow, out_sems, *, tt, nt, rt, steps):
    s = pl.program_id(0)
    tc = lax.rem(jnp.minimum(s, steps - 1), nt)
    resident = (wgu_ref, wd_ref, wout_ref)
    exported = (wgu_out, wd_out, wout_out)

    @pl.when(s == 0)
    def _():
        _weights_to_bf16(wgu_hbm, wd_hbm, wout_hbm, wgu_ref, wd_ref, wout_ref, False,
                         stage_wide, stage_narrow, sems_wide, sems_narrow)
        for k in range(len(resident)):
            _weight_export(k, resident, exported, out_sems).start()
        ycur[...] = jnp.zeros(ycur.shape, BF16)

    @pl.when(s == steps)
    def _():
        for k in range(len(resident)):
            _weight_export(k, resident, exported, out_sems).wait()

    @pl.when(tc == 0)
    def _():
        ubuf[:, 0:TAIL_PAD, :] = jnp.zeros((1, TAIL_PAD, D_CONV), F32)

    def store_y(b0, r0, val):
        ynext[r0:r0 + rt, :] = val[0].astype(BF16)

    pieces = _conv_group(u_ref[...], ubuf, ush, cw_ref, cb_ref, lg_ref, lb_ref, store_y, lambda tail: None,
                         bt=1, rt=rt)
    costs = [3] + [10] * (SUBLANES - 1) + [3 * rt // SUBLANES] * (tt // rt)

    x = x_ref[...]
    ada = ada_ref[pl.ds(jnp.maximum(s - 1, 0) // nt, 1), :]
    between = _piece_feeder(pieces, costs, [1.0] * (1 + N_FF_CHUNKS))
    mix = _dot(ycur[...], wout_ref[:D_CONV, :]) + _dot(h_ref[0], wout_ref[D_CONV:, :])
    x2 = x + _ada_row(ada, 5) * _tie(mix, between()).reshape(1, tt, D_MODEL)
    xm = _rms(x2, g3_ref[...]) * (1.0 + _ada_row(ada, 7)) + _ada_row(ada, 6)
    ff = _swiglu(xm.reshape(tt, D_MODEL).astype(BF16), wgu_ref, wd_ref, hid_ref, between)
    _run(pieces)
    x3 = x2 + 0.5 * _ada_row(ada, 8) * ff.reshape(1, tt, D_MODEL)
    o_ref[...] = _rms(x3, gf_ref[...])
    ycur[...] = ynext[...]

    @pl.when(jnp.logical_and(tc == nt - 1, s < steps))
    def _():
        cs_ref[...] = ubuf[:, TAIL_LO:TAIL_PAD, :]


def _back(x1, ada, u, h, p, *, tt, rt):
    bsz, seq, _ = x1.shape
    nt = seq // tt
    steps = bsz * nt
    cur = lambda s: jnp.minimum(s, steps - 1)
    lag = lambda s: jnp.maximum(s - 1, 0)
    big = [p["wgu2"], p["wd2"], p["wout"]]
    big_shapes = [(D_MODEL, 2 * D_FF), (D_FF, D_MODEL), (D_MODEL, D_MODEL)]
    small = [p["g3"], p["gf"], p["conv_w"], p["conv_b"], p["cn_g"], p["cn_b"]]
    any_spec = pl.BlockSpec(memory_space=pl.ANY)
    return pl.pallas_call(
        functools.partial(_back_kernel, tt=tt, nt=nt, rt=rt, steps=steps),
        grid=(steps + 1,),
        in_specs=[
            pl.BlockSpec((1, tt, D_MODEL), lambda s: (lag(s) // nt, lag(s) % nt, 0)),
            _const_spec(ada.shape),
            pl.BlockSpec((1, tt, D_CONV), lambda s: (cur(s) // nt, cur(s) % nt, 0)),
            pl.BlockSpec((1, tt, D_MLSTM), lambda s: (lag(s) // nt, lag(s) % nt, 0)),
        ] + [any_spec] * len(big) + [_const_spec(w.shape) for w in small],
        out_specs=[
            pl.BlockSpec((1, tt, D_MODEL), lambda s: (lag(s) // nt, lag(s) % nt, 0)),
            pl.BlockSpec((1, CONV_TAIL, D_CONV), lambda s: (cur(s) // nt, 0, 0)),
        ] + [any_spec] * len(big_shapes),
        out_shape=[
            jax.ShapeDtypeStruct(x1.shape, F32),
            jax.ShapeDtypeStruct((bsz, CONV_TAIL, D_CONV), F32),
        ] + [jax.ShapeDtypeStruct(shape, BF16) for shape in big_shapes],
        scratch_shapes=[
            pltpu.VMEM((tt, D_FF), BF16),
            pltpu.VMEM((1, TAIL_PAD + tt, D_CONV), F32),
            pltpu.VMEM((SUBLANES - 1, 1, TAIL_PAD - SUBLANES + tt, D_CONV), F32),
            pltpu.VMEM((tt, D_CONV), BF16),
            pltpu.VMEM((tt, D_CONV), BF16),
        ] + [pltpu.VMEM(shape, BF16) for shape in big_shapes] + _cast_scratch(len(big_shapes)),
        compiler_params=_params(1),
        name="back",
    )(x1, ada, u, h, *big, *small)


def _ffn_a_kernel(x_ref, ada_ref, g1_ref, wgu_ref, wd_ref, g2_ref, win_ref, bin_ref, wg_ref, bg_ref,
                  x1_ref, z_ref, gate_ref, hid_ref):
    def store_z(c0, zc):
        z_ref[:, c0:c0 + D_CONV] = zc

    x1, gates = _ffn1_inproj(x_ref[...], ada_ref[...], g1_ref, wgu_ref, wd_ref, g2_ref,
                             win_ref, bin_ref, wg_ref, bg_ref, hid_ref, store_z)
    x1_ref[...] = x1
    gate_ref[...] = gates


def _ffn_a(x, ada, p, *, bb):
    bsz, seq, _ = x.shape
    n = bb * seq
    weights = [p["g1"], p["wgu1"], p["wd1"], p["g2"], p["win"], p["b_main"], p["wg"], p["b_gate"]]
    return pl.pallas_call(
        _ffn_a_kernel,
        grid=(bsz // bb,),
        in_specs=[
            pl.BlockSpec((bb, seq, D_MODEL), lambda b: (b, 0, 0)),
            pl.BlockSpec((bb, N_ADA * D_MODEL), lambda b: (b, 0)),
        ] + [_const_spec(w.shape) for w in weights],
        out_specs=[
            pl.BlockSpec((bb, seq, D_MODEL), lambda b: (b, 0, 0)),
            pl.BlockSpec((n, D_Z), lambda b: (b, 0)),
            pl.BlockSpec((n, D_GATE), lambda b: (b, 0)),
        ],
        out_shape=[
            jax.ShapeDtypeStruct(x.shape, F32),
            jax.ShapeDtypeStruct((bsz * seq, D_Z), F32),
            jax.ShapeDtypeStruct((bsz * seq, D_GATE), F32),
        ],
        scratch_shapes=[pltpu.VMEM((n, D_FF), BF16)],
        compiler_params=_params(1),
        name="ffn_a",
    )(x, ada, *weights)


def _conv_kernel(a_ref, b_ref, cs_ref, w_ref, cb_ref, lg_ref, lb_ref, y_ref, so_ref, full):
    seq = a_ref.shape[1]
    full[0:CONV_TAIL] = cs_ref[...]
    for t in range(seq):
        full[CONV_TAIL + t] = a_ref[:, t, :] * jax.nn.sigmoid(b_ref[:, t, :])
    for t in range(seq):
        acc = jnp.broadcast_to(cb_ref[...], full.shape[1:])
        for j in range(CONV_WIDTH):
            acc = acc + w_ref[j:j + 1, :] * full[t + j]
        mu = jnp.mean(acc, axis=-1, keepdims=True)
        xc = acc - mu
        var = jnp.mean(xc * xc, axis=-1, keepdims=True)
        yn = xc * lax.rsqrt(var + EPS) * lg_ref[...] + lb_ref[...]
        y_ref[:, t, :] = yn * jax.nn.sigmoid(yn)
    so_ref[...] = full[seq:seq + CONV_TAIL]


def _conv(z3, state_t, p, *, bb):
    bsz, seq, _ = z3.shape
    weights = [p["conv_w"], p["conv_b"], p["cn_g"], p["cn_b"]]
    state_spec = pl.BlockSpec((CONV_TAIL, bb, D_CONV), lambda b: (0, b, 0))
    return pl.pallas_call(
        _conv_kernel,
        grid=(bsz // bb,),
        in_specs=[
            pl.BlockSpec((bb, seq, D_CONV), lambda b: (b, 0, 0)),
            pl.BlockSpec((bb, seq, D_CONV), lambda b: (b, 0, 1)),
            state_spec,
        ] + [_const_spec(w.shape) for w in weights],
        out_specs=[pl.BlockSpec((bb, seq, D_CONV), lambda b: (b, 0, 0)), state_spec],
        out_shape=[
            jax.ShapeDtypeStruct((bsz, seq, D_CONV), F32),
            jax.ShapeDtypeStruct((CONV_TAIL, bsz, D_CONV), F32),
        ],
        scratch_shapes=[pltpu.VMEM((CONV_TAIL + seq, bb, D_CONV), F32)],
        compiler_params=_params(1),
        name="conv",
    )(z3, z3, state_t, *weights)


def _mlstm_sample_kernel(q_ref, k_ref, v_ref, o_ref, gt_ref, mg_ref, c0_ref, n0_ref, m0_ref,
                         h_ref, c_ref, n_ref, m_ref, *, nseq, ls):
    row = lax.broadcasted_iota(jnp.int32, (GROUP, GROUP), 0)
    col = lax.broadcasted_iota(jnp.int32, (GROUP, GROUP), 1)
    shift = ls.bit_length() - 1
    seg_r = lax.shift_right_logical(row, shift)
    seg_c = lax.shift_right_logical(col, shift)
    causal = jnp.logical_and(col <= row, seg_r == seg_c)
    segtril = causal.astype(F32)
    lastsel = (col == seg_r * ls + (ls - 1)).astype(F32)

    gates = gt_ref[...].reshape(GROUP, D_GATE)
    gi = gates[:, :LANES]
    cum = _dot_exact(segtril, _log_sigmoid(gates[:, LANES:]))
    cumlast = _dot_exact(lastsel, cum)
    cum_t = cum.T
    gi_t = gi.T
    mprev = m0_ref[...].reshape(GROUP, LANES)
    qg = q_ref[...].reshape(GROUP, D_MLSTM)
    kg = k_ref[...].reshape(GROUP, D_MLSTM) * (HEAD_DIM ** -0.5)
    vg = v_ref[...].reshape(GROUP, D_MLSTM)
    og = o_ref[...].reshape(GROUP, D_MLSTM)

    m_all = jnp.zeros((GROUP, LANES), F32)
    lane = lax.broadcasted_iota(jnp.int32, (GROUP, LANES), 1)
    hs = []
    for h in range(N_HEADS):
        sl = slice(h * HEAD_DIM, (h + 1) * HEAD_DIM)
        cum_col = cum[:, h:h + 1]
        dmat = jnp.where(causal, cum_col - cum_t[h:h + 1, :] + gi_t[h:h + 1, :], -jnp.inf)
        inter = cum_col + mprev[:, h:h + 1]
        m_col = jnp.maximum(inter, jnp.max(dmat, axis=-1, keepdims=True))
        w_intra = jnp.exp(dmat - m_col)
        w_inter = jnp.exp(inter - m_col)
        qh = qg[:, sl]
        qb = qh.astype(BF16)
        s = lax.dot_general(qb, kg[:, sl].astype(BF16), NT_DIMS, preferred_element_type=F32)
        p = s * w_intra
        num = _dot(p.astype(BF16), vg[:, sl].astype(BF16))
        den = jnp.sum(p, axis=-1, keepdims=True)
        qc = jnp.zeros((GROUP, HEAD_DIM), F32)
        qn = jnp.zeros((GROUP, 1), F32)
        for sq in range(nseq):
            qc = jnp.where(seg_r == sq, _dot(qb, c0_ref[sq, h].astype(BF16)), qc)
            qn = jnp.where(seg_r[:, 0:1] == sq,
                           jnp.sum(qh * n0_ref[sq, h:h + 1, :], axis=-1, keepdims=True), qn)
        num = num + w_inter * qc
        den = den + w_inter * qn
        hh = num / jnp.maximum(jnp.abs(den), jnp.exp(-m_col))
        hh = hh * lax.rsqrt(jnp.mean(hh * hh, axis=-1, keepdims=True) + EPS) * mg_ref[:, sl]
        hs.append(jax.nn.sigmoid(og[:, sl]) * hh)
        m_all = jnp.where(lane == h, m_col, m_all)

    h_ref[...] = jnp.concatenate(hs, axis=-1).reshape(h_ref.shape)
    m_ref[...] = m_all.reshape(m_ref.shape)

    mnew = _dot_exact(lastsel, m_all)
    for h in range(N_HEADS):
        sl = slice(h * HEAD_DIM, (h + 1) * HEAD_DIM)
        last_col = cumlast[:, h:h + 1]
        mnew_col = mnew[:, h:h + 1]
        w_k = jnp.exp(last_col - cum[:, h:h + 1] + gi[:, h:h + 1] - mnew_col)
        decay = jnp.exp(last_col + mprev[:, h:h + 1] - mnew_col)
        kw = kg[:, sl] * w_k
        kwb = kw.astype(BF16)
        vh = vg[:, sl]
        for sq in range(nseq):
            vs = jnp.where(seg_r == sq, vh, 0.0)
            kws = jnp.where(seg_r == sq, kw, 0.0)
            dc = lax.dot_general(kwb, vs.astype(BF16), TN_DIMS, preferred_element_type=F32)
            dn = jnp.sum(kws, axis=0, keepdims=True)
            dec = decay[sq * ls:sq * ls + 1, :]
            c_ref[sq, h] = dec * c0_ref[sq, h] + dc
            n_ref[sq, h:h + 1, :] = dec * n0_ref[sq, h:h + 1, :] + dn


def _mlstm_sample(z3, gates3, mg, c0, n0, m_tok, *, bb):
    bsz, seq, _ = z3.shape
    assert bb * seq == GROUP
    zspec = lambda k: pl.BlockSpec((bb, seq, D_MLSTM), lambda b: (b, 0, k))
    c_spec = pl.BlockSpec((bb, N_HEADS, HEAD_DIM, HEAD_DIM), lambda b: (b, 0, 0, 0))
    n_spec = pl.BlockSpec((bb, N_HEADS, HEAD_DIM), lambda b: (b, 0, 0))
    m_spec = pl.BlockSpec((bb, seq, LANES), lambda b: (b, 0, 0))
    return pl.pallas_call(
        functools.partial(_mlstm_sample_kernel, nseq=bb, ls=seq),
        grid=(bsz // bb,),
        in_specs=[zspec(2), zspec(3), zspec(4), zspec(5),
                  pl.BlockSpec((bb, seq, D_GATE), lambda b: (b, 0, 0)),
                  _const_spec(mg.shape),
                  c_spec, n_spec, m_spec],
        out_specs=[pl.BlockSpec((bb, seq, D_MLSTM), lambda b: (b, 0, 0)), c_spec, n_spec, m_spec],
        out_shape=[
            jax.ShapeDtypeStruct((bsz, seq, D_MLSTM), F32),
            jax.ShapeDtypeStruct((bsz, N_HEADS, HEAD_DIM, HEAD_DIM), F32),
            jax.ShapeDtypeStruct((bsz, N_HEADS, HEAD_DIM), F32),
            jax.ShapeDtypeStruct((bsz, seq, LANES), F32),
        ],
        compiler_params=_params(1),
        name="mlstm_sample",
    )(z3, z3, z3, z3, gates3, mg, c0, n0, m_tok)


def _ffn_b_kernel(x_ref, ada_ref, y_ref, h_ref, wout_ref, g3_ref, wgu_ref, wd_ref, gf_ref, o_ref, hid_ref):
    x = x_ref[...]
    bb, tt, _ = x.shape
    n = bb * tt
    ada = ada_ref[...]
    mix = (_dot(y_ref[...].reshape(n, D_CONV).astype(BF16), wout_ref[:D_CONV, :])
           + _dot(h_ref[...].reshape(n, D_MLSTM).astype(BF16), wout_ref[D_CONV:, :]))
    x2 = x + _ada_row(ada, 5) * mix.reshape(bb, tt, D_MODEL)
    xm = _rms(x2, g3_ref[...]) * (1.0 + _ada_row(ada, 7)) + _ada_row(ada, 6)
    ff = _swiglu(xm.reshape(n, D_MODEL).astype(BF16), wgu_ref, wd_ref, hid_ref)
    x3 = x2 + 0.5 * _ada_row(ada, 8) * ff.reshape(bb, tt, D_MODEL)
    o_ref[...] = _rms(x3, gf_ref[...])


def _ffn_b(x, ada, y, h, p, *, bb):
    bsz, seq, _ = x.shape
    tok = lambda w: pl.BlockSpec((bb, seq, w), lambda b: (b, 0, 0))
    weights = [p["wout"], p["g3"], p["wgu2"], p["wd2"], p["gf"]]
    return pl.pallas_call(
        _ffn_b_kernel,
        grid=(bsz // bb,),
        in_specs=[tok(D_MODEL), pl.BlockSpec((bb, N_ADA * D_MODEL), lambda b: (b, 0)), tok(D_CONV), tok(D_MLSTM)]
        + [_const_spec(w.shape) for w in weights],
        out_specs=tok(D_MODEL),
        out_shape=jax.ShapeDtypeStruct(x.shape, F32),
        scratch_shapes=[pltpu.VMEM((bb * seq, D_FF), BF16)],
        compiler_params=_params(1),
        name="ffn_b",
    )(x, ada, y, h, *weights)


def _prompt_trunk(x, ada, p):
    x1, u, h, c_new, n_new, m_row, wgu1, wd1, win, wg = _front(x, ada, p, tt=512)
    out, conv_new, wgu2, wd2, wout = _back(x1, ada, u, h, p, tt=512, rt=32)
    bf16_weights = dict(wgu1=wgu1, wd1=wd1, win=win, wg=wg, wgu2=wgu2, wd2=wd2, wout=wout)
    return (out, conv_new[None], c_new[None], n_new[None], m_row[None, :, 0, :N_HEADS]), bf16_weights


def _sample_trunk(x, ada, states, p):
    bsz, seq, _ = x.shape
    conv_state, c0, n0, m0 = states
    x1, z, gates = _ffn_a(x, ada, p, bb=32)
    z3 = z.reshape(bsz, seq, D_Z)
    gates3 = gates.reshape(bsz, seq, D_GATE)
    y, conv_new_t = _conv(z3, jnp.transpose(conv_state, (1, 0, 2)), p, bb=32)
    conv_new = jnp.transpose(conv_new_t, (1, 0, 2))
    m_tok = jnp.broadcast_to(
        jnp.pad(m0, ((0, 0), (0, LANES - N_HEADS)))[:, None, :], (bsz, seq, LANES))
    h, c_new, n_new, m_tok_new = _mlstm_sample(z3, gates3, p["mn_g"], c0, n0, m_tok, bb=GROUP // seq)
    out = _ffn_b(x1, ada, y, h, p, bb=64)
    return out, conv_new[None], c_new[None], n_new[None], m_tok_new[None, :, seq - 1, :N_HEADS]


def _gate_cols(a):
    pad = ((0, 0), (0, LANES - N_HEADS))
    return jnp.concatenate([jnp.pad(a[:, :N_HEADS], pad), jnp.pad(a[:, N_HEADS:], pad)], axis=1)


def kernel(x_prompt, x_sample, c_prompt, c_sample, state_conv, state_C, state_n, state_m, w_ada, b_ada, norm_ffn1, ffn1_w_gu, ffn1_w_down, norm_mix, w_in, b_in, conv_w, conv_b, conv_norm_g, conv_norm_b, mlstm_norm_g, w_out, norm_ffn2, ffn2_w_gu, ffn2_w_down, norm_final):
    assert w_ada.shape[0] == 1, "single layer"
    w_in_t = jnp.transpose(w_in[0])
    p = {
        "g1": norm_ffn1, "g2": norm_mix, "g3": norm_ffn2, "gf": norm_final[None],
        "wgu1": ffn1_w_gu[0], "wd1": ffn1_w_down[0], "wgu2": ffn2_w_gu[0], "wd2": ffn2_w_down[0],
        "win": w_in_t, "wout": w_out[0],
        "b_main": b_in,
        "b_gate": _gate_cols(b_in[:, D_Z:]),
        "conv_w": jnp.pad(conv_w[0], ((0, TAIL_PAD - CONV_WIDTH), (0, 0))),
        "conv_b": conv_b, "cn_g": conv_norm_g, "cn_b": conv_norm_b, "mn_g": mlstm_norm_g,
    }
    bs = x_sample.shape[0]
    ada = _ada(jnp.concatenate([c_sample, c_prompt], axis=0), w_ada[0], b_ada)
    (yp, conv_p, c_p, n_p, m_p), bf16_weights = _prompt_trunk(x_prompt, ada[bs:], p)
    ys, conv_s, c_s, n_s, m_s = _sample_trunk(
        x_sample, ada, (state_conv[0], state_C[0], state_n[0], state_m[0]), {**p, **bf16_weights})
    return (yp, ys, conv_p, c_p, n_p, m_p, conv_s, c_s, n_s, m_s)
```

```python
import functools

import jax
import jax.numpy as jnp
from jax import lax
from jax.experimental import pallas as pl
from jax.experimental.pallas import tpu as pltpu

F32 = jnp.float32
BF16 = jnp.bfloat16

D_MODEL = 1024
D_CONV = 512
D_MLSTM = 512
N_HEADS = 4
HEAD_DIM = 128
CONV_WIDTH = 31
CONV_TAIL = CONV_WIDTH - 1
D_FF = 2816
N_ADA = 9
EPS = 1e-6
D_Z = 2 * D_CONV + 4 * D_MLSTM
D_IN = D_Z + 2 * N_HEADS
Q_OFF = 2 * D_CONV
LANES = 128
SUBLANES = 8
D_GATE = 2 * LANES
FF_CHUNK = 256
N_FF_CHUNKS = D_FF // FF_CHUNK
GROUP = 128
TAIL_PAD = 32
TAIL_LO = TAIL_PAD - CONV_TAIL
TIE_PARTS = 2
FRONT_TIE_WEIGHTS = [0.0, 0.5] + [1.0] * (N_FF_CHUNKS - 2)
CAST_SLOTS = 4
CAST_ROWS_WIDE = 64
CAST_ROWS_NARROW = 128
VMEM_LIMIT = 56 * 1024 * 1024
NT_DIMS = (((1,), (1,)), ((), ()))
TN_DIMS = (((0,), (0,)), ((), ()))


def _dot(a, b):
    return jnp.dot(a, b, preferred_element_type=F32)


def _dot_exact(a, b):
    return jnp.dot(a, b, preferred_element_type=F32, precision=lax.Precision.HIGHEST)


def _rms(x, g):
    ms = jnp.mean(x * x, axis=-1, keepdims=True)
    return x * lax.rsqrt(ms + EPS) * g


def _ada_row(ada, k):
    return ada[:, k * D_MODEL:(k + 1) * D_MODEL][:, None, :]


def _log_sigmoid(x):
    return jnp.minimum(x, 0.0) - jnp.log(1.0 + jnp.exp(-jnp.abs(x)))


def _const_spec(shape):
    nd = len(shape)
    return pl.BlockSpec(shape, lambda *_: (0,) * nd, pipeline_mode=pl.Buffered(1))


def _params(n_grid):
    return pltpu.CompilerParams(dimension_semantics=("arbitrary",) * n_grid, vmem_limit_bytes=VMEM_LIMIT)


def _ada_kernel(c_ref, w_ref, b_ref, o_ref):
    c = c_ref[...]
    s = (c * jax.nn.sigmoid(c)).astype(BF16)
    o_ref[...] = _dot(s, w_ref[...].astype(BF16)) + b_ref[...]


def _ada(c_all, w_ada, b_ada):
    n = c_all.shape[0]
    tile = D_MODEL
    return pl.pallas_call(
        _ada_kernel,
        grid=(N_ADA,),
        in_specs=[
            pl.BlockSpec((n, D_MODEL), lambda j: (0, 0)),
            pl.BlockSpec((D_MODEL, tile), lambda j: (0, j)),
            pl.BlockSpec((1, tile), lambda j: (0, j)),
        ],
        out_specs=pl.BlockSpec((n, tile), lambda j: (0, j)),
        out_shape=jax.ShapeDtypeStruct((n, N_ADA * D_MODEL), F32),
        compiler_params=_params(1),
        name="ada",
    )(c_all, w_ada, b_ada)


def _tie(value, zeros):
    rb = value.shape[0] // TIE_PARTS
    packing = 4 // value.dtype.itemsize
    reps = (rb // (SUBLANES * packing), value.shape[1] // LANES)

    def add(block, zero):
        if zero is None:
            return block
        zero = jnp.concatenate([zero] * packing, axis=0).astype(value.dtype)
        return block + jnp.tile(zero, reps)

    return jnp.concatenate([add(value[k * rb:(k + 1) * rb], z) for k, z in enumerate(zeros)], axis=0)


def _swiglu(xm, wgu_ref, wd_ref, hid_ref, between=None):
    for j in range(N_FF_CHUNKS):
        lo = j * FF_CHUNK
        zeros = None if between is None else between()
        gate = _dot(xm, wgu_ref[:, lo:lo + FF_CHUNK])
        up = _dot(xm, wgu_ref[:, D_FF + lo:D_FF + lo + FF_CHUNK])
        hidden = gate * jax.nn.sigmoid(gate) * up
        if zeros is not None:
            hidden = _tie(hidden, zeros)
        hid_ref[:, lo:lo + FF_CHUNK] = hidden.astype(BF16)
    return _dot(hid_ref[...], wd_ref[...])


def _ffn1_inproj(x, ada, g1_ref, wgu_ref, wd_ref, g2_ref, win_ref, bin_ref, wg_ref, bg_ref, hid_ref,
                 store_z, between=None):
    bb, tt, _ = x.shape
    n = bb * tt
    xm = _rms(x, g1_ref[...]) * (1.0 + _ada_row(ada, 1)) + _ada_row(ada, 0)
    ff = _swiglu(xm.reshape(n, D_MODEL).astype(BF16), wgu_ref, wd_ref, hid_ref, between)
    x1 = x + 0.5 * _ada_row(ada, 2) * ff.reshape(bb, tt, D_MODEL)
    hm = _rms(x1, g2_ref[...]) * (1.0 + _ada_row(ada, 4)) + _ada_row(ada, 3)
    hm = hm.reshape(n, D_MODEL).astype(BF16)
    for c0 in range(0, D_Z, D_CONV):
        store_z(c0, _dot(hm, win_ref[:, c0:c0 + D_CONV]) + bin_ref[:, c0:c0 + D_CONV])
    return x1, _dot(hm, wg_ref[...]) + bg_ref[...]


def _run(steps):
    for _ in steps:
        pass


def _all_bits(*arrays):
    acc = None
    for a in arrays:
        bits = lax.bitcast_convert_type(a.reshape(-1, a.shape[-1]), jnp.uint32)
        for r in range(0, bits.shape[0], SUBLANES):
            for c in range(0, bits.shape[1], LANES):
                v = bits[r:r + SUBLANES, c:c + LANES]
                acc = v if acc is None else acc | v
    return acc


def _conv_group(u, ubuf, ush, w_ref, cb_ref, lg_ref, lb_ref, store_y, store_tail, *, bt, rt):
    bb, tt, _ = u.shape
    sh_rows = TAIL_PAD - SUBLANES + tt
    ubuf[:, TAIL_PAD:TAIL_PAD + tt, :] = u
    yield None
    for r in range(1, SUBLANES):
        ush[r - 1] = ubuf[:, r:r + sh_rows, :]
        yield None
    for b0 in range(0, bb, bt):
        for r0 in range(0, tt, rt):
            acc = jnp.broadcast_to(cb_ref[...].reshape(1, 1, D_CONV), (bt, rt, D_CONV))
            for j in range(CONV_WIDTH):
                q, r = divmod(TAIL_LO + j, SUBLANES)
                p0 = r0 + q * SUBLANES
                if r == 0:
                    win = ubuf[b0:b0 + bt, p0:p0 + rt, :]
                else:
                    win = ush[r - 1, b0:b0 + bt, p0:p0 + rt, :]
                acc = acc + w_ref[j:j + 1, :].reshape(1, 1, D_CONV) * win
            mu = jnp.mean(acc, axis=-1, keepdims=True)
            xc = acc - mu
            var = jnp.mean(xc * xc, axis=-1, keepdims=True)
            yn = xc * lax.rsqrt(var + EPS) * lg_ref[...].reshape(1, 1, D_CONV) + lb_ref[...].reshape(1, 1, D_CONV)
            out = yn * jax.nn.sigmoid(yn)
            store_y(b0, r0, out)
            yield _all_bits(out)
    tail = ubuf[:, TAIL_LO + tt:TAIL_PAD + tt, :]
    ubuf[:, TAIL_LO:TAIL_PAD, :] = tail
    store_tail(tail)


def _scan_rows(x, op, ident):
    row = lax.broadcasted_iota(jnp.int32, x.shape, 0)
    s = 1
    while s < x.shape[0]:
        x = op(x, jnp.where(row >= s, pltpu.roll(x, s, axis=0), ident))
        s *= 2
    return x


def _mlstm_group(gi, gf, load_z, store_h, mg_ref, cext, mcar):
    row = lax.broadcasted_iota(jnp.int32, (GROUP, GROUP), 0)
    col = lax.broadcasted_iota(jnp.int32, (GROUP, GROUP), 1)
    causal = col <= row
    ones_b = jnp.ones((GROUP, HEAD_DIM), BF16)
    cum = _scan_rows(_log_sigmoid(gf), jnp.add, 0.0)
    gv = gi - cum
    mx = _scan_rows(gv, jnp.maximum, -jnp.inf)
    mprev = mcar[...]
    mm = jnp.maximum(mprev, mx)
    mcol = cum + mm
    mm_last = mm[GROUP - 1:GROUP, :]
    w_k = jnp.exp(gv - mm_last)
    decay = jnp.exp(mprev - mm_last)
    mcar[...] = mcol[GROUP - 1:GROUP, :]
    gv_t = gv.T
    yield _all_bits(gv_t[0:SUBLANES, :])
    for h in range(N_HEADS):
        sl = slice(h * HEAD_DIM, (h + 1) * HEAD_DIM)
        mm_b = jnp.broadcast_to(mm[:, h:h + 1], (GROUP, GROUP))
        mcol_b = jnp.broadcast_to(mcol[:, h:h + 1], (GROUP, GROUP))
        w_intra = jnp.exp(jnp.where(causal, gv_t[h:h + 1, :] - mm_b, -jnp.inf))
        w_inter = jnp.exp(mprev[:, h:h + 1] - mm_b)
        qb = load_z(0, h).astype(BF16)
        kh = load_z(1, h) * (HEAD_DIM ** -0.5)
        v_ext = jnp.concatenate([load_z(2, h).astype(BF16), ones_b], axis=1)
        s = lax.dot_general(qb, kh.astype(BF16), NT_DIMS, preferred_element_type=F32)
        pv = _dot((s * w_intra).astype(BF16), v_ext)
        cprev = cext[h]
        qc = _dot(qb, cprev.astype(BF16))
        num = pv[:, :HEAD_DIM] + w_inter * qc[:, :HEAD_DIM]
        den = pv[:, HEAD_DIM:] + w_inter * qc[:, HEAD_DIM:]
        hh = num / jnp.maximum(jnp.abs(den), jnp.exp(-mcol_b))
        hh = hh * lax.rsqrt(jnp.mean(hh * hh, axis=-1, keepdims=True) + EPS) * mg_ref[:, sl]
        out = jax.nn.sigmoid(load_z(3, h)) * hh
        store_h(h, out)
        kw = (kh * w_k[:, h:h + 1]).astype(BF16)
        dc = lax.dot_general(kw, v_ext, TN_DIMS, preferred_element_type=F32)
        cnew = decay[:, h:h + 1] * cprev + dc
        cext[h] = cnew
        yield _all_bits(out, cnew)


def _piece_feeder(pieces, costs, call_weights):
    point_weights = [w / TIE_PARTS for w in call_weights for _ in range(TIE_PARTS)]
    done = [0, 0]

    def one_point():
        done[0] += 1
        target = sum(costs) * sum(point_weights[:done[0]]) / sum(point_weights)
        bits = None
        while done[1] < len(costs) and sum(costs[:done[1]]) + costs[done[1]] / 2 <= target:
            piece_bits = next(pieces)
            if piece_bits is not None:
                bits = piece_bits if bits is None else bits | piece_bits
            done[1] += 1
        if bits is None:
            return None
        zero_bits = lax.shift_right_logical(lax.shift_right_logical(bits, jnp.uint32(16)), jnp.uint32(16))
        return lax.bitcast_convert_type(zero_bits, F32)

    return lambda: [one_point() for _ in range(TIE_PARTS)]


def _weight_export(k, vmem_refs, out_refs, sems):
    return pltpu.make_async_copy(vmem_refs[k], out_refs[k], sems.at[k])


def _cast_scratch(n_exports):
    return [
        pltpu.VMEM((CAST_SLOTS, CAST_ROWS_WIDE, 2 * D_FF), F32),
        pltpu.VMEM((CAST_SLOTS, CAST_ROWS_NARROW, D_MODEL), F32),
        pltpu.SemaphoreType.DMA((CAST_SLOTS,)),
        pltpu.SemaphoreType.DMA((CAST_SLOTS,)),
        pltpu.SemaphoreType.DMA((n_exports,)),
    ]


def _stream_rows(src, n_chunks, stage, sems, consume):
    depth, chunk_rows, cols = stage.shape

    def load(i):
        return pltpu.make_async_copy(src.at[pl.ds(i * chunk_rows, chunk_rows), pl.ds(0, cols)],
                                     stage.at[i % depth], sems.at[i % depth])

    for i in range(min(depth - 1, n_chunks)):
        load(i).start()
    for i in range(n_chunks):
        if i + depth - 1 < n_chunks:
            load(i + depth - 1).start()
        load(i).wait()
        consume(i, stage[i % depth])


def _weights_to_bf16(wgu_hbm, wd_hbm, third_hbm, wgu_ref, wd_ref, third_ref, third_is_transposed,
                     stage_wide, stage_narrow, sems_wide, sems_narrow):
    wide_rows, narrow_rows = stage_wide.shape[1], stage_narrow.shape[1]

    def put_wgu(i, chunk):
        wgu_ref[i * wide_rows:(i + 1) * wide_rows, :] = chunk.astype(BF16)

    def put_wd(i, chunk):
        wd_ref[i * narrow_rows:(i + 1) * narrow_rows, :] = chunk.astype(BF16)

    def put_third(i, chunk):
        if third_is_transposed:
            third_ref[:, i * narrow_rows:(i + 1) * narrow_rows] = chunk.T.astype(BF16)
        else:
            third_ref[i * narrow_rows:(i + 1) * narrow_rows, :] = chunk.astype(BF16)

    third_rows = third_ref.shape[1] if third_is_transposed else third_ref.shape[0]
    _stream_rows(wgu_hbm, wgu_ref.shape[0] // wide_rows, stage_wide, sems_wide, put_wgu)
    _stream_rows(wd_hbm, wd_ref.shape[0] // narrow_rows, stage_narrow, sems_narrow, put_wd)
    _stream_rows(third_hbm, third_rows // narrow_rows, stage_narrow, sems_narrow, put_third)


def _gate_weights(win_hbm, wg_ref, stage_narrow, sems_narrow):
    rows = stage_narrow.shape[1]
    tail = pltpu.make_async_copy(win_hbm.at[pl.ds(D_IN - rows, rows), :], stage_narrow.at[0], sems_narrow.at[0])
    tail.start()
    tail.wait()
    cols = stage_narrow[0].T
    lane = lax.broadcasted_iota(jnp.int32, cols.shape, 1)
    i_part = jnp.where(lane < N_HEADS, pltpu.roll(cols, 2 * N_HEADS, axis=1), 0.0)
    f_part = jnp.where(lane < N_HEADS, pltpu.roll(cols, N_HEADS, axis=1), 0.0)
    wg_ref[...] = jnp.concatenate([i_part, f_part], axis=1).astype(BF16)


def _front_kernel(x_ref, ada_ref, wgu_hbm, wd_hbm, win_hbm, g1_ref, g2_ref, bin_ref, bg_ref, mg_ref,
                  x1_ref, u_ref, h_ref, c_ref, n_ref, m_ref, wgu_out, wd_out, win_out, wg_out,
                  hid_ref, zbuf, gbuf, cext, mcar, wgu_ref, wd_ref, win_ref, wg_ref,
                  stage_wide, stage_narrow, sems_wide, sems_narrow, out_sems, *, tt, nt, steps):
    s = pl.program_id(0)
    tl = lax.rem(jnp.maximum(s - 1, 0), nt)
    resident = (wgu_ref, wd_ref, win_ref, wg_ref)
    exported = (wgu_out, wd_out, win_out, wg_out)

    @pl.when(s == 0)
    def _():
        _weights_to_bf16(wgu_hbm, wd_hbm, win_hbm, wgu_ref, wd_ref, win_ref, True,
                         stage_wide, stage_narrow, sems_wide, sems_narrow)
        _gate_weights(win_hbm, wg_ref, stage_narrow, sems_narrow)
        for k in range(len(resident)):
            _weight_export(k, resident, exported, out_sems).start()
        zbuf[...] = jnp.zeros(zbuf.shape, F32)
        gbuf[...] = jnp.zeros(gbuf.shape, F32)

    @pl.when(s == steps)
    def _():
        for k in range(len(resident)):
            _weight_export(k, resident, exported, out_sems).wait()

    @pl.when(tl == 0)
    def _():
        cext[...] = jnp.zeros(cext.shape, F32)
        mcar[...] = jnp.zeros(mcar.shape, F32)

    def mlstm_pieces(g):
        rs = slice(g * GROUP, (g + 1) * GROUP)

        def load_z(which, h):
            c0 = which * D_MLSTM + h * HEAD_DIM
            return zbuf[rs, c0:c0 + HEAD_DIM]

        def store_h(h, val):
            h_ref[0, rs, h * HEAD_DIM:(h + 1) * HEAD_DIM] = val.astype(BF16)

        return _mlstm_group(gbuf[rs, 0:LANES], gbuf[rs, LANES:D_GATE], load_z, store_h, mg_ref, cext, mcar)

    def mixer_pieces():
        for g in range(tt // GROUP):
            yield from mlstm_pieces(g)

    @pl.when(s < steps)
    def _():
        pieces = mixer_pieces()
        costs = ([10] + [5] * N_HEADS) * (tt // GROUP)
        glu_in = []

        def store_z(c0, zc):
            if c0 < Q_OFF:
                glu_in.append(zc)
                if len(glu_in) == 2:
                    u_ref[0] = glu_in[0] * jax.nn.sigmoid(glu_in[1])
            else:
                if c0 == Q_OFF:
                    _run(pieces)
                zbuf[:, c0 - Q_OFF:c0 - Q_OFF + D_CONV] = zc

        x1, gates = _ffn1_inproj(x_ref[...], ada_ref[pl.ds(s // nt, 1), :], g1_ref, wgu_ref, wd_ref, g2_ref,
                                 win_ref, bin_ref, wg_ref, bg_ref, hid_ref, store_z,
                                 _piece_feeder(pieces, costs, FRONT_TIE_WEIGHTS))
        x1_ref[...] = x1
        gbuf[...] = gates

    @pl.when(s == steps)
    def _():
        _run(mixer_pieces())

    @pl.when(tl == nt - 1)
    def _():
        for h in range(N_HEADS):
            c_ref[0, h] = cext[h, :, :HEAD_DIM]
            n_ref[0, h:h + 1, :] = cext[h, :, HEAD_DIM:].T[0:1, :]
        m_ref[0] = mcar[...]


def _front(x, ada, p, *, tt):
    bsz, seq, _ = x.shape
    nt = seq // tt
    steps = bsz * nt
    cur = lambda s: jnp.minimum(s, steps - 1)
    lag = lambda s: jnp.maximum(s - 1, 0)
    big = [p["wgu1"], p["wd1"], p["win"]]
    big_shapes = [(D_MODEL, 2 * D_FF), (D_FF, D_MODEL), (D_MODEL, D_Z), (D_MODEL, D_GATE)]
    small = [p["g1"], p["g2"], p["b_main"], p["b_gate"], p["mn_g"]]
    any_spec = pl.BlockSpec(memory_space=pl.ANY)
    return pl.pallas_call(
        functools.partial(_front_kernel, tt=tt, nt=nt, steps=steps),
        grid=(steps + 1,),
        in_specs=[
            pl.BlockSpec((1, tt, D_MODEL), lambda s: (cur(s) // nt, cur(s) % nt, 0)),
            _const_spec(ada.shape),
        ] + [any_spec] * len(big) + [_const_spec(w.shape) for w in small],
        out_specs=[
            pl.BlockSpec((1, tt, D_MODEL), lambda s: (cur(s) // nt, cur(s) % nt, 0)),
            pl.BlockSpec((1, tt, D_CONV), lambda s: (cur(s) // nt, cur(s) % nt, 0)),
            pl.BlockSpec((1, tt, D_MLSTM), lambda s: (lag(s) // nt, lag(s) % nt, 0)),
            pl.BlockSpec((1, N_HEADS, HEAD_DIM, HEAD_DIM), lambda s: (lag(s) // nt, 0, 0, 0)),
            pl.BlockSpec((1, N_HEADS, HEAD_DIM), lambda s: (lag(s) // nt, 0, 0)),
            pl.BlockSpec((1, 1, LANES), lambda s: (lag(s) // nt, 0, 0)),
        ] + [any_spec] * len(big_shapes),
        out_shape=[
            jax.ShapeDtypeStruct(x.shape, F32),
            jax.ShapeDtypeStruct((bsz, seq, D_CONV), F32),
            jax.ShapeDtypeStruct((bsz, seq, D_MLSTM), BF16),
            jax.ShapeDtypeStruct((bsz, N_HEADS, HEAD_DIM, HEAD_DIM), F32),
            jax.ShapeDtypeStruct((bsz, N_HEADS, HEAD_DIM), F32),
            jax.ShapeDtypeStruct((bsz, 1, LANES), F32),
        ] + [jax.ShapeDtypeStruct(shape, BF16) for shape in big_shapes],
        scratch_shapes=[
            pltpu.VMEM((tt, D_FF), BF16),
            pltpu.VMEM((tt, 4 * D_MLSTM), F32),
            pltpu.VMEM((tt, D_GATE), F32),
            pltpu.VMEM((N_HEADS, HEAD_DIM, 2 * HEAD_DIM), F32),
            pltpu.VMEM((1, LANES), F32),
        ] + [pltpu.VMEM(shape, BF16) for shape in big_shapes] + _cast_scratch(len(big_shapes)),
        compiler_params=_params(1),
        name="front",
    )(x, ada, *big, *small)


def _back_kernel(x_ref, ada_ref, u_ref, h_ref, wgu_hbm, wd_hbm, wout_hbm, g3_ref, gf_ref,
                 cw_ref, cb_ref, lg_ref, lb_ref,
                 o_ref, cs_ref, wgu_out, wd_out, wout_out,
                 hid_ref, ubuf, ush, ycur, ynext, wgu_ref, wd_ref, wout_ref,
                 stage_wide, stage_narrow, sems_wide, sems_narrow, out_sems, *, tt, nt, rt, steps):
    s = pl.program_id(0)
    tc = lax.rem(jnp.minimum(s, steps - 1), nt)
    resident = (wgu_ref, wd_ref, wout_ref)
    exported = (wgu_out, wd_out, wout_out)

    @pl.when(s == 0)
    def _():
        _weights_to_bf16(wgu_hbm, wd_hbm, wout_hbm, wgu_ref, wd_ref, wout_ref, False,
                         stage_wide, stage_narrow, sems_wide, sems_narrow)
        for k in range(len(resident)):
            _weight_export(k, resident, exported, out_sems).start()
        ycur[...] = jnp.zeros(ycur.shape, BF16)

    @pl.when(s == steps)
    def _():
        for k in range(len(resident)):
            _weight_export(k, resident, exported, out_sems).wait()

    @pl.when(tc == 0)
    def _():
        ubuf[:, 0:TAIL_PAD, :] = jnp.zeros((1, TAIL_PAD, D_CONV), F32)

    def store_y(b0, r0, val):
        ynext[r0:r0 + rt, :] = val[0].astype(BF16)

    def matmul_half(pieces, between):
        x = x_ref[...]
        ada = ada_ref[pl.ds(jnp.maximum(s - 1, 0) // nt, 1), :]
        mix = _dot(ycur[...], wout_ref[:D_CONV, :]) + _dot(h_ref[0], wout_ref[D_CONV:, :])
        if between is not None:
            mix = _tie(mix, between())
        x2 = x + _ada_row(ada, 5) * mix.reshape(1, tt, D_MODEL)
        xm = _rms(x2, g3_ref[...]) * (1.0 + _ada_row(ada, 7)) + _ada_row(ada, 6)
        ff = _swiglu(xm.reshape(tt, D_MODEL).astype(BF16), wgu_ref, wd_ref, hid_ref, between)
        _run(pieces)
        x3 = x2 + 0.5 * _ada_row(ada, 8) * ff.reshape(1, tt, D_MODEL)
        o_ref[...] = _rms(x3, gf_ref[...])

    @pl.when(s < steps)
    def _():
        pieces = _conv_group(u_ref[...], ubuf, ush, cw_ref, cb_ref, lg_ref, lb_ref, store_y,
                             lambda tail: None, bt=1, rt=rt)
        costs = [2] + [5] * (SUBLANES - 1) + [7 * rt // (2 * SUBLANES)] * (tt // rt)
        matmul_half(pieces, _piece_feeder(pieces, costs, [1.0] * (1 + N_FF_CHUNKS)))
        ycur[...] = ynext[...]

    @pl.when(s == steps)
    def _():
        matmul_half((), None)

    @pl.when(tc == nt - 1)
    def _():
        cs_ref[...] = ubuf[:, TAIL_LO:TAIL_PAD, :]


def _back(x1, ada, u, h, p, *, tt, rt):
    bsz, seq, _ = x1.shape
    nt = seq // tt
    steps = bsz * nt
    cur = lambda s: jnp.minimum(s, steps - 1)
    lag = lambda s: jnp.maximum(s - 1, 0)
    big = [p["wgu2"], p["wd2"], p["wout"]]
    big_shapes = [(D_MODEL, 2 * D_FF), (D_FF, D_MODEL), (D_MODEL, D_MODEL)]
    small = [p["g3"], p["gf"], p["conv_w"], p["conv_b"], p["cn_g"], p["cn_b"]]
    any_spec = pl.BlockSpec(memory_space=pl.ANY)
    return pl.pallas_call(
        functools.partial(_back_kernel, tt=tt, nt=nt, rt=rt, steps=steps),
        grid=(steps + 1,),
        in_specs=[
            pl.BlockSpec((1, tt, D_MODEL), lambda s: (lag(s) // nt, lag(s) % nt, 0)),
            _const_spec(ada.shape),
            pl.BlockSpec((1, tt, D_CONV), lambda s: (cur(s) // nt, cur(s) % nt, 0)),
            pl.BlockSpec((1, tt, D_MLSTM), lambda s: (lag(s) // nt, lag(s) % nt, 0)),
        ] + [any_spec] * len(big) + [_const_spec(w.shape) for w in small],
        out_specs=[
            pl.BlockSpec((1, tt, D_MODEL), lambda s: (lag(s) // nt, lag(s) % nt, 0)),
            pl.BlockSpec((1, CONV_TAIL, D_CONV), lambda s: (cur(s) // nt, 0, 0)),
        ] + [any_spec] * len(big_shapes),
        out_shape=[
            jax.ShapeDtypeStruct(x1.shape, F32),
            jax.ShapeDtypeStruct((bsz, CONV_TAIL, D_CONV), F32),
        ] + [jax.ShapeDtypeStruct(shape, BF16) for shape in big_shapes],
        scratch_shapes=[
            pltpu.VMEM((tt, D_FF), BF16),
            pltpu.VMEM((1, TAIL_PAD + tt, D_CONV), F32),
            pltpu.VMEM((SUBLANES - 1, 1, TAIL_PAD - SUBLANES + tt, D_CONV), F32),
            pltpu.VMEM((tt, D_CONV), BF16),
            pltpu.VMEM((tt, D_CONV), BF16),
        ] + [pltpu.VMEM(shape, BF16) for shape in big_shapes] + _cast_scratch(len(big_shapes)),
        compiler_params=_params(1),
        name="back",
    )(x1, ada, u, h, *big, *small)


def _ffn_a_kernel(x_ref, ada_ref, g1_ref, wgu_ref, wd_ref, g2_ref, win_ref, bin_ref, wg_ref, bg_ref,
                  x1_ref, z_ref, gate_ref, hid_ref):
    def store_z(c0, zc):
        z_ref[:, c0:c0 + D_CONV] = zc

    x1, gates = _ffn1_inproj(x_ref[...], ada_ref[...], g1_ref, wgu_ref, wd_ref, g2_ref,
                             win_ref, bin_ref, wg_ref, bg_ref, hid_ref, store_z)
    x1_ref[...] = x1
    gate_ref[...] = gates


def _ffn_a(x, ada, p, *, bb):
    bsz, seq, _ = x.shape
    n = bb * seq
    weights = [p["g1"], p["wgu1"], p["wd1"], p["g2"], p["win"], p["b_main"], p["wg"], p["b_gate"]]
    return pl.pallas_call(
        _ffn_a_kernel,
        grid=(bsz // bb,),
        in_specs=[
            pl.BlockSpec((bb, seq, D_MODEL), lambda b: (b, 0, 0)),
            pl.BlockSpec((bb, N_ADA * D_MODEL), lambda b: (b, 0)),
        ] + [_const_spec(w.shape) for w in weights],
        out_specs=[
            pl.BlockSpec((bb, seq, D_MODEL), lambda b: (b, 0, 0)),
            pl.BlockSpec((n, D_Z), lambda b: (b, 0)),
            pl.BlockSpec((n, D_GATE), lambda b: (b, 0)),
        ],
        out_shape=[
            jax.ShapeDtypeStruct(x.shape, F32),
            jax.ShapeDtypeStruct((bsz * seq, D_Z), F32),
            jax.ShapeDtypeStruct((bsz * seq, D_GATE), F32),
        ],
        scratch_shapes=[pltpu.VMEM((n, D_FF), BF16)],
        compiler_params=_params(1),
        name="ffn_a",
    )(x, ada, *weights)


def _conv_kernel(a_ref, b_ref, cs_ref, w_ref, cb_ref, lg_ref, lb_ref, y_ref, so_ref, full):
    seq = a_ref.shape[1]
    full[0:CONV_TAIL] = cs_ref[...]
    for t in range(seq):
        full[CONV_TAIL + t] = a_ref[:, t, :] * jax.nn.sigmoid(b_ref[:, t, :])
    for t in range(seq):
        acc = jnp.broadcast_to(cb_ref[...], full.shape[1:])
        for j in range(CONV_WIDTH):
            acc = acc + w_ref[j:j + 1, :] * full[t + j]
        mu = jnp.mean(acc, axis=-1, keepdims=True)
        xc = acc - mu
        var = jnp.mean(xc * xc, axis=-1, keepdims=True)
        yn = xc * lax.rsqrt(var + EPS) * lg_ref[...] + lb_ref[...]
        y_ref[:, t, :] = yn * jax.nn.sigmoid(yn)
    so_ref[...] = full[seq:seq + CONV_TAIL]


def _conv(z3, state_t, p, *, bb):
    bsz, seq, _ = z3.shape
    weights = [p["conv_w"], p["conv_b"], p["cn_g"], p["cn_b"]]
    state_spec = pl.BlockSpec((CONV_TAIL, bb, D_CONV), lambda b: (0, b, 0))
    return pl.pallas_call(
        _conv_kernel,
        grid=(bsz // bb,),
        in_specs=[
            pl.BlockSpec((bb, seq, D_CONV), lambda b: (b, 0, 0)),
            pl.BlockSpec((bb, seq, D_CONV), lambda b: (b, 0, 1)),
            state_spec,
        ] + [_const_spec(w.shape) for w in weights],
        out_specs=[pl.BlockSpec((bb, seq, D_CONV), lambda b: (b, 0, 0)), state_spec],
        out_shape=[
            jax.ShapeDtypeStruct((bsz, seq, D_CONV), F32),
            jax.ShapeDtypeStruct((CONV_TAIL, bsz, D_CONV), F32),
        ],
        scratch_shapes=[pltpu.VMEM((CONV_TAIL + seq, bb, D_CONV), F32)],
        compiler_params=_params(1),
        name="conv",
    )(z3, z3, state_t, *weights)


def _mlstm_sample_kernel(q_ref, k_ref, v_ref, o_ref, gt_ref, mg_ref, c0_ref, n0_ref, m0_ref,
                         h_ref, c_ref, n_ref, m_ref, *, nseq, ls):
    row = lax.broadcasted_iota(jnp.int32, (GROUP, GROUP), 0)
    col = lax.broadcasted_iota(jnp.int32, (GROUP, GROUP), 1)
    shift = ls.bit_length() - 1
    seg_r = lax.shift_right_logical(row, shift)
    seg_c = lax.shift_right_logical(col, shift)
    causal = jnp.logical_and(col <= row, seg_r == seg_c)
    segtril = causal.astype(F32)
    lastsel = (col == seg_r * ls + (ls - 1)).astype(F32)

    gates = gt_ref[...].reshape(GROUP, D_GATE)
    gi = gates[:, :LANES]
    cum = _dot_exact(segtril, _log_sigmoid(gates[:, LANES:]))
    cumlast = _dot_exact(lastsel, cum)
    cum_t = cum.T
    gi_t = gi.T
    mprev = m0_ref[...].reshape(GROUP, LANES)
    qg = q_ref[...].reshape(GROUP, D_MLSTM)
    kg = k_ref[...].reshape(GROUP, D_MLSTM) * (HEAD_DIM ** -0.5)
    vg = v_ref[...].reshape(GROUP, D_MLSTM)
    og = o_ref[...].reshape(GROUP, D_MLSTM)

    m_all = jnp.zeros((GROUP, LANES), F32)
    lane = lax.broadcasted_iota(jnp.int32, (GROUP, LANES), 1)
    hs = []
    for h in range(N_HEADS):
        sl = slice(h * HEAD_DIM, (h + 1) * HEAD_DIM)
        cum_col = cum[:, h:h + 1]
        dmat = jnp.where(causal, cum_col - cum_t[h:h + 1, :] + gi_t[h:h + 1, :], -jnp.inf)
        inter = cum_col + mprev[:, h:h + 1]
        m_col = jnp.maximum(inter, jnp.max(dmat, axis=-1, keepdims=True))
        w_intra = jnp.exp(dmat - m_col)
        w_inter = jnp.exp(inter - m_col)
        qh = qg[:, sl]
        qb = qh.astype(BF16)
        s = lax.dot_general(qb, kg[:, sl].astype(BF16), NT_DIMS, preferred_element_type=F32)
        p = s * w_intra
        num = _dot(p.astype(BF16), vg[:, sl].astype(BF16))
        den = jnp.sum(p, axis=-1, keepdims=True)
        qc = jnp.zeros((GROUP, HEAD_DIM), F32)
        qn = jnp.zeros((GROUP, 1), F32)
        for sq in range(nseq):
            qc = jnp.where(seg_r == sq, _dot(qb, c0_ref[sq, h].astype(BF16)), qc)
            qn = jnp.where(seg_r[:, 0:1] == sq,
                           jnp.sum(qh * n0_ref[sq, h:h + 1, :], axis=-1, keepdims=True), qn)
        num = num + w_inter * qc
        den = den + w_inter * qn
        hh = num / jnp.maximum(jnp.abs(den), jnp.exp(-m_col))
        hh = hh * lax.rsqrt(jnp.mean(hh * hh, axis=-1, keepdims=True) + EPS) * mg_ref[:, sl]
        hs.append(jax.nn.sigmoid(og[:, sl]) * hh)
        m_all = jnp.where(lane == h, m_col, m_all)

    h_ref[...] = jnp.concatenate(hs, axis=-1).reshape(h_ref.shape)
    m_ref[...] = m_all.reshape(m_ref.shape)

    mnew = _dot_exact(lastsel, m_all)
    for h in range(N_HEADS):
        sl = slice(h * HEAD_DIM, (h + 1) * HEAD_DIM)
        last_col = cumlast[:, h:h + 1]
        mnew_col = mnew[:, h:h + 1]
        w_k = jnp.exp(last_col - cum[:, h:h + 1] + gi[:, h:h + 1] - mnew_col)
        decay = jnp.exp(last_col + mprev[:, h:h + 1] - mnew_col)
        kw = kg[:, sl] * w_k
        kwb = kw.astype(BF16)
        vh = vg[:, sl]
        for sq in range(nseq):
            vs = jnp.where(seg_r == sq, vh, 0.0)
            kws = jnp.where(seg_r == sq, kw, 0.0)
            dc = lax.dot_general(kwb, vs.astype(BF16), TN_DIMS, preferred_element_type=F32)
            dn = jnp.sum(kws, axis=0, keepdims=True)
            dec = decay[sq * ls:sq * ls + 1, :]
            c_ref[sq, h] = dec * c0_ref[sq, h] + dc
            n_ref[sq, h:h + 1, :] = dec * n0_ref[sq, h:h + 1, :] + dn


def _mlstm_sample(z3, gates3, mg, c0, n0, m_tok, *, bb):
    bsz, seq, _ = z3.shape
    assert bb * seq == GROUP
    zspec = lambda k: pl.BlockSpec((bb, seq, D_MLSTM), lambda b: (b, 0, k))
    c_spec = pl.BlockSpec((bb, N_HEADS, HEAD_DIM, HEAD_DIM), lambda b: (b, 0, 0, 0))
    n_spec = pl.BlockSpec((bb, N_HEADS, HEAD_DIM), lambda b: (b, 0, 0))
    m_spec = pl.BlockSpec((bb, seq, LANES), lambda b: (b, 0, 0))
    return pl.pallas_call(
        functools.partial(_mlstm_sample_kernel, nseq=bb, ls=seq),
        grid=(bsz // bb,),
        in_specs=[zspec(2), zspec(3), zspec(4), zspec(5),
                  pl.BlockSpec((bb, seq, D_GATE), lambda b: (b, 0, 0)),
                  _const_spec(mg.shape),
                  c_spec, n_spec, m_spec],
        out_specs=[pl.BlockSpec((bb, seq, D_MLSTM), lambda b: (b, 0, 0)), c_spec, n_spec, m_spec],
        out_shape=[
            jax.ShapeDtypeStruct((bsz, seq, D_MLSTM), F32),
            jax.ShapeDtypeStruct((bsz, N_HEADS, HEAD_DIM, HEAD_DIM), F32),
            jax.ShapeDtypeStruct((bsz, N_HEADS, HEAD_DIM), F32),
            jax.ShapeDtypeStruct((bsz, seq, LANES), F32),
        ],
        compiler_params=_params(1),
        name="mlstm_sample",
    )(z3, z3, z3, z3, gates3, mg, c0, n0, m_tok)


def _ffn_b_kernel(x_ref, ada_ref, y_ref, h_ref, wout_ref, g3_ref, wgu_ref, wd_ref, gf_ref, o_ref, hid_ref):
    x = x_ref[...]
    bb, tt, _ = x.shape
    n = bb * tt
    ada = ada_ref[...]
    mix = (_dot(y_ref[...].reshape(n, D_CONV).astype(BF16), wout_ref[:D_CONV, :])
           + _dot(h_ref[...].reshape(n, D_MLSTM).astype(BF16), wout_ref[D_CONV:, :]))
    x2 = x + _ada_row(ada, 5) * mix.reshape(bb, tt, D_MODEL)
    xm = _rms(x2, g3_ref[...]) * (1.0 + _ada_row(ada, 7)) + _ada_row(ada, 6)
    ff = _swiglu(xm.reshape(n, D_MODEL).astype(BF16), wgu_ref, wd_ref, hid_ref)
    x3 = x2 + 0.5 * _ada_row(ada, 8) * ff.reshape(bb, tt, D_MODEL)
    o_ref[...] = _rms(x3, gf_ref[...])


def _ffn_b(x, ada, y, h, p, *, bb):
    bsz, seq, _ = x.shape
    tok = lambda w: pl.BlockSpec((bb, seq, w), lambda b: (b, 0, 0))
    weights = [p["wout"], p["g3"], p["wgu2"], p["wd2"], p["gf"]]
    return pl.pallas_call(
        _ffn_b_kernel,
        grid=(bsz // bb,),
        in_specs=[tok(D_MODEL), pl.BlockSpec((bb, N_ADA * D_MODEL), lambda b: (b, 0)), tok(D_CONV), tok(D_MLSTM)]
        + [_const_spec(w.shape) for w in weights],
        out_specs=tok(D_MODEL),
        out_shape=jax.ShapeDtypeStruct(x.shape, F32),
        scratch_shapes=[pltpu.VMEM((bb * seq, D_FF), BF16)],
        compiler_params=_params(1),
        name="ffn_b",
    )(x, ada, y, h, *weights)


def _prompt_trunk(x, ada, p):
    x1, u, h, c_new, n_new, m_row, wgu1, wd1, win, wg = _front(x, ada, p, tt=512)
    out, conv_new, wgu2, wd2, wout = _back(x1, ada, u, h, p, tt=512, rt=32)
    bf16_weights = dict(wgu1=wgu1, wd1=wd1, win=win, wg=wg, wgu2=wgu2, wd2=wd2, wout=wout)
    return (out, conv_new[None], c_new[None], n_new[None], m_row[None, :, 0, :N_HEADS]), bf16_weights


def _sample_trunk(x, ada, states, p):
    bsz, seq, _ = x.shape
    conv_state, c0, n0, m0 = states
    x1, z, gates = _ffn_a(x, ada, p, bb=32)
    z3 = z.reshape(bsz, seq, D_Z)
    gates3 = gates.reshape(bsz, seq, D_GATE)
    y, conv_new_t = _conv(z3, jnp.transpose(conv_state, (1, 0, 2)), p, bb=32)
    conv_new = jnp.transpose(conv_new_t, (1, 0, 2))
    m_tok = jnp.broadcast_to(
        jnp.pad(m0, ((0, 0), (0, LANES - N_HEADS)))[:, None, :], (bsz, seq, LANES))
    h, c_new, n_new, m_tok_new = _mlstm_sample(z3, gates3, p["mn_g"], c0, n0, m_tok, bb=GROUP // seq)
    out = _ffn_b(x1, ada, y, h, p, bb=64)
    return out, conv_new[None], c_new[None], n_new[None], m_tok_new[None, :, seq - 1, :N_HEADS]


def _gate_cols(a):
    pad = ((0, 0), (0, LANES - N_HEADS))
    return jnp.concatenate([jnp.pad(a[:, :N_HEADS], pad), jnp.pad(a[:, N_HEADS:], pad)], axis=1)


def kernel(x_prompt, x_sample, c_prompt, c_sample, state_conv, state_C, state_n, state_m, w_ada, b_ada, norm_ffn1, ffn1_w_gu, ffn1_w_down, norm_mix, w_in, b_in, conv_w, conv_b, conv_norm_g, conv_norm_b, mlstm_norm_g, w_out, norm_ffn2, ffn2_w_gu, ffn2_w_down, norm_final):
    assert w_ada.shape[0] == 1, "single layer"
    w_in_t = jnp.transpose(w_in[0])
    p = {
        "g1": norm_ffn1, "g2": norm_mix, "g3": norm_ffn2, "gf": norm_final[None],
        "wgu1": ffn1_w_gu[0], "wd1": ffn1_w_down[0], "wgu2": ffn2_w_gu[0], "wd2": ffn2_w_down[0],
        "win": w_in_t, "wout": w_out[0],
        "b_main": b_in,
        "b_gate": _gate_cols(b_in[:, D_Z:]),
        "conv_w": jnp.pad(conv_w[0], ((0, TAIL_PAD - CONV_WIDTH), (0, 0))),
        "conv_b": conv_b, "cn_g": conv_norm_g, "cn_b": conv_norm_b, "mn_g": mlstm_norm_g,
    }
    bs = x_sample.shape[0]
    ada = _ada(jnp.concatenate([c_sample, c_prompt], axis=0), w_ada[0], b_ada)
    (yp, conv_p, c_p, n_p, m_p), bf16_weights = _prompt_trunk(x_prompt, ada[bs:], p)
    ys, conv_s, c_s, n_s, m_s = _sample_trunk(
        x_sample, ada, (state_conv[0], state_C[0], state_n[0], state_m[0]), {**p, **bf16_weights})
    return (yp, ys, conv_p, c_p, n_p, m_p, conv_s, c_s, n_s, m_s)
```

```python
import functools

import jax
import jax.numpy as jnp
from jax import lax
from jax.experimental import pallas as pl
from jax.experimental.pallas import tpu as pltpu

F32 = jnp.float32
BF16 = jnp.bfloat16

D_MODEL = 1024
D_CONV = 512
D_MLSTM = 512
N_HEADS = 4
HEAD_DIM = 128
CONV_WIDTH = 31
CONV_TAIL = CONV_WIDTH - 1
D_FF = 2816
N_ADA = 9
EPS = 1e-6
D_Z = 2 * D_CONV + 4 * D_MLSTM
D_IN = D_Z + 2 * N_HEADS
Q_OFF = 2 * D_CONV
LANES = 128
SUBLANES = 8
D_GATE = 2 * LANES
FF_CHUNK = 256
N_FF_CHUNKS = D_FF // FF_CHUNK
GROUP = 128
TAIL_PAD = 32
TAIL_LO = TAIL_PAD - CONV_TAIL
TIE_PARTS = 2
FRONT_TIE_WEIGHTS = [0.0, 0.5] + [1.0] * (N_FF_CHUNKS - 2)
CAST_SLOTS = 4
CAST_ROWS_WIDE = 64
CAST_ROWS_NARROW = 128
VMEM_LIMIT = 56 * 1024 * 1024
NT_DIMS = (((1,), (1,)), ((), ()))
TN_DIMS = (((0,), (0,)), ((), ()))


def _dot(a, b):
    return jnp.dot(a, b, preferred_element_type=F32)


def _dot_exact(a, b):
    return jnp.dot(a, b, preferred_element_type=F32, precision=lax.Precision.HIGHEST)


def _rms(x, g):
    ms = jnp.mean(x * x, axis=-1, keepdims=True)
    return x * lax.rsqrt(ms + EPS) * g


def _ada_row(ada, k):
    return ada[:, k * D_MODEL:(k + 1) * D_MODEL][:, None, :]


def _log_sigmoid(x):
    return jnp.minimum(x, 0.0) - jnp.log(1.0 + jnp.exp(-jnp.abs(x)))


def _const_spec(shape):
    nd = len(shape)
    return pl.BlockSpec(shape, lambda *_: (0,) * nd, pipeline_mode=pl.Buffered(1))


def _params(n_grid):
    return pltpu.CompilerParams(dimension_semantics=("arbitrary",) * n_grid, vmem_limit_bytes=VMEM_LIMIT)


def _ada_kernel(c_ref, w_ref, b_ref, o_ref):
    c = c_ref[...]
    s = (c * jax.nn.sigmoid(c)).astype(BF16)
    o_ref[...] = _dot(s, w_ref[...].astype(BF16)) + b_ref[...]


def _ada(c_all, w_ada, b_ada):
    n = c_all.shape[0]
    tile = 3 * D_MODEL
    return pl.pallas_call(
        _ada_kernel,
        grid=(N_ADA * D_MODEL // tile,),
        in_specs=[
            pl.BlockSpec((n, D_MODEL), lambda j: (0, 0)),
            pl.BlockSpec((D_MODEL, tile), lambda j: (0, j)),
            pl.BlockSpec((1, tile), lambda j: (0, j)),
        ],
        out_specs=pl.BlockSpec((n, tile), lambda j: (0, j)),
        out_shape=jax.ShapeDtypeStruct((n, N_ADA * D_MODEL), F32),
        compiler_params=_params(1),
        name="ada",
    )(c_all, w_ada, b_ada)


def _tie(value, zeros):
    rb = value.shape[0] // TIE_PARTS
    packing = 4 // value.dtype.itemsize
    reps = (rb // (SUBLANES * packing), value.shape[1] // LANES)

    def add(block, zero):
        if zero is None:
            return block
        zero = jnp.concatenate([zero] * packing, axis=0).astype(value.dtype)
        return block + jnp.tile(zero, reps)

    return jnp.concatenate([add(value[k * rb:(k + 1) * rb], z) for k, z in enumerate(zeros)], axis=0)


def _swiglu(xm, wgu_ref, wd_ref, hid_ref, between=None):
    for j in range(N_FF_CHUNKS):
        lo = j * FF_CHUNK
        zeros = None if between is None else between()
        gate = _dot(xm, wgu_ref[:, lo:lo + FF_CHUNK])
        up = _dot(xm, wgu_ref[:, D_FF + lo:D_FF + lo + FF_CHUNK])
        hidden = gate * jax.nn.sigmoid(gate) * up
        if zeros is not None:
            hidden = _tie(hidden, zeros)
        hid_ref[:, lo:lo + FF_CHUNK] = hidden.astype(BF16)
    return _dot(hid_ref[...], wd_ref[...])


def _ffn1_inproj(x, ada, g1_ref, wgu_ref, wd_ref, g2_ref, win_ref, bin_ref, wg_ref, bg_ref, hid_ref,
                 store_z, between=None):
    bb, tt, _ = x.shape
    n = bb * tt
    xm = _rms(x, g1_ref[...]) * (1.0 + _ada_row(ada, 1)) + _ada_row(ada, 0)
    ff = _swiglu(xm.reshape(n, D_MODEL).astype(BF16), wgu_ref, wd_ref, hid_ref, between)
    x1 = x + 0.5 * _ada_row(ada, 2) * ff.reshape(bb, tt, D_MODEL)
    hm = _rms(x1, g2_ref[...]) * (1.0 + _ada_row(ada, 4)) + _ada_row(ada, 3)
    hm = hm.reshape(n, D_MODEL).astype(BF16)
    for c0 in range(0, D_Z, D_CONV):
        store_z(c0, _dot(hm, win_ref[:, c0:c0 + D_CONV]) + bin_ref[:, c0:c0 + D_CONV])
    return x1, _dot(hm, wg_ref[...]) + bg_ref[...]


def _run(steps):
    for _ in steps:
        pass


def _all_bits(*arrays):
    acc = None
    for a in arrays:
        bits = lax.bitcast_convert_type(a.reshape(-1, a.shape[-1]), jnp.uint32)
        for r in range(0, bits.shape[0], SUBLANES):
            for c in range(0, bits.shape[1], LANES):
                v = bits[r:r + SUBLANES, c:c + LANES]
                acc = v if acc is None else acc | v
    return acc


def _conv_group(u, ubuf, ush, w_ref, cb_ref, lg_ref, lb_ref, store_y, store_tail, *, bt, rt):
    bb, tt, _ = u.shape
    sh_rows = TAIL_PAD - SUBLANES + tt
    ubuf[:, TAIL_PAD:TAIL_PAD + tt, :] = u
    yield None
    for r in range(1, SUBLANES):
        ush[r - 1] = ubuf[:, r:r + sh_rows, :]
        yield None
    for b0 in range(0, bb, bt):
        for r0 in range(0, tt, rt):
            acc = jnp.broadcast_to(cb_ref[...].reshape(1, 1, D_CONV), (bt, rt, D_CONV))
            for j in range(CONV_WIDTH):
                q, r = divmod(TAIL_LO + j, SUBLANES)
                p0 = r0 + q * SUBLANES
                if r == 0:
                    win = ubuf[b0:b0 + bt, p0:p0 + rt, :]
                else:
                    win = ush[r - 1, b0:b0 + bt, p0:p0 + rt, :]
                acc = acc + w_ref[j:j + 1, :].reshape(1, 1, D_CONV) * win
            mu = jnp.mean(acc, axis=-1, keepdims=True)
            xc = acc - mu
            var = jnp.mean(xc * xc, axis=-1, keepdims=True)
            yn = xc * lax.rsqrt(var + EPS) * lg_ref[...].reshape(1, 1, D_CONV) + lb_ref[...].reshape(1, 1, D_CONV)
            out = yn * jax.nn.sigmoid(yn)
            store_y(b0, r0, out)
            yield _all_bits(out)
    tail = ubuf[:, TAIL_LO + tt:TAIL_PAD + tt, :]
    ubuf[:, TAIL_LO:TAIL_PAD, :] = tail
    store_tail(tail)


def _scan_rows(x, op, ident):
    row = lax.broadcasted_iota(jnp.int32, x.shape, 0)
    s = 1
    while s < x.shape[0]:
        x = op(x, jnp.where(row >= s, pltpu.roll(x, s, axis=0), ident))
        s *= 2
    return x


def _mlstm_group(gi, gf, load_z, store_h, mg_ref, cext, mcar):
    row = lax.broadcasted_iota(jnp.int32, (GROUP, GROUP), 0)
    col = lax.broadcasted_iota(jnp.int32, (GROUP, GROUP), 1)
    causal = col <= row
    ones_b = jnp.ones((GROUP, HEAD_DIM), BF16)
    cum = _scan_rows(_log_sigmoid(gf), jnp.add, 0.0)
    gv = gi - cum
    mx = _scan_rows(gv, jnp.maximum, -jnp.inf)
    mprev = mcar[...]
    mm = jnp.maximum(mprev, mx)
    mcol = cum + mm
    mm_last = mm[GROUP - 1:GROUP, :]
    w_k = jnp.exp(gv - mm_last)
    decay = jnp.exp(mprev - mm_last)
    mcar[...] = mcol[GROUP - 1:GROUP, :]
    gv_t = gv.T
    yield _all_bits(gv_t[0:SUBLANES, :])
    for h in range(N_HEADS):
        sl = slice(h * HEAD_DIM, (h + 1) * HEAD_DIM)
        mm_b = jnp.broadcast_to(mm[:, h:h + 1], (GROUP, GROUP))
        mcol_b = jnp.broadcast_to(mcol[:, h:h + 1], (GROUP, GROUP))
        w_intra = jnp.exp(jnp.where(causal, gv_t[h:h + 1, :] - mm_b, -jnp.inf))
        w_inter = jnp.exp(mprev[:, h:h + 1] - mm_b)
        qb = load_z(0, h).astype(BF16)
        kh = load_z(1, h) * (HEAD_DIM ** -0.5)
        v_ext = jnp.concatenate([load_z(2, h).astype(BF16), ones_b], axis=1)
        s = lax.dot_general(qb, kh.astype(BF16), NT_DIMS, preferred_element_type=F32)
        pv = _dot((s * w_intra).astype(BF16), v_ext)
        cprev = cext[h]
        qc = _dot(qb, cprev.astype(BF16))
        num = pv[:, :HEAD_DIM] + w_inter * qc[:, :HEAD_DIM]
        den = pv[:, HEAD_DIM:] + w_inter * qc[:, HEAD_DIM:]
        hh = num / jnp.maximum(jnp.abs(den), jnp.exp(-mcol_b))
        hh = hh * lax.rsqrt(jnp.mean(hh * hh, axis=-1, keepdims=True) + EPS) * mg_ref[:, sl]
        out = jax.nn.sigmoid(load_z(3, h)) * hh
        store_h(h, out)
        kw = (kh * w_k[:, h:h + 1]).astype(BF16)
        dc = lax.dot_general(kw, v_ext, TN_DIMS, preferred_element_type=F32)
        cnew = decay[:, h:h + 1] * cprev + dc
        cext[h] = cnew
        yield _all_bits(out, cnew)


def _piece_feeder(pieces, costs, call_weights):
    point_weights = [w / TIE_PARTS for w in call_weights for _ in range(TIE_PARTS)]
    done = [0, 0]

    def one_point():
        done[0] += 1
        target = sum(costs) * sum(point_weights[:done[0]]) / sum(point_weights)
        bits = None
        while done[1] < len(costs) and sum(costs[:done[1]]) + costs[done[1]] / 2 <= target:
            piece_bits = next(pieces)
            if piece_bits is not None:
                bits = piece_bits if bits is None else bits | piece_bits
            done[1] += 1
        if bits is None:
            return None
        zero_bits = lax.shift_right_logical(lax.shift_right_logical(bits, jnp.uint32(16)), jnp.uint32(16))
        return lax.bitcast_convert_type(zero_bits, F32)

    return lambda: [one_point() for _ in range(TIE_PARTS)]


def _weight_export(k, vmem_refs, out_refs, sems):
    return pltpu.make_async_copy(vmem_refs[k], out_refs[k], sems.at[k])


def _cast_scratch(n_exports):
    return [
        pltpu.VMEM((CAST_SLOTS, CAST_ROWS_WIDE, 2 * D_FF), F32),
        pltpu.VMEM((CAST_SLOTS, CAST_ROWS_NARROW, D_MODEL), F32),
        pltpu.SemaphoreType.DMA((CAST_SLOTS,)),
        pltpu.SemaphoreType.DMA((CAST_SLOTS,)),
        pltpu.SemaphoreType.DMA((n_exports,)),
    ]


def _stream_rows(src, n_chunks, stage, sems, consume):
    depth, chunk_rows, cols = stage.shape

    def load(i):
        return pltpu.make_async_copy(src.at[pl.ds(i * chunk_rows, chunk_rows), pl.ds(0, cols)],
                                     stage.at[i % depth], sems.at[i % depth])

    for i in range(min(depth - 1, n_chunks)):
        load(i).start()
    for i in range(n_chunks):
        if i + depth - 1 < n_chunks:
            load(i + depth - 1).start()
        load(i).wait()
        consume(i, stage[i % depth])


def _weights_to_bf16(wgu_hbm, wd_hbm, third_hbm, wgu_ref, wd_ref, third_ref, third_is_transposed,
                     stage_wide, stage_narrow, sems_wide, sems_narrow):
    wide_rows, narrow_rows = stage_wide.shape[1], stage_narrow.shape[1]

    def put_wgu(i, chunk):
        wgu_ref[i * wide_rows:(i + 1) * wide_rows, :] = chunk.astype(BF16)

    def put_wd(i, chunk):
        wd_ref[i * narrow_rows:(i + 1) * narrow_rows, :] = chunk.astype(BF16)

    def put_third(i, chunk):
        if third_is_transposed:
            third_ref[:, i * narrow_rows:(i + 1) * narrow_rows] = chunk.T.astype(BF16)
        else:
            third_ref[i * narrow_rows:(i + 1) * narrow_rows, :] = chunk.astype(BF16)

    third_rows = third_ref.shape[1] if third_is_transposed else third_ref.shape[0]
    _stream_rows(wgu_hbm, wgu_ref.shape[0] // wide_rows, stage_wide, sems_wide, put_wgu)
    _stream_rows(wd_hbm, wd_ref.shape[0] // narrow_rows, stage_narrow, sems_narrow, put_wd)
    _stream_rows(third_hbm, third_rows // narrow_rows, stage_narrow, sems_narrow, put_third)


def _gate_weights(win_hbm, wg_ref, stage_narrow, sems_narrow):
    rows = stage_narrow.shape[1]
    tail = pltpu.make_async_copy(win_hbm.at[pl.ds(D_IN - rows, rows), :], stage_narrow.at[0], sems_narrow.at[0])
    tail.start()
    tail.wait()
    cols = stage_narrow[0].T
    lane = lax.broadcasted_iota(jnp.int32, cols.shape, 1)
    i_part = jnp.where(lane < N_HEADS, pltpu.roll(cols, 2 * N_HEADS, axis=1), 0.0)
    f_part = jnp.where(lane < N_HEADS, pltpu.roll(cols, N_HEADS, axis=1), 0.0)
    wg_ref[...] = jnp.concatenate([i_part, f_part], axis=1).astype(BF16)


def _front_kernel(x_ref, ada_ref, wgu_hbm, wd_hbm, win_hbm, g1_ref, g2_ref, bin_ref, bg_ref, mg_ref,
                  x1_ref, u_ref, h_ref, c_ref, n_ref, m_ref, wgu_out, wd_out, win_out, wg_out,
                  hid_ref, zbuf, gbuf, cext, mcar, wgu_ref, wd_ref, win_ref, wg_ref,
                  stage_wide, stage_narrow, sems_wide, sems_narrow, out_sems, *, tt, nt, steps):
    s = pl.program_id(0)
    tl = lax.rem(jnp.maximum(s - 1, 0), nt)
    resident = (wgu_ref, wd_ref, win_ref, wg_ref)
    exported = (wgu_out, wd_out, win_out, wg_out)

    @pl.when(s == 0)
    def _():
        _weights_to_bf16(wgu_hbm, wd_hbm, win_hbm, wgu_ref, wd_ref, win_ref, True,
                         stage_wide, stage_narrow, sems_wide, sems_narrow)
        _gate_weights(win_hbm, wg_ref, stage_narrow, sems_narrow)
        for k in range(len(resident)):
            _weight_export(k, resident, exported, out_sems).start()
        zbuf[...] = jnp.zeros(zbuf.shape, F32)
        gbuf[...] = jnp.zeros(gbuf.shape, F32)

    @pl.when(s == steps)
    def _():
        for k in range(len(resident)):
            _weight_export(k, resident, exported, out_sems).wait()

    @pl.when(tl == 0)
    def _():
        cext[...] = jnp.zeros(cext.shape, F32)
        mcar[...] = jnp.zeros(mcar.shape, F32)

    def mlstm_pieces(g):
        rs = slice(g * GROUP, (g + 1) * GROUP)

        def load_z(which, h):
            c0 = which * D_MLSTM + h * HEAD_DIM
            return zbuf[rs, c0:c0 + HEAD_DIM]

        def store_h(h, val):
            h_ref[0, rs, h * HEAD_DIM:(h + 1) * HEAD_DIM] = val.astype(BF16)

        return _mlstm_group(gbuf[rs, 0:LANES], gbuf[rs, LANES:D_GATE], load_z, store_h, mg_ref, cext, mcar)

    def mixer_pieces():
        for g in range(tt // GROUP):
            yield from mlstm_pieces(g)

    @pl.when(s < steps)
    def _():
        pieces = mixer_pieces()
        costs = ([10] + [5] * N_HEADS) * (tt // GROUP)
        glu_in = []

        def store_z(c0, zc):
            if c0 < Q_OFF:
                glu_in.append(zc)
                if len(glu_in) == 2:
                    u_ref[0] = glu_in[0] * jax.nn.sigmoid(glu_in[1])
            else:
                if c0 == Q_OFF:
                    _run(pieces)
                zbuf[:, c0 - Q_OFF:c0 - Q_OFF + D_CONV] = zc

        x1, gates = _ffn1_inproj(x_ref[...], ada_ref[pl.ds(s // nt, 1), :], g1_ref, wgu_ref, wd_ref, g2_ref,
                                 win_ref, bin_ref, wg_ref, bg_ref, hid_ref, store_z,
                                 _piece_feeder(pieces, costs, FRONT_TIE_WEIGHTS))
        x1_ref[...] = x1
        gbuf[...] = gates

    @pl.when(s == steps)
    def _():
        _run(mixer_pieces())

    @pl.when(tl == nt - 1)
    def _():
        for h in range(N_HEADS):
            c_ref[0, h] = cext[h, :, :HEAD_DIM]
            n_ref[0, h:h + 1, :] = cext[h, :, HEAD_DIM:].T[0:1, :]
        m_ref[0] = mcar[...]


def _front(x, ada, p, *, tt):
    bsz, seq, _ = x.shape
    nt = seq // tt
    steps = bsz * nt
    cur = lambda s: jnp.minimum(s, steps - 1)
    lag = lambda s: jnp.maximum(s - 1, 0)
    big = [p["wgu1"], p["wd1"], p["win"]]
    big_shapes = [(D_MODEL, 2 * D_FF), (D_FF, D_MODEL), (D_MODEL, D_Z), (D_MODEL, D_GATE)]
    small = [p["g1"], p["g2"], p["b_main"], p["b_gate"], p["mn_g"]]
    any_spec = pl.BlockSpec(memory_space=pl.ANY)
    return pl.pallas_call(
        functools.partial(_front_kernel, tt=tt, nt=nt, steps=steps),
        grid=(steps + 1,),
        in_specs=[
            pl.BlockSpec((1, tt, D_MODEL), lambda s: (cur(s) // nt, cur(s) % nt, 0)),
            _const_spec(ada.shape),
        ] + [any_spec] * len(big) + [_const_spec(w.shape) for w in small],
        out_specs=[
            pl.BlockSpec((1, tt, D_MODEL), lambda s: (cur(s) // nt, cur(s) % nt, 0)),
            pl.BlockSpec((1, tt, D_CONV), lambda s: (cur(s) // nt, cur(s) % nt, 0)),
            pl.BlockSpec((1, tt, D_MLSTM), lambda s: (lag(s) // nt, lag(s) % nt, 0)),
            pl.BlockSpec((1, N_HEADS, HEAD_DIM, HEAD_DIM), lambda s: (lag(s) // nt, 0, 0, 0)),
            pl.BlockSpec((1, N_HEADS, HEAD_DIM), lambda s: (lag(s) // nt, 0, 0)),
            pl.BlockSpec((1, 1, LANES), lambda s: (lag(s) // nt, 0, 0)),
        ] + [any_spec] * len(big_shapes),
        out_shape=[
            jax.ShapeDtypeStruct(x.shape, F32),
            jax.ShapeDtypeStruct((bsz, seq, D_CONV), F32),
            jax.ShapeDtypeStruct((bsz, seq, D_MLSTM), BF16),
            jax.ShapeDtypeStruct((bsz, N_HEADS, HEAD_DIM, HEAD_DIM), F32),
            jax.ShapeDtypeStruct((bsz, N_HEADS, HEAD_DIM), F32),
            jax.ShapeDtypeStruct((bsz, 1, LANES), F32),
        ] + [jax.ShapeDtypeStruct(shape, BF16) for shape in big_shapes],
        scratch_shapes=[
            pltpu.VMEM((tt, D_FF), BF16),
            pltpu.VMEM((tt, 4 * D_MLSTM), F32),
            pltpu.VMEM((tt, D_GATE), F32),
            pltpu.VMEM((N_HEADS, HEAD_DIM, 2 * HEAD_DIM), F32),
            pltpu.VMEM((1, LANES), F32),
        ] + [pltpu.VMEM(shape, BF16) for shape in big_shapes] + _cast_scratch(len(big_shapes)),
        compiler_params=_params(1),
        name="front",
    )(x, ada, *big, *small)


def _back_kernel(x_ref, ada_ref, u_ref, h_ref, wgu_hbm, wd_hbm, wout_hbm, g3_ref, gf_ref,
                 cw_ref, cb_ref, lg_ref, lb_ref,
                 o_ref, cs_ref, wgu_out, wd_out, wout_out,
                 hid_ref, ubuf, ush, ycur, ynext, wgu_ref, wd_ref, wout_ref,
                 stage_wide, stage_narrow, sems_wide, sems_narrow, out_sems, *, tt, nt, rt, steps):
    s = pl.program_id(0)
    tc = lax.rem(jnp.minimum(s, steps - 1), nt)
    resident = (wgu_ref, wd_ref, wout_ref)
    exported = (wgu_out, wd_out, wout_out)

    @pl.when(s == 0)
    def _():
        _weights_to_bf16(wgu_hbm, wd_hbm, wout_hbm, wgu_ref, wd_ref, wout_ref, False,
                         stage_wide, stage_narrow, sems_wide, sems_narrow)
        for k in range(len(resident)):
            _weight_export(k, resident, exported, out_sems).start()
        ycur[...] = jnp.zeros(ycur.shape, BF16)

    @pl.when(s == steps)
    def _():
        for k in range(len(resident)):
            _weight_export(k, resident, exported, out_sems).wait()

    @pl.when(tc == 0)
    def _():
        ubuf[:, 0:TAIL_PAD, :] = jnp.zeros((1, TAIL_PAD, D_CONV), F32)

    def store_y(b0, r0, val):
        ynext[r0:r0 + rt, :] = val[0].astype(BF16)

    def matmul_half(pieces, between):
        x = x_ref[...]
        ada = ada_ref[pl.ds(jnp.maximum(s - 1, 0) // nt, 1), :]
        mix = _dot(ycur[...], wout_ref[:D_CONV, :]) + _dot(h_ref[0], wout_ref[D_CONV:, :])
        if between is not None:
            mix = _tie(mix, between())
        x2 = x + _ada_row(ada, 5) * mix.reshape(1, tt, D_MODEL)
        xm = _rms(x2, g3_ref[...]) * (1.0 + _ada_row(ada, 7)) + _ada_row(ada, 6)
        ff = _swiglu(xm.reshape(tt, D_MODEL).astype(BF16), wgu_ref, wd_ref, hid_ref, between)
        _run(pieces)
        x3 = x2 + 0.5 * _ada_row(ada, 8) * ff.reshape(1, tt, D_MODEL)
        o_ref[...] = _rms(x3, gf_ref[...])

    @pl.when(s < steps)
    def _():
        pieces = _conv_group(u_ref[...], ubuf, ush, cw_ref, cb_ref, lg_ref, lb_ref, store_y,
                             lambda tail: None, bt=1, rt=rt)
        costs = [2] + [5] * (SUBLANES - 1) + [7 * rt // (2 * SUBLANES)] * (tt // rt)
        matmul_half(pieces, _piece_feeder(pieces, costs, [1.0] * (1 + N_FF_CHUNKS)))
        ycur[...] = ynext[...]

    @pl.when(s == steps)
    def _():
        matmul_half((), None)

    @pl.when(tc == nt - 1)
    def _():
        cs_ref[...] = ubuf[:, TAIL_LO:TAIL_PAD, :]


def _back(x1, ada, u, h, p, *, tt, rt):
    bsz, seq, _ = x1.shape
    nt = seq // tt
    steps = bsz * nt
    cur = lambda s: jnp.minimum(s, steps - 1)
    lag = lambda s: jnp.maximum(s - 1, 0)
    big = [p["wgu2"], p["wd2"], p["wout"]]
    big_shapes = [(D_MODEL, 2 * D_FF), (D_FF, D_MODEL), (D_MODEL, D_MODEL)]
    small = [p["g3"], p["gf"], p["conv_w"], p["conv_b"], p["cn_g"], p["cn_b"]]
    any_spec = pl.BlockSpec(memory_space=pl.ANY)
    return pl.pallas_call(
        functools.partial(_back_kernel, tt=tt, nt=nt, rt=rt, steps=steps),
        grid=(steps + 1,),
        in_specs=[
            pl.BlockSpec((1, tt, D_MODEL), lambda s: (lag(s) // nt, lag(s) % nt, 0)),
            _const_spec(ada.shape),
            pl.BlockSpec((1, tt, D_CONV), lambda s: (cur(s) // nt, cur(s) % nt, 0)),
            pl.BlockSpec((1, tt, D_MLSTM), lambda s: (lag(s) // nt, lag(s) % nt, 0)),
        ] + [any_spec] * len(big) + [_const_spec(w.shape) for w in small],
        out_specs=[
            pl.BlockSpec((1, tt, D_MODEL), lambda s: (lag(s) // nt, lag(s) % nt, 0)),
            pl.BlockSpec((1, CONV_TAIL, D_CONV), lambda s: (cur(s) // nt, 0, 0)),
        ] + [any_spec] * len(big_shapes),
        out_shape=[
            jax.ShapeDtypeStruct(x1.shape, F32),
            jax.ShapeDtypeStruct((bsz, CONV_TAIL, D_CONV), F32),
        ] + [jax.ShapeDtypeStruct(shape, BF16) for shape in big_shapes],
        scratch_shapes=[
            pltpu.VMEM((tt, D_FF), BF16),
            pltpu.VMEM((1, TAIL_PAD + tt, D_CONV), F32),
            pltpu.VMEM((SUBLANES - 1, 1, TAIL_PAD - SUBLANES + tt, D_CONV), F32),
            pltpu.VMEM((tt, D_CONV), BF16),
            pltpu.VMEM((tt, D_CONV), BF16),
        ] + [pltpu.VMEM(shape, BF16) for shape in big_shapes] + _cast_scratch(len(big_shapes)),
        compiler_params=_params(1),
        name="back",
    )(x1, ada, u, h, *big, *small)


def _ffn_a_kernel(x_ref, ada_ref, g1_ref, wgu_ref, wd_ref, g2_ref, win_ref, bin_ref, wg_ref, bg_ref,
                  x1_ref, z_ref, gate_ref, hid_ref):
    def store_z(c0, zc):
        z_ref[:, c0:c0 + D_CONV] = zc

    x1, gates = _ffn1_inproj(x_ref[...], ada_ref[...], g1_ref, wgu_ref, wd_ref, g2_ref,
                             win_ref, bin_ref, wg_ref, bg_ref, hid_ref, store_z)
    x1_ref[...] = x1
    gate_ref[...] = gates


def _ffn_a(x, ada, p, *, bb):
    bsz, seq, _ = x.shape
    n = bb * seq
    weights = [p["g1"], p["wgu1"], p["wd1"], p["g2"], p["win"], p["b_main"], p["wg"], p["b_gate"]]
    return pl.pallas_call(
        _ffn_a_kernel,
        grid=(bsz // bb,),
        in_specs=[
            pl.BlockSpec((bb, seq, D_MODEL), lambda b: (b, 0, 0)),
            pl.BlockSpec((bb, N_ADA * D_MODEL), lambda b: (b, 0)),
        ] + [_const_spec(w.shape) for w in weights],
        out_specs=[
            pl.BlockSpec((bb, seq, D_MODEL), lambda b: (b, 0, 0)),
            pl.BlockSpec((n, D_Z), lambda b: (b, 0)),
            pl.BlockSpec((n, D_GATE), lambda b: (b, 0)),
        ],
        out_shape=[
            jax.ShapeDtypeStruct(x.shape, F32),
            jax.ShapeDtypeStruct((bsz * seq, D_Z), F32),
            jax.ShapeDtypeStruct((bsz * seq, D_GATE), F32),
        ],
        scratch_shapes=[pltpu.VMEM((n, D_FF), BF16)],
        compiler_params=_params(1),
        name="ffn_a",
    )(x, ada, *weights)


def _conv_kernel(a_ref, b_ref, cs_ref, w_ref, cb_ref, lg_ref, lb_ref, y_ref, so_ref, full):
    seq = a_ref.shape[1]
    full[0:CONV_TAIL] = cs_ref[...]
    for t in range(seq):
        full[CONV_TAIL + t] = a_ref[:, t, :] * jax.nn.sigmoid(b_ref[:, t, :])
    for t in range(seq):
        acc = jnp.broadcast_to(cb_ref[...], full.shape[1:])
        for j in range(CONV_WIDTH):
            acc = acc + w_ref[j:j + 1, :] * full[t + j]
        mu = jnp.mean(acc, axis=-1, keepdims=True)
        xc = acc - mu
        var = jnp.mean(xc * xc, axis=-1, keepdims=True)
        yn = xc * lax.rsqrt(var + EPS) * lg_ref[...] + lb_ref[...]
        y_ref[:, t, :] = yn * jax.nn.sigmoid(yn)
    so_ref[...] = full[seq:seq + CONV_TAIL]


def _conv(z3, state_t, p, *, bb):
    bsz, seq, _ = z3.shape
    weights = [p["conv_w"], p["conv_b"], p["cn_g"], p["cn_b"]]
    state_spec = pl.BlockSpec((CONV_TAIL, bb, D_CONV), lambda b: (0, b, 0))
    return pl.pallas_call(
        _conv_kernel,
        grid=(bsz // bb,),
        in_specs=[
            pl.BlockSpec((bb, seq, D_CONV), lambda b: (b, 0, 0)),
            pl.BlockSpec((bb, seq, D_CONV), lambda b: (b, 0, 1)),
            state_spec,
        ] + [_const_spec(w.shape) for w in weights],
        out_specs=[pl.BlockSpec((bb, seq, D_CONV), lambda b: (b, 0, 0)), state_spec],
        out_shape=[
            jax.ShapeDtypeStruct((bsz, seq, D_CONV), F32),
            jax.ShapeDtypeStruct((CONV_TAIL, bsz, D_CONV), F32),
        ],
        scratch_shapes=[pltpu.VMEM((CONV_TAIL + seq, bb, D_CONV), F32)],
        compiler_params=_params(1),
        name="conv",
    )(z3, z3, state_t, *weights)


def _mlstm_sample_kernel(q_ref, k_ref, v_ref, o_ref, gt_ref, mg_ref, c0_ref, n0_ref, m0_ref,
                         h_ref, c_ref, n_ref, m_ref, *, nseq, ls):
    row = lax.broadcasted_iota(jnp.int32, (GROUP, GROUP), 0)
    col = lax.broadcasted_iota(jnp.int32, (GROUP, GROUP), 1)
    shift = ls.bit_length() - 1
    seg_r = lax.shift_right_logical(row, shift)
    seg_c = lax.shift_right_logical(col, shift)
    causal = jnp.logical_and(col <= row, seg_r == seg_c)
    segtril = causal.astype(F32)
    lastsel = (col == seg_r * ls + (ls - 1)).astype(F32)

    gates = gt_ref[...].reshape(GROUP, D_GATE)
    gi = gates[:, :LANES]
    cum = _dot_exact(segtril, _log_sigmoid(gates[:, LANES:]))
    cumlast = _dot_exact(lastsel, cum)
    cum_t = cum.T
    gi_t = gi.T
    mprev = m0_ref[...].reshape(GROUP, LANES)
    qg = q_ref[...].reshape(GROUP, D_MLSTM)
    kg = k_ref[...].reshape(GROUP, D_MLSTM) * (HEAD_DIM ** -0.5)
    vg = v_ref[...].reshape(GROUP, D_MLSTM)
    og = o_ref[...].reshape(GROUP, D_MLSTM)

    m_all = jnp.zeros((GROUP, LANES), F32)
    lane = lax.broadcasted_iota(jnp.int32, (GROUP, LANES), 1)
    hs = []
    for h in range(N_HEADS):
        sl = slice(h * HEAD_DIM, (h + 1) * HEAD_DIM)
        cum_col = cum[:, h:h + 1]
        dmat = jnp.where(causal, cum_col - cum_t[h:h + 1, :] + gi_t[h:h + 1, :], -jnp.inf)
        inter = cum_col + mprev[:, h:h + 1]
        m_col = jnp.maximum(inter, jnp.max(dmat, axis=-1, keepdims=True))
        w_intra = jnp.exp(dmat - m_col)
        w_inter = jnp.exp(inter - m_col)
        qh = qg[:, sl]
        qb = qh.astype(BF16)
        s = lax.dot_general(qb, kg[:, sl].astype(BF16), NT_DIMS, preferred_element_type=F32)
        p = s * w_intra
        num = _dot(p.astype(BF16), vg[:, sl].astype(BF16))
        den = jnp.sum(p, axis=-1, keepdims=True)
        qc = jnp.zeros((GROUP, HEAD_DIM), F32)
        qn = jnp.zeros((GROUP, 1), F32)
        for sq in range(nseq):
            qc = jnp.where(seg_r == sq, _dot(qb, c0_ref[sq, h].astype(BF16)), qc)
            qn = jnp.where(seg_r[:, 0:1] == sq,
                           jnp.sum(qh * n0_ref[sq, h:h + 1, :], axis=-1, keepdims=True), qn)
        num = num + w_inter * qc
        den = den + w_inter * qn
        hh = num / jnp.maximum(jnp.abs(den), jnp.exp(-m_col))
        hh = hh * lax.rsqrt(jnp.mean(hh * hh, axis=-1, keepdims=True) + EPS) * mg_ref[:, sl]
        hs.append(jax.nn.sigmoid(og[:, sl]) * hh)
        m_all = jnp.where(lane == h, m_col, m_all)

    h_ref[...] = jnp.concatenate(hs, axis=-1).reshape(h_ref.shape)
    m_ref[...] = m_all.reshape(m_ref.shape)

    mnew = _dot_exact(lastsel, m_all)
    for h in range(N_HEADS):
        sl = slice(h * HEAD_DIM, (h + 1) * HEAD_DIM)
        last_col = cumlast[:, h:h + 1]
        mnew_col = mnew[:, h:h + 1]
        w_k = jnp.exp(last_col - cum[:, h:h + 1] + gi[:, h:h + 1] - mnew_col)
        decay = jnp.exp(last_col + mprev[:, h:h + 1] - mnew_col)
        kw = kg[:, sl] * w_k
        kwb = kw.astype(BF16)
        vh = vg[:, sl]
        for sq in range(nseq):
            vs = jnp.where(seg_r == sq, vh, 0.0)
            kws = jnp.where(seg_r == sq, kw, 0.0)
            dc = lax.dot_general(kwb, vs.astype(BF16), TN_DIMS, preferred_element_type=F32)
            dn = jnp.sum(kws, axis=0, keepdims=True)
            dec = decay[sq * ls:sq * ls + 1, :]
            c_ref[sq, h] = dec * c0_ref[sq, h] + dc
            n_ref[sq, h:h + 1, :] = dec * n0_ref[sq, h:h + 1, :] + dn


def _mlstm_sample(z3, gates3, mg, c0, n0, m_tok, *, bb):
    bsz, seq, _ = z3.shape
    assert bb * seq == GROUP
    zspec = lambda k: pl.BlockSpec((bb, seq, D_MLSTM), lambda b: (b, 0, k))
    c_spec = pl.BlockSpec((bb, N_HEADS, HEAD_DIM, HEAD_DIM), lambda b: (b, 0, 0, 0))
    n_spec = pl.BlockSpec((bb, N_HEADS, HEAD_DIM), lambda b: (b, 0, 0))
    m_spec = pl.BlockSpec((bb, seq, LANES), lambda b: (b, 0, 0))
    return pl.pallas_call(
        functools.partial(_mlstm_sample_kernel, nseq=bb, ls=seq),
        grid=(bsz // bb,),
        in_specs=[zspec(2), zspec(3), zspec(4), zspec(5),
                  pl.BlockSpec((bb, seq, D_GATE), lambda b: (b, 0, 0)),
                  _const_spec(mg.shape),
                  c_spec, n_spec, m_spec],
        out_specs=[pl.BlockSpec((bb, seq, D_MLSTM), lambda b: (b, 0, 0)), c_spec, n_spec, m_spec],
        out_shape=[
            jax.ShapeDtypeStruct((bsz, seq, D_MLSTM), F32),
            jax.ShapeDtypeStruct((bsz, N_HEADS, HEAD_DIM, HEAD_DIM), F32),
            jax.ShapeDtypeStruct((bsz, N_HEADS, HEAD_DIM), F32),
            jax.ShapeDtypeStruct((bsz, seq, LANES), F32),
        ],
        compiler_params=_params(1),
        name="mlstm_sample",
    )(z3, z3, z3, z3, gates3, mg, c0, n0, m_tok)


def _ffn_b_kernel(x_ref, ada_ref, y_ref, h_ref, wout_ref, g3_ref, wgu_ref, wd_ref, gf_ref, o_ref, hid_ref):
    x = x_ref[...]
    bb, tt, _ = x.shape
    n = bb * tt
    ada = ada_ref[...]
    mix = (_dot(y_ref[...].reshape(n, D_CONV).astype(BF16), wout_ref[:D_CONV, :])
           + _dot(h_ref[...].reshape(n, D_MLSTM).astype(BF16), wout_ref[D_CONV:, :]))
    x2 = x + _ada_row(ada, 5) * mix.reshape(bb, tt, D_MODEL)
    xm = _rms(x2, g3_ref[...]) * (1.0 + _ada_row(ada, 7)) + _ada_row(ada, 6)
    ff = _swiglu(xm.reshape(n, D_MODEL).astype(BF16), wgu_ref, wd_ref, hid_ref)
    x3 = x2 + 0.5 * _ada_row(ada, 8) * ff.reshape(bb, tt, D_MODEL)
    o_ref[...] = _rms(x3, gf_ref[...])


def _ffn_b(x, ada, y, h, p, *, bb):
    bsz, seq, _ = x.shape
    tok = lambda w: pl.BlockSpec((bb, seq, w), lambda b: (b, 0, 0))
    weights = [p["wout"], p["g3"], p["wgu2"], p["wd2"], p["gf"]]
    return pl.pallas_call(
        _ffn_b_kernel,
        grid=(bsz // bb,),
        in_specs=[tok(D_MODEL), pl.BlockSpec((bb, N_ADA * D_MODEL), lambda b: (b, 0)), tok(D_CONV), tok(D_MLSTM)]
        + [_const_spec(w.shape) for w in weights],
        out_specs=tok(D_MODEL),
        out_shape=jax.ShapeDtypeStruct(x.shape, F32),
        scratch_shapes=[pltpu.VMEM((bb * seq, D_FF), BF16)],
        compiler_params=_params(1),
        name="ffn_b",
    )(x, ada, y, h, *weights)


def _prompt_trunk(x, ada, p):
    x1, u, h, c_new, n_new, m_row, wgu1, wd1, win, wg = _front(x, ada, p, tt=512)
    out, conv_new, wgu2, wd2, wout = _back(x1, ada, u, h, p, tt=512, rt=32)
    bf16_weights = dict(wgu1=wgu1, wd1=wd1, win=win, wg=wg, wgu2=wgu2, wd2=wd2, wout=wout)
    return (out, conv_new[None], c_new[None], n_new[None], m_row[None, :, 0, :N_HEADS]), bf16_weights


def _sample_trunk(x, ada, states, p):
    bsz, seq, _ = x.shape
    conv_state, c0, n0, m0 = states
    x1, z, gates = _ffn_a(x, ada, p, bb=64)
    z3 = z.reshape(bsz, seq, D_Z)
    gates3 = gates.reshape(bsz, seq, D_GATE)
    y, conv_new_t = _conv(z3, jnp.transpose(conv_state, (1, 0, 2)), p, bb=32)
    conv_new = jnp.transpose(conv_new_t, (1, 0, 2))
    m_tok = jnp.broadcast_to(
        jnp.pad(m0, ((0, 0), (0, LANES - N_HEADS)))[:, None, :], (bsz, seq, LANES))
    h, c_new, n_new, m_tok_new = _mlstm_sample(z3, gates3, p["mn_g"], c0, n0, m_tok, bb=GROUP // seq)
    out = _ffn_b(x1, ada, y, h, p, bb=64)
    return out, conv_new[None], c_new[None], n_new[None], m_tok_new[None, :, seq - 1, :N_HEADS]


def _gate_cols(a):
    pad = ((0, 0), (0, LANES - N_HEADS))
    return jnp.concatenate([jnp.pad(a[:, :N_HEADS], pad), jnp.pad(a[:, N_HEADS:], pad)], axis=1)


def kernel(x_prompt, x_sample, c_prompt, c_sample, state_conv, state_C, state_n, state_m, w_ada, b_ada, norm_ffn1, ffn1_w_gu, ffn1_w_down, norm_mix, w_in, b_in, conv_w, conv_b, conv_norm_g, conv_norm_b, mlstm_norm_g, w_out, norm_ffn2, ffn2_w_gu, ffn2_w_down, norm_final):
    assert w_ada.shape[0] == 1, "single layer"
    w_in_t = jnp.transpose(w_in[0])
    p = {
        "g1": norm_ffn1, "g2": norm_mix, "g3": norm_ffn2, "gf": norm_final[None],
        "wgu1": ffn1_w_gu[0], "wd1": ffn1_w_down[0], "wgu2": ffn2_w_gu[0], "wd2": ffn2_w_down[0],
        "win": w_in_t, "wout": w_out[0],
        "b_main": b_in,
        "b_gate": _gate_cols(b_in[:, D_Z:]),
        "conv_w": jnp.pad(conv_w[0], ((0, TAIL_PAD - CONV_WIDTH), (0, 0))),
        "conv_b": conv_b, "cn_g": conv_norm_g, "cn_b": conv_norm_b, "mn_g": mlstm_norm_g,
    }
    bs = x_sample.shape[0]
    ada = _ada(jnp.concatenate([c_sample, c_prompt], axis=0), w_ada[0], b_ada)
    (yp, conv_p, c_p, n_p, m_p), bf16_weights = _prompt_trunk(x_prompt, ada[bs:], p)
    ys, conv_s, c_s, n_s, m_s = _sample_trunk(
        x_sample, ada, (state_conv[0], state_C[0], state_n[0], state_m[0]), {**p, **bf16_weights})
    return (yp, ys, conv_p, c_p, n_p, m_p, conv_s, c_s, n_s, m_s)
```

```python
import functools

import jax
import jax.numpy as jnp
from jax import lax
from jax.experimental import pallas as pl
from jax.experimental.pallas import tpu as pltpu

F32 = jnp.float32
BF16 = jnp.bfloat16

D_MODEL = 1024
D_CONV = 512
D_MLSTM = 512
N_HEADS = 4
HEAD_DIM = 128
CONV_WIDTH = 31
CONV_TAIL = CONV_WIDTH - 1
D_FF = 2816
N_ADA = 9
EPS = 1e-6
D_Z = 2 * D_CONV + 4 * D_MLSTM
D_IN = D_Z + 2 * N_HEADS
Q_OFF = 2 * D_CONV
LANES = 128
SUBLANES = 8
D_GATE = 2 * LANES
FF_CHUNK = 256
N_FF_CHUNKS = D_FF // FF_CHUNK
GROUP = 128
TAIL_PAD = 32
TAIL_LO = TAIL_PAD - CONV_TAIL
TIE_PARTS = 2
FRONT_TIE_WEIGHTS = [0.0, 0.5] + [1.0] * (N_FF_CHUNKS - 2) + [0.7, 0.7, 0.7]
CAST_SLOTS = 4
CAST_ROWS_WIDE = 64
CAST_ROWS_NARROW = 128
VMEM_LIMIT = 57 * 1024 * 1024
NT_DIMS = (((1,), (1,)), ((), ()))
TN_DIMS = (((0,), (0,)), ((), ()))


def _dot(a, b):
    return jnp.dot(a, b, preferred_element_type=F32)


def _dot_exact(a, b):
    return jnp.dot(a, b, preferred_element_type=F32, precision=lax.Precision.HIGHEST)


def _rms(x, g):
    ms = jnp.mean(x * x, axis=-1, keepdims=True)
    return x * lax.rsqrt(ms + EPS) * g


def _ada_row(ada, k):
    return ada[:, k * D_MODEL:(k + 1) * D_MODEL][:, None, :]


def _log_sigmoid(x):
    return jnp.minimum(x, 0.0) - jnp.log(1.0 + jnp.exp(-jnp.abs(x)))


def _const_spec(shape):
    nd = len(shape)
    return pl.BlockSpec(shape, lambda *_: (0,) * nd, pipeline_mode=pl.Buffered(1))


def _params(n_grid):
    return pltpu.CompilerParams(dimension_semantics=("arbitrary",) * n_grid, vmem_limit_bytes=VMEM_LIMIT)


def _ada_kernel(c_ref, w_ref, b_ref, o_ref):
    c = c_ref[...]
    s = (c * jax.nn.sigmoid(c)).astype(BF16)
    o_ref[...] = _dot(s, w_ref[...].astype(BF16)) + b_ref[...]


def _ada(c_all, w_ada, b_ada):
    n = c_all.shape[0]
    tile = D_MODEL
    return pl.pallas_call(
        _ada_kernel,
        grid=(N_ADA,),
        in_specs=[
            pl.BlockSpec((n, D_MODEL), lambda j: (0, 0)),
            pl.BlockSpec((D_MODEL, tile), lambda j: (0, j)),
            pl.BlockSpec((1, tile), lambda j: (0, j)),
        ],
        out_specs=pl.BlockSpec((n, tile), lambda j: (0, j)),
        out_shape=jax.ShapeDtypeStruct((n, N_ADA * D_MODEL), F32),
        compiler_params=_params(1),
        name="ada",
    )(c_all, w_ada, b_ada)


def _tie(value, zeros):
    rb = value.shape[0] // TIE_PARTS
    packing = 4 // value.dtype.itemsize
    reps = (rb // (SUBLANES * packing), value.shape[1] // LANES)

    def add(block, zero):
        if zero is None:
            return block
        zero = jnp.concatenate([zero] * packing, axis=0).astype(value.dtype)
        return block + jnp.tile(zero, reps)

    return jnp.concatenate([add(value[k * rb:(k + 1) * rb], z) for k, z in enumerate(zeros)], axis=0)


def _swiglu(xm, wgu_ref, wd_ref, hid_ref, between=None):
    for j in range(N_FF_CHUNKS):
        lo = j * FF_CHUNK
        zeros = None if between is None else between()
        gate = _dot(xm, wgu_ref[:, lo:lo + FF_CHUNK])
        up = _dot(xm, wgu_ref[:, D_FF + lo:D_FF + lo + FF_CHUNK])
        hidden = gate * jax.nn.sigmoid(gate) * up
        if zeros is not None:
            hidden = _tie(hidden, zeros)
        hid_ref[:, lo:lo + FF_CHUNK] = hidden.astype(BF16)
    return _dot(hid_ref[...], wd_ref[...])


def _ffn1_inproj(x, ada, g1_ref, wgu_ref, wd_ref, g2_ref, win_ref, bin_ref, wg_ref, bg_ref, hid_ref,
                 store_z, between=None):
    bb, tt, _ = x.shape
    n = bb * tt
    xm = _rms(x, g1_ref[...]) * (1.0 + _ada_row(ada, 1)) + _ada_row(ada, 0)
    ff = _swiglu(xm.reshape(n, D_MODEL).astype(BF16), wgu_ref, wd_ref, hid_ref, between)
    x1 = x + 0.5 * _ada_row(ada, 2) * ff.reshape(bb, tt, D_MODEL)
    hm = _rms(x1, g2_ref[...]) * (1.0 + _ada_row(ada, 4)) + _ada_row(ada, 3)
    hm = hm.reshape(n, D_MODEL).astype(BF16)
    gates = None
    for c0 in range(0, D_Z, D_CONV):
        zc = _dot(hm, win_ref[:, c0:c0 + D_CONV]) + bin_ref[:, c0:c0 + D_CONV]
        if c0 < Q_OFF and between is not None:
            zc = _tie(zc, between())
        if c0 == Q_OFF:
            gates = _dot(hm, wg_ref[...]) + bg_ref[...]
            if between is not None:
                gates = _tie(gates, between())
        store_z(c0, zc)
    return x1, gates


def _run(steps):
    for _ in steps:
        pass


def _all_bits(*arrays):
    acc = None
    for a in arrays:
        bits = lax.bitcast_convert_type(a.reshape(-1, a.shape[-1]), jnp.uint32)
        for r in range(0, bits.shape[0], SUBLANES):
            for c in range(0, bits.shape[1], LANES):
                v = bits[r:r + SUBLANES, c:c + LANES]
                acc = v if acc is None else acc | v
    return acc


def _conv_group(u, ubuf, ush, w_ref, cb_ref, lg_ref, lb_ref, store_y, store_tail, *, bt, rt):
    bb, tt, _ = u.shape
    sh_rows = TAIL_PAD - SUBLANES + tt
    ubuf[:, TAIL_PAD:TAIL_PAD + tt, :] = u
    yield None
    for r in range(1, SUBLANES):
        ush[r - 1] = ubuf[:, r:r + sh_rows, :]
        yield None
    for b0 in range(0, bb, bt):
        for r0 in range(0, tt, rt):
            acc = jnp.broadcast_to(cb_ref[...].reshape(1, 1, D_CONV), (bt, rt, D_CONV))
            for j in range(CONV_WIDTH):
                q, r = divmod(TAIL_LO + j, SUBLANES)
                p0 = r0 + q * SUBLANES
                if r == 0:
                    win = ubuf[b0:b0 + bt, p0:p0 + rt, :]
                else:
                    win = ush[r - 1, b0:b0 + bt, p0:p0 + rt, :]
                acc = acc + w_ref[j:j + 1, :].reshape(1, 1, D_CONV) * win
            mu = jnp.mean(acc, axis=-1, keepdims=True)
            xc = acc - mu
            var = jnp.mean(xc * xc, axis=-1, keepdims=True)
            yn = xc * lax.rsqrt(var + EPS) * lg_ref[...].reshape(1, 1, D_CONV) + lb_ref[...].reshape(1, 1, D_CONV)
            out = yn * jax.nn.sigmoid(yn)
            store_y(b0, r0, out)
            yield _all_bits(out)
    tail = ubuf[:, TAIL_LO + tt:TAIL_PAD + tt, :]
    ubuf[:, TAIL_LO:TAIL_PAD, :] = tail
    store_tail(tail)


def _scan_rows(x, op, ident):
    row = lax.broadcasted_iota(jnp.int32, x.shape, 0)
    s = 1
    while s < x.shape[0]:
        x = op(x, jnp.where(row >= s, pltpu.roll(x, s, axis=0), ident))
        s *= 2
    return x


def _mlstm_group(gi, gf, load_z, store_h, mg_ref, cext, mcar):
    row = lax.broadcasted_iota(jnp.int32, (GROUP, GROUP), 0)
    col = lax.broadcasted_iota(jnp.int32, (GROUP, GROUP), 1)
    causal = col <= row
    ones_b = jnp.ones((GROUP, HEAD_DIM), BF16)
    cum = _scan_rows(_log_sigmoid(gf), jnp.add, 0.0)
    gv = gi - cum
    mx = _scan_rows(gv, jnp.maximum, -jnp.inf)
    mprev = mcar[...]
    mm = jnp.maximum(mprev, mx)
    mcol = cum + mm
    mm_last = mm[GROUP - 1:GROUP, :]
    w_k = jnp.exp(gv - mm_last)
    decay = jnp.exp(mprev - mm_last)
    mcar[...] = mcol[GROUP - 1:GROUP, :]
    gv_t = gv.T
    yield _all_bits(gv_t[0:SUBLANES, :])
    for h in range(N_HEADS):
        sl = slice(h * HEAD_DIM, (h + 1) * HEAD_DIM)
        mm_b = jnp.broadcast_to(mm[:, h:h + 1], (GROUP, GROUP))
        mcol_b = jnp.broadcast_to(mcol[:, h:h + 1], (GROUP, GROUP))
        w_intra = jnp.exp(jnp.where(causal, gv_t[h:h + 1, :] - mm_b, -jnp.inf))
        w_inter = jnp.exp(mprev[:, h:h + 1] - mm_b)
        qb = load_z(0, h).astype(BF16)
        kh = load_z(1, h) * (HEAD_DIM ** -0.5)
        v_ext = jnp.concatenate([load_z(2, h).astype(BF16), ones_b], axis=1)
        s = lax.dot_general(qb, kh.astype(BF16), NT_DIMS, preferred_element_type=F32)
        pv = _dot((s * w_intra).astype(BF16), v_ext)
        cprev = cext[h]
        qc = _dot(qb, cprev.astype(BF16))
        num = pv[:, :HEAD_DIM] + w_inter * qc[:, :HEAD_DIM]
        den = pv[:, HEAD_DIM:] + w_inter * qc[:, HEAD_DIM:]
        hh = num / jnp.maximum(jnp.abs(den), jnp.exp(-mcol_b))
        hh = hh * lax.rsqrt(jnp.mean(hh * hh, axis=-1, keepdims=True) + EPS) * mg_ref[:, sl]
        out = jax.nn.sigmoid(load_z(3, h)) * hh
        store_h(h, out)
        kw = (kh * w_k[:, h:h + 1]).astype(BF16)
        dc = lax.dot_general(kw, v_ext, TN_DIMS, preferred_element_type=F32)
        cnew = decay[:, h:h + 1] * cprev + dc
        cext[h] = cnew
        yield _all_bits(out, cnew)


def _piece_feeder(pieces, costs, call_weights):
    point_weights = [w / TIE_PARTS for w in call_weights for _ in range(TIE_PARTS)]
    done = [0, 0]

    def one_point():
        done[0] += 1
        target = sum(costs) * sum(point_weights[:done[0]]) / sum(point_weights)
        bits = None
        while done[1] < len(costs) and sum(costs[:done[1]]) + costs[done[1]] / 2 <= target:
            piece_bits = next(pieces)
            if piece_bits is not None:
                bits = piece_bits if bits is None else bits | piece_bits
            done[1] += 1
        if bits is None:
            return None
        zero_bits = lax.shift_right_logical(lax.shift_right_logical(bits, jnp.uint32(16)), jnp.uint32(16))
        return lax.bitcast_convert_type(zero_bits, F32)

    return lambda: [one_point() for _ in range(TIE_PARTS)]


def _weight_export(k, vmem_refs, out_refs, sems):
    return pltpu.make_async_copy(vmem_refs[k], out_refs[k], sems.at[k])


def _cast_scratch(n_exports):
    return [
        pltpu.VMEM((CAST_SLOTS, CAST_ROWS_WIDE, 2 * D_FF), F32),
        pltpu.VMEM((CAST_SLOTS, CAST_ROWS_NARROW, D_MODEL), F32),
        pltpu.SemaphoreType.DMA((CAST_SLOTS,)),
        pltpu.SemaphoreType.DMA((CAST_SLOTS,)),
        pltpu.SemaphoreType.DMA((n_exports,)),
    ]


def _stream_rows(src, n_chunks, stage, sems, consume):
    depth, chunk_rows, cols = stage.shape

    def load(i):
        return pltpu.make_async_copy(src.at[pl.ds(i * chunk_rows, chunk_rows), pl.ds(0, cols)],
                                     stage.at[i % depth], sems.at[i % depth])

    for i in range(min(depth - 1, n_chunks)):
        load(i).start()
    for i in range(n_chunks):
        if i + depth - 1 < n_chunks:
            load(i + depth - 1).start()
        load(i).wait()
        consume(i, stage[i % depth])


def _weights_to_bf16(wgu_hbm, wd_hbm, third_hbm, wgu_ref, wd_ref, third_ref, third_is_transposed,
                     stage_wide, stage_narrow, sems_wide, sems_narrow):
    wide_rows, narrow_rows = stage_wide.shape[1], stage_narrow.shape[1]

    def put_wgu(i, chunk):
        wgu_ref[i * wide_rows:(i + 1) * wide_rows, :] = chunk.astype(BF16)

    def put_wd(i, chunk):
        wd_ref[i * narrow_rows:(i + 1) * narrow_rows, :] = chunk.astype(BF16)

    def put_third(i, chunk):
        if third_is_transposed:
            third_ref[:, i * narrow_rows:(i + 1) * narrow_rows] = chunk.T.astype(BF16)
        else:
            third_ref[i * narrow_rows:(i + 1) * narrow_rows, :] = chunk.astype(BF16)

    third_rows = third_ref.shape[1] if third_is_transposed else third_ref.shape[0]
    _stream_rows(wgu_hbm, wgu_ref.shape[0] // wide_rows, stage_wide, sems_wide, put_wgu)
    _stream_rows(wd_hbm, wd_ref.shape[0] // narrow_rows, stage_narrow, sems_narrow, put_wd)
    _stream_rows(third_hbm, third_rows // narrow_rows, stage_narrow, sems_narrow, put_third)


def _gate_weights(win_hbm, wg_ref, stage_narrow, sems_narrow):
    rows = stage_narrow.shape[1]
    tail = pltpu.make_async_copy(win_hbm.at[pl.ds(D_IN - rows, rows), :], stage_narrow.at[0], sems_narrow.at[0])
    tail.start()
    tail.wait()
    cols = stage_narrow[0].T
    lane = lax.broadcasted_iota(jnp.int32, cols.shape, 1)
    i_part = jnp.where(lane < N_HEADS, pltpu.roll(cols, 2 * N_HEADS, axis=1), 0.0)
    f_part = jnp.where(lane < N_HEADS, pltpu.roll(cols, N_HEADS, axis=1), 0.0)
    wg_ref[...] = jnp.concatenate([i_part, f_part], axis=1).astype(BF16)


def _front_kernel(x_ref, ada_ref, wgu_hbm, wd_hbm, win_hbm, g1_ref, g2_ref, bin_ref, bg_ref, mg_ref,
                  x1_ref, u_ref, h_ref, c_ref, n_ref, m_ref, wgu_out, wd_out, win_out, wg_out,
                  hid_ref, zbuf, gbuf, cext, mcar, wgu_ref, wd_ref, win_ref, wg_ref,
                  stage_wide, stage_narrow, sems_wide, sems_narrow, out_sems, *, tt, nt, steps):
    s = pl.program_id(0)
    tl = lax.rem(jnp.maximum(s - 1, 0), nt)
    resident = (wgu_ref, wd_ref, win_ref, wg_ref)
    exported = (wgu_out, wd_out, win_out, wg_out)

    @pl.when(s == 0)
    def _():
        _weights_to_bf16(wgu_hbm, wd_hbm, win_hbm, wgu_ref, wd_ref, win_ref, True,
                         stage_wide, stage_narrow, sems_wide, sems_narrow)
        _gate_weights(win_hbm, wg_ref, stage_narrow, sems_narrow)
        for k in range(len(resident)):
            _weight_export(k, resident, exported, out_sems).start()
        zbuf[...] = jnp.zeros(zbuf.shape, F32)
        gbuf[...] = jnp.zeros(gbuf.shape, F32)

    @pl.when(s == steps)
    def _():
        for k in range(len(resident)):
            _weight_export(k, resident, exported, out_sems).wait()

    @pl.when(tl == 0)
    def _():
        cext[...] = jnp.zeros(cext.shape, F32)
        mcar[...] = jnp.zeros(mcar.shape, F32)

    def mlstm_pieces(g):
        rs = slice(g * GROUP, (g + 1) * GROUP)

        def load_z(which, h):
            c0 = which * D_MLSTM + h * HEAD_DIM
            return zbuf[rs, c0:c0 + HEAD_DIM]

        def store_h(h, val):
            h_ref[0, rs, h * HEAD_DIM:(h + 1) * HEAD_DIM] = val.astype(BF16)

        return _mlstm_group(gbuf[rs, 0:LANES], gbuf[rs, LANES:D_GATE], load_z, store_h, mg_ref, cext, mcar)

    def mixer_pieces():
        for g in range(tt // GROUP):
            yield from mlstm_pieces(g)

    @pl.when(s < steps)
    def _():
        pieces = mixer_pieces()
        costs = ([10] + [5] * N_HEADS) * (tt // GROUP)
        glu_in = []

        def store_z(c0, zc):
            if c0 < Q_OFF:
                glu_in.append(zc)
                if len(glu_in) == 2:
                    u_ref[0] = glu_in[0] * jax.nn.sigmoid(glu_in[1])
            else:
                if c0 == Q_OFF:
                    _run(pieces)
                zbuf[:, c0 - Q_OFF:c0 - Q_OFF + D_CONV] = zc

        x1, gates = _ffn1_inproj(x_ref[...], ada_ref[pl.ds(s // nt, 1), :], g1_ref, wgu_ref, wd_ref, g2_ref,
                                 win_ref, bin_ref, wg_ref, bg_ref, hid_ref, store_z,
                                 _piece_feeder(pieces, costs, FRONT_TIE_WEIGHTS))
        x1_ref[...] = x1
        gbuf[...] = gates

    @pl.when(s == steps)
    def _():
        _run(mixer_pieces())

    @pl.when(tl == nt - 1)
    def _():
        for h in range(N_HEADS):
            c_ref[0, h] = cext[h, :, :HEAD_DIM]
            n_ref[0, h:h + 1, :] = cext[h, :, HEAD_DIM:].T[0:1, :]
        m_ref[0] = mcar[...]


def _front(x, ada, p, *, tt):
    bsz, seq, _ = x.shape
    nt = seq // tt
    steps = bsz * nt
    cur = lambda s: jnp.minimum(s, steps - 1)
    lag = lambda s: jnp.maximum(s - 1, 0)
    big = [p["wgu1"], p["wd1"], p["win"]]
    big_shapes = [(D_MODEL, 2 * D_FF), (D_FF, D_MODEL), (D_MODEL, D_Z), (D_MODEL, D_GATE)]
    small = [p["g1"], p["g2"], p["b_main"], p["b_gate"], p["mn_g"]]
    any_spec = pl.BlockSpec(memory_space=pl.ANY)
    return pl.pallas_call(
        functools.partial(_front_kernel, tt=tt, nt=nt, steps=steps),
        grid=(steps + 1,),
        in_specs=[
            pl.BlockSpec((1, tt, D_MODEL), lambda s: (cur(s) // nt, cur(s) % nt, 0)),
            _const_spec(ada.shape),
        ] + [any_spec] * len(big) + [_const_spec(w.shape) for w in small],
        out_specs=[
            pl.BlockSpec((1, tt, D_MODEL), lambda s: (cur(s) // nt, cur(s) % nt, 0)),
            pl.BlockSpec((1, tt, D_CONV), lambda s: (cur(s) // nt, cur(s) % nt, 0)),
            pl.BlockSpec((1, tt, D_MLSTM), lambda s: (lag(s) // nt, lag(s) % nt, 0)),
            pl.BlockSpec((1, N_HEADS, HEAD_DIM, HEAD_DIM), lambda s: (lag(s) // nt, 0, 0, 0)),
            pl.BlockSpec((1, N_HEADS, HEAD_DIM), lambda s: (lag(s) // nt, 0, 0)),
            pl.BlockSpec((1, 1, LANES), lambda s: (lag(s) // nt, 0, 0)),
        ] + [any_spec] * len(big_shapes),
        out_shape=[
            jax.ShapeDtypeStruct(x.shape, F32),
            jax.ShapeDtypeStruct((bsz, seq, D_CONV), F32),
            jax.ShapeDtypeStruct((bsz, seq, D_MLSTM), BF16),
            jax.ShapeDtypeStruct((bsz, N_HEADS, HEAD_DIM, HEAD_DIM), F32),
            jax.ShapeDtypeStruct((bsz, N_HEADS, HEAD_DIM), F32),
            jax.ShapeDtypeStruct((bsz, 1, LANES), F32),
        ] + [jax.ShapeDtypeStruct(shape, BF16) for shape in big_shapes],
        scratch_shapes=[
            pltpu.VMEM((tt, D_FF), BF16),
            pltpu.VMEM((tt, 4 * D_MLSTM), F32),
            pltpu.VMEM((tt, D_GATE), F32),
            pltpu.VMEM((N_HEADS, HEAD_DIM, 2 * HEAD_DIM), F32),
            pltpu.VMEM((1, LANES), F32),
        ] + [pltpu.VMEM(shape, BF16) for shape in big_shapes] + _cast_scratch(len(big_shapes)),
        compiler_params=_params(1),
        name="front",
    )(x, ada, *big, *small)


def _back_kernel(x_ref, ada_ref, u_ref, h_ref, sx_ref, sada_ref, sy_ref, sh_ref,
                 wgu_hbm, wd_hbm, wout_hbm, g3_ref, gf_ref, cw_ref, cb_ref, lg_ref, lb_ref,
                 o_ref, cs_ref, so_ref, wgu_out, wd_out, wout_out,
                 hid_ref, ubuf, ush, ycur, ynext, wgu_ref, wd_ref, wout_ref,
                 stage_wide, stage_narrow, sems_wide, sems_narrow, out_sems, *, tt, nt, rt, steps):
    s = pl.program_id(0)
    tc = lax.rem(jnp.minimum(s, steps - 1), nt)
    resident = (wgu_ref, wd_ref, wout_ref)
    exported = (wgu_out, wd_out, wout_out)

    @pl.when(s == 0)
    def _():
        _weights_to_bf16(wgu_hbm, wd_hbm, wout_hbm, wgu_ref, wd_ref, wout_ref, False,
                         stage_wide, stage_narrow, sems_wide, sems_narrow)
        for k in range(len(resident)):
            _weight_export(k, resident, exported, out_sems).start()
        ycur[...] = jnp.zeros(ycur.shape, BF16)

    @pl.when(s == steps)
    def _():
        for k in range(len(resident)):
            _weight_export(k, resident, exported, out_sems).wait()

    @pl.when(tc == 0)
    def _():
        ubuf[:, 0:TAIL_PAD, :] = jnp.zeros((1, TAIL_PAD, D_CONV), F32)

    def store_y(b0, r0, val):
        ynext[r0:r0 + rt, :] = val[0].astype(BF16)

    def matmul_half(pieces, between):
        x = x_ref[...]
        ada = ada_ref[pl.ds(jnp.maximum(s - 1, 0) // nt, 1), :]
        mix = _dot(ycur[...], wout_ref[:D_CONV, :]) + _dot(h_ref[0], wout_ref[D_CONV:, :])
        if between is not None:
            mix = _tie(mix, between())
        x2 = x + _ada_row(ada, 5) * mix.reshape(1, tt, D_MODEL)
        xm = _rms(x2, g3_ref[...]) * (1.0 + _ada_row(ada, 7)) + _ada_row(ada, 6)
        ff = _swiglu(xm.reshape(tt, D_MODEL).astype(BF16), wgu_ref, wd_ref, hid_ref, between)
        _run(pieces)
        x3 = x2 + 0.5 * _ada_row(ada, 8) * ff.reshape(1, tt, D_MODEL)
        o_ref[...] = _rms(x3, gf_ref[...])

    @pl.when(s < steps)
    def _():
        pieces = _conv_group(u_ref[...], ubuf, ush, cw_ref, cb_ref, lg_ref, lb_ref, store_y,
                             lambda tail: None, bt=1, rt=rt)
        costs = [2] + [5] * (SUBLANES - 1) + [7 * rt // (2 * SUBLANES)] * (tt // rt)
        matmul_half(pieces, _piece_feeder(pieces, costs, [1.0] * (1 + N_FF_CHUNKS)))
        ycur[...] = ynext[...]

    @pl.when(s == steps)
    def _():
        matmul_half((), None)

    @pl.when(s > steps)
    def _():
        _ffn_b_tile(sx_ref, sada_ref, sy_ref, sh_ref, wout_ref, g3_ref, wgu_ref, wd_ref, gf_ref, so_ref, hid_ref)

    @pl.when(tc == nt - 1)
    def _():
        cs_ref[...] = ubuf[:, TAIL_LO:TAIL_PAD, :]


def _back(x1, ada, u, h, sample, p, *, tt, rt):
    bsz, seq, _ = x1.shape
    nt = seq // tt
    steps = bsz * nt
    sx, sada, sy, sh = sample
    s_bsz, s_seq, _ = sx.shape
    s_bb = tt // s_seq
    s_steps = s_bsz // s_bb
    cur = lambda s: jnp.minimum(s, steps - 1)
    lag = lambda s: jnp.clip(s - 1, 0, steps - 1)
    smp = lambda s: jnp.clip(s - steps - 1, 0, s_steps - 1)
    s_tok = lambda w, **kw: pl.BlockSpec((s_bb, s_seq, w), lambda s: (smp(s), 0, 0), **kw)
    once = dict(pipeline_mode=pl.Buffered(1))
    big = [p["wgu2"], p["wd2"], p["wout"]]
    big_shapes = [(D_MODEL, 2 * D_FF), (D_FF, D_MODEL), (D_MODEL, D_MODEL)]
    small = [p["g3"], p["gf"], p["conv_w"], p["conv_b"], p["cn_g"], p["cn_b"]]
    any_spec = pl.BlockSpec(memory_space=pl.ANY)
    return pl.pallas_call(
        functools.partial(_back_kernel, tt=tt, nt=nt, rt=rt, steps=steps),
        grid=(steps + 1 + s_steps,),
        in_specs=[
            pl.BlockSpec((1, tt, D_MODEL), lambda s: (lag(s) // nt, lag(s) % nt, 0)),
            _const_spec(ada.shape),
            pl.BlockSpec((1, tt, D_CONV), lambda s: (cur(s) // nt, cur(s) % nt, 0)),
            pl.BlockSpec((1, tt, D_MLSTM), lambda s: (lag(s) // nt, lag(s) % nt, 0)),
            s_tok(D_MODEL, **once),
            pl.BlockSpec((s_bb, N_ADA * D_MODEL), lambda s: (smp(s), 0), **once),
            s_tok(D_CONV, **once),
            s_tok(D_MLSTM, **once),
        ] + [any_spec] * len(big) + [_const_spec(w.shape) for w in small],
        out_specs=[
            pl.BlockSpec((1, tt, D_MODEL), lambda s: (lag(s) // nt, lag(s) % nt, 0)),
            pl.BlockSpec((1, CONV_TAIL, D_CONV), lambda s: (cur(s) // nt, 0, 0)),
            s_tok(D_MODEL),
        ] + [any_spec] * len(big_shapes),
        out_shape=[
            jax.ShapeDtypeStruct(x1.shape, F32),
            jax.ShapeDtypeStruct((bsz, CONV_TAIL, D_CONV), F32),
            jax.ShapeDtypeStruct(sx.shape, F32),
        ] + [jax.ShapeDtypeStruct(shape, BF16) for shape in big_shapes],
        scratch_shapes=[
            pltpu.VMEM((tt, D_FF), BF16),
            pltpu.VMEM((1, TAIL_PAD + tt, D_CONV), F32),
            pltpu.VMEM((SUBLANES - 1, 1, TAIL_PAD - SUBLANES + tt, D_CONV), F32),
            pltpu.VMEM((tt, D_CONV), BF16),
            pltpu.VMEM((tt, D_CONV), BF16),
        ] + [pltpu.VMEM(shape, BF16) for shape in big_shapes] + _cast_scratch(len(big_shapes)),
        compiler_params=_params(1),
        name="back",
    )(x1, ada, u, h, sx, sada, sy, sh, *big, *small)


def _ffn_a_kernel(x_ref, ada_ref, g1_ref, wgu_ref, wd_ref, g2_ref, win_ref, bin_ref, wg_ref, bg_ref,
                  x1_ref, z_ref, gate_ref, hid_ref):
    def store_z(c0, zc):
        z_ref[:, c0:c0 + D_CONV] = zc

    x1, gates = _ffn1_inproj(x_ref[...], ada_ref[...], g1_ref, wgu_ref, wd_ref, g2_ref,
                             win_ref, bin_ref, wg_ref, bg_ref, hid_ref, store_z)
    x1_ref[...] = x1
    gate_ref[...] = gates


def _ffn_a(x, ada, p, *, bb):
    bsz, seq, _ = x.shape
    n = bb * seq
    weights = [p["g1"], p["wgu1"], p["wd1"], p["g2"], p["win"], p["b_main"], p["wg"], p["b_gate"]]
    return pl.pallas_call(
        _ffn_a_kernel,
        grid=(bsz // bb,),
        in_specs=[
            pl.BlockSpec((bb, seq, D_MODEL), lambda b: (b, 0, 0)),
            pl.BlockSpec((bb, N_ADA * D_MODEL), lambda b: (b, 0)),
        ] + [_const_spec(w.shape) for w in weights],
        out_specs=[
            pl.BlockSpec((bb, seq, D_MODEL), lambda b: (b, 0, 0)),
            pl.BlockSpec((n, D_Z), lambda b: (b, 0)),
            pl.BlockSpec((n, D_GATE), lambda b: (b, 0)),
        ],
        out_shape=[
            jax.ShapeDtypeStruct(x.shape, F32),
            jax.ShapeDtypeStruct((bsz * seq, D_Z), F32),
            jax.ShapeDtypeStruct((bsz * seq, D_GATE), F32),
        ],
        scratch_shapes=[pltpu.VMEM((n, D_FF), BF16)],
        compiler_params=_params(1),
        name="ffn_a",
    )(x, ada, *weights)


def _conv_kernel(a_ref, b_ref, cs_ref, w_ref, cb_ref, lg_ref, lb_ref, y_ref, so_ref, full):
    seq = a_ref.shape[1]
    full[0:CONV_TAIL] = cs_ref[...]
    for t in range(seq):
        full[CONV_TAIL + t] = a_ref[:, t, :] * jax.nn.sigmoid(b_ref[:, t, :])
    for t in range(seq):
        acc = jnp.broadcast_to(cb_ref[...], full.shape[1:])
        for j in range(CONV_WIDTH):
            acc = acc + w_ref[j:j + 1, :] * full[t + j]
        mu = jnp.mean(acc, axis=-1, keepdims=True)
        xc = acc - mu
        var = jnp.mean(xc * xc, axis=-1, keepdims=True)
        yn = xc * lax.rsqrt(var + EPS) * lg_ref[...] + lb_ref[...]
        y_ref[:, t, :] = yn * jax.nn.sigmoid(yn)
    so_ref[...] = full[seq:seq + CONV_TAIL]


def _conv(z3, state_t, p, *, bb):
    bsz, seq, _ = z3.shape
    weights = [p["conv_w"], p["conv_b"], p["cn_g"], p["cn_b"]]
    state_spec = pl.BlockSpec((CONV_TAIL, bb, D_CONV), lambda b: (0, b, 0))
    return pl.pallas_call(
        _conv_kernel,
        grid=(bsz // bb,),
        in_specs=[
            pl.BlockSpec((bb, seq, D_CONV), lambda b: (b, 0, 0)),
            pl.BlockSpec((bb, seq, D_CONV), lambda b: (b, 0, 1)),
            state_spec,
        ] + [_const_spec(w.shape) for w in weights],
        out_specs=[pl.BlockSpec((bb, seq, D_CONV), lambda b: (b, 0, 0)), state_spec],
        out_shape=[
            jax.ShapeDtypeStruct((bsz, seq, D_CONV), F32),
            jax.ShapeDtypeStruct((CONV_TAIL, bsz, D_CONV), F32),
        ],
        scratch_shapes=[pltpu.VMEM((CONV_TAIL + seq, bb, D_CONV), F32)],
        compiler_params=_params(1),
        name="conv",
    )(z3, z3, state_t, *weights)


def _mlstm_sample_kernel(q_ref, k_ref, v_ref, o_ref, gt_ref, mg_ref, c0_ref, n0_ref, m0_ref,
                         h_ref, c_ref, n_ref, m_ref, *, nseq, ls):
    row = lax.broadcasted_iota(jnp.int32, (GROUP, GROUP), 0)
    col = lax.broadcasted_iota(jnp.int32, (GROUP, GROUP), 1)
    shift = ls.bit_length() - 1
    seg_r = lax.shift_right_logical(row, shift)
    seg_c = lax.shift_right_logical(col, shift)
    causal = jnp.logical_and(col <= row, seg_r == seg_c)
    segtril = causal.astype(F32)
    lastsel = (col == seg_r * ls + (ls - 1)).astype(F32)

    gates = gt_ref[...].reshape(GROUP, D_GATE)
    gi = gates[:, :LANES]
    cum = _dot_exact(segtril, _log_sigmoid(gates[:, LANES:]))
    cumlast = _dot_exact(lastsel, cum)
    cum_t = cum.T
    gi_t = gi.T
    mprev = m0_ref[...].reshape(GROUP, LANES)
    qg = q_ref[...].reshape(GROUP, D_MLSTM)
    kg = k_ref[...].reshape(GROUP, D_MLSTM) * (HEAD_DIM ** -0.5)
    vg = v_ref[...].reshape(GROUP, D_MLSTM)
    og = o_ref[...].reshape(GROUP, D_MLSTM)

    m_all = jnp.zeros((GROUP, LANES), F32)
    lane = lax.broadcasted_iota(jnp.int32, (GROUP, LANES), 1)
    hs = []
    for h in range(N_HEADS):
        sl = slice(h * HEAD_DIM, (h + 1) * HEAD_DIM)
        cum_col = cum[:, h:h + 1]
        dmat = jnp.where(causal, cum_col - cum_t[h:h + 1, :] + gi_t[h:h + 1, :], -jnp.inf)
        inter = cum_col + mprev[:, h:h + 1]
        m_col = jnp.maximum(inter, jnp.max(dmat, axis=-1, keepdims=True))
        w_intra = jnp.exp(dmat - m_col)
        w_inter = jnp.exp(inter - m_col)
        qh = qg[:, sl]
        qb = qh.astype(BF16)
        s = lax.dot_general(qb, kg[:, sl].astype(BF16), NT_DIMS, preferred_element_type=F32)
        p = s * w_intra
        num = _dot(p.astype(BF16), vg[:, sl].astype(BF16))
        den = jnp.sum(p, axis=-1, keepdims=True)
        qc = jnp.zeros((GROUP, HEAD_DIM), F32)
        qn = jnp.zeros((GROUP, 1), F32)
        for sq in range(nseq):
            qc = jnp.where(seg_r == sq, _dot(qb, c0_ref[sq, h].astype(BF16)), qc)
            qn = jnp.where(seg_r[:, 0:1] == sq,
                           jnp.sum(qh * n0_ref[sq, h:h + 1, :], axis=-1, keepdims=True), qn)
        num = num + w_inter * qc
        den = den + w_inter * qn
        hh = num / jnp.maximum(jnp.abs(den), jnp.exp(-m_col))
        hh = hh * lax.rsqrt(jnp.mean(hh * hh, axis=-1, keepdims=True) + EPS) * mg_ref[:, sl]
        hs.append(jax.nn.sigmoid(og[:, sl]) * hh)
        m_all = jnp.where(lane == h, m_col, m_all)

    h_ref[...] = jnp.concatenate(hs, axis=-1).reshape(h_ref.shape)
    m_ref[...] = m_all.reshape(m_ref.shape)

    mnew = _dot_exact(lastsel, m_all)
    for h in range(N_HEADS):
        sl = slice(h * HEAD_DIM, (h + 1) * HEAD_DIM)
        last_col = cumlast[:, h:h + 1]
        mnew_col = mnew[:, h:h + 1]
        w_k = jnp.exp(last_col - cum[:, h:h + 1] + gi[:, h:h + 1] - mnew_col)
        decay = jnp.exp(last_col + mprev[:, h:h + 1] - mnew_col)
        kw = kg[:, sl] * w_k
        kwb = kw.astype(BF16)
        vh = vg[:, sl]
        for sq in range(nseq):
            vs = jnp.where(seg_r == sq, vh, 0.0)
            kws = jnp.where(seg_r == sq, kw, 0.0)
            dc = lax.dot_general(kwb, vs.astype(BF16), TN_DIMS, preferred_element_type=F32)
            dn = jnp.sum(kws, axis=0, keepdims=True)
            dec = decay[sq * ls:sq * ls + 1, :]
            c_ref[sq, h] = dec * c0_ref[sq, h] + dc
            n_ref[sq, h:h + 1, :] = dec * n0_ref[sq, h:h + 1, :] + dn


def _mlstm_sample(z3, gates3, mg, c0, n0, m_tok, *, bb):
    bsz, seq, _ = z3.shape
    assert bb * seq == GROUP
    zspec = lambda k: pl.BlockSpec((bb, seq, D_MLSTM), lambda b: (b, 0, k))
    c_spec = pl.BlockSpec((bb, N_HEADS, HEAD_DIM, HEAD_DIM), lambda b: (b, 0, 0, 0))
    n_spec = pl.BlockSpec((bb, N_HEADS, HEAD_DIM), lambda b: (b, 0, 0))
    m_spec = pl.BlockSpec((bb, seq, LANES), lambda b: (b, 0, 0))
    return pl.pallas_call(
        functools.partial(_mlstm_sample_kernel, nseq=bb, ls=seq),
        grid=(bsz // bb,),
        in_specs=[zspec(2), zspec(3), zspec(4), zspec(5),
                  pl.BlockSpec((bb, seq, D_GATE), lambda b: (b, 0, 0)),
                  _const_spec(mg.shape),
                  c_spec, n_spec, m_spec],
        out_specs=[pl.BlockSpec((bb, seq, D_MLSTM), lambda b: (b, 0, 0)), c_spec, n_spec, m_spec],
        out_shape=[
            jax.ShapeDtypeStruct((bsz, seq, D_MLSTM), F32),
            jax.ShapeDtypeStruct((bsz, N_HEADS, HEAD_DIM, HEAD_DIM), F32),
            jax.ShapeDtypeStruct((bsz, N_HEADS, HEAD_DIM), F32),
            jax.ShapeDtypeStruct((bsz, seq, LANES), F32),
        ],
        compiler_params=_params(1),
        name="mlstm_sample",
    )(z3, z3, z3, z3, gates3, mg, c0, n0, m_tok)


def _ffn_b_tile(x_ref, ada_ref, y_ref, h_ref, wout_ref, g3_ref, wgu_ref, wd_ref, gf_ref, o_ref, hid_ref):
    x = x_ref[...]
    bb, tt, _ = x.shape
    n = bb * tt
    ada = ada_ref[...]
    mix = (_dot(y_ref[...].reshape(n, D_CONV).astype(BF16), wout_ref[:D_CONV, :])
           + _dot(h_ref[...].reshape(n, D_MLSTM).astype(BF16), wout_ref[D_CONV:, :]))
    x2 = x + _ada_row(ada, 5) * mix.reshape(bb, tt, D_MODEL)
    xm = _rms(x2, g3_ref[...]) * (1.0 + _ada_row(ada, 7)) + _ada_row(ada, 6)
    ff = _swiglu(xm.reshape(n, D_MODEL).astype(BF16), wgu_ref, wd_ref, hid_ref)
    x3 = x2 + 0.5 * _ada_row(ada, 8) * ff.reshape(bb, tt, D_MODEL)
    o_ref[...] = _rms(x3, gf_ref[...])


def _sample_mixer(x, ada, states, p):
    bsz, seq, _ = x.shape
    conv_state, c0, n0, m0 = states
    x1, z, gates = _ffn_a(x, ada, p, bb=32)
    z3 = z.reshape(bsz, seq, D_Z)
    gates3 = gates.reshape(bsz, seq, D_GATE)
    y, conv_new_t = _conv(z3, jnp.transpose(conv_state, (1, 0, 2)), p, bb=32)
    conv_new = jnp.transpose(conv_new_t, (1, 0, 2))
    m_tok = jnp.broadcast_to(
        jnp.pad(m0, ((0, 0), (0, LANES - N_HEADS)))[:, None, :], (bsz, seq, LANES))
    h, c_new, n_new, m_tok_new = _mlstm_sample(z3, gates3, p["mn_g"], c0, n0, m_tok, bb=GROUP // seq)
    return (x1, ada, y, h), (conv_new[None], c_new[None], n_new[None], m_tok_new[None, :, seq - 1, :N_HEADS])


def _gate_cols(a):
    pad = ((0, 0), (0, LANES - N_HEADS))
    return jnp.concatenate([jnp.pad(a[:, :N_HEADS], pad), jnp.pad(a[:, N_HEADS:], pad)], axis=1)


def kernel(x_prompt, x_sample, c_prompt, c_sample, state_conv, state_C, state_n, state_m, w_ada, b_ada, norm_ffn1, ffn1_w_gu, ffn1_w_down, norm_mix, w_in, b_in, conv_w, conv_b, conv_norm_g, conv_norm_b, mlstm_norm_g, w_out, norm_ffn2, ffn2_w_gu, ffn2_w_down, norm_final):
    assert w_ada.shape[0] == 1, "single layer"
    w_in_t = jnp.transpose(w_in[0])
    p = {
        "g1": norm_ffn1, "g2": norm_mix, "g3": norm_ffn2, "gf": norm_final[None],
        "wgu1": ffn1_w_gu[0], "wd1": ffn1_w_down[0], "wgu2": ffn2_w_gu[0], "wd2": ffn2_w_down[0],
        "win": w_in_t, "wout": w_out[0],
        "b_main": b_in,
        "b_gate": _gate_cols(b_in[:, D_Z:]),
        "conv_w": jnp.pad(conv_w[0], ((0, TAIL_PAD - CONV_WIDTH), (0, 0))),
        "conv_b": conv_b, "cn_g": conv_norm_g, "cn_b": conv_norm_b, "mn_g": mlstm_norm_g,
    }
    bs = x_sample.shape[0]
    ada = _ada(jnp.concatenate([c_sample, c_prompt], axis=0), w_ada[0], b_ada)
    ada_p = ada[bs:]
    x1, u, h, c_p, n_p, m_row, wgu1, wd1, win, wg = _front(x_prompt, ada_p, p, tt=512)
    sample_in, (conv_s, c_s, n_s, m_s) = _sample_mixer(
        x_sample, ada, (state_conv[0], state_C[0], state_n[0], state_m[0]),
        {**p, "wgu1": wgu1, "wd1": wd1, "win": win, "wg": wg})
    yp, conv_p, ys, _, _, _ = _back(x1, ada_p, u, h, sample_in, p, tt=512, rt=32)
    return (yp, ys, conv_p[None], c_p[None], n_p[None], m_row[None, :, 0, :N_HEADS], conv_s, c_s, n_s, m_s)
```

```python
import functools

import jax
import jax.numpy as jnp
from jax import lax
from jax.experimental import pallas as pl
from jax.experimental.pallas import tpu as pltpu

F32 = jnp.float32
BF16 = jnp.bfloat16

D_MODEL = 1024
D_CONV = 512
D_MLSTM = 512
N_HEADS = 4
HEAD_DIM = 128
CONV_WIDTH = 31
CONV_TAIL = CONV_WIDTH - 1
D_FF = 2816
N_ADA = 9
EPS = 1e-6
D_Z = 2 * D_CONV + 4 * D_MLSTM
D_IN = D_Z + 2 * N_HEADS
Q_OFF = 2 * D_CONV
LANES = 128
SUBLANES = 8
D_GATE = 2 * LANES
FF_CHUNK = 256
N_FF_CHUNKS = D_FF // FF_CHUNK
GROUP = 128
TAIL_PAD = 32
TAIL_LO = TAIL_PAD - CONV_TAIL
TIE_PARTS = 2
FRONT_TIE_WEIGHTS = [0.0, 0.5] + [1.0] * (N_FF_CHUNKS - 2) + [0.7, 0.7, 0.7]
CAST_SLOTS = 4
CAST_ROWS_WIDE = 64
CAST_ROWS_NARROW = 128
VMEM_LIMIT = 57 * 1024 * 1024
NT_DIMS = (((1,), (1,)), ((), ()))
TN_DIMS = (((0,), (0,)), ((), ()))


def _dot(a, b):
    return jnp.dot(a, b, preferred_element_type=F32)


def _dot_exact(a, b):
    return jnp.dot(a, b, preferred_element_type=F32, precision=lax.Precision.HIGHEST)


def _rms(x, g):
    ms = jnp.mean(x * x, axis=-1, keepdims=True)
    return x * lax.rsqrt(ms + EPS) * g


def _ada_row(ada, k):
    return ada[:, k * D_MODEL:(k + 1) * D_MODEL][:, None, :]


def _log_sigmoid(x):
    return jnp.minimum(x, 0.0) - jnp.log(1.0 + jnp.exp(-jnp.abs(x)))


def _const_spec(shape):
    nd = len(shape)
    return pl.BlockSpec(shape, lambda *_: (0,) * nd, pipeline_mode=pl.Buffered(1))


def _params(n_grid):
    return pltpu.CompilerParams(dimension_semantics=("arbitrary",) * n_grid, vmem_limit_bytes=VMEM_LIMIT)


def _ada_kernel(c_ref, w_ref, b_ref, o_ref):
    c = c_ref[...]
    s = (c * jax.nn.sigmoid(c)).astype(BF16)
    o_ref[...] = _dot(s, w_ref[...].astype(BF16)) + b_ref[...]


def _ada(c_all, w_ada, b_ada):
    n = c_all.shape[0]
    tile = D_MODEL
    return pl.pallas_call(
        _ada_kernel,
        grid=(N_ADA,),
        in_specs=[
            pl.BlockSpec((n, D_MODEL), lambda j: (0, 0)),
            pl.BlockSpec((D_MODEL, tile), lambda j: (0, j)),
            pl.BlockSpec((1, tile), lambda j: (0, j)),
        ],
        out_specs=pl.BlockSpec((n, tile), lambda j: (0, j)),
        out_shape=jax.ShapeDtypeStruct((n, N_ADA * D_MODEL), F32),
        compiler_params=_params(1),
        name="ada",
    )(c_all, w_ada, b_ada)


def _tie(value, zeros):
    rb = value.shape[0] // TIE_PARTS
    packing = 4 // value.dtype.itemsize
    reps = (rb // (SUBLANES * packing), value.shape[1] // LANES)

    def add(block, zero):
        if zero is None:
            return block
        zero = jnp.concatenate([zero] * packing, axis=0).astype(value.dtype)
        return block + jnp.tile(zero, reps)

    return jnp.concatenate([add(value[k * rb:(k + 1) * rb], z) for k, z in enumerate(zeros)], axis=0)


def _swiglu(xm, wgu_ref, wd_ref, hid_ref, between=None):
    for j in range(N_FF_CHUNKS):
        lo = j * FF_CHUNK
        zeros = None if between is None else between()
        gate = _dot(xm, wgu_ref[:, lo:lo + FF_CHUNK])
        up = _dot(xm, wgu_ref[:, D_FF + lo:D_FF + lo + FF_CHUNK])
        hidden = gate * jax.nn.sigmoid(gate) * up
        if zeros is not None:
            hidden = _tie(hidden, zeros)
        hid_ref[:, lo:lo + FF_CHUNK] = hidden.astype(BF16)
    return _dot(hid_ref[...], wd_ref[...])


def _ffn1_inproj(x, ada, g1_ref, wgu_ref, wd_ref, g2_ref, win_ref, bin_ref, wg_ref, bg_ref, hid_ref,
                 store_z, between=None):
    bb, tt, _ = x.shape
    n = bb * tt
    xm = _rms(x, g1_ref[...]) * (1.0 + _ada_row(ada, 1)) + _ada_row(ada, 0)
    ff = _swiglu(xm.reshape(n, D_MODEL).astype(BF16), wgu_ref, wd_ref, hid_ref, between)
    x1 = x + 0.5 * _ada_row(ada, 2) * ff.reshape(bb, tt, D_MODEL)
    hm = _rms(x1, g2_ref[...]) * (1.0 + _ada_row(ada, 4)) + _ada_row(ada, 3)
    hm = hm.reshape(n, D_MODEL).astype(BF16)
    gates = None
    for c0 in range(0, D_Z, D_CONV):
        zc = _dot(hm, win_ref[:, c0:c0 + D_CONV]) + bin_ref[:, c0:c0 + D_CONV]
        if c0 < Q_OFF and between is not None:
            zc = _tie(zc, between())
        if c0 == Q_OFF:
            gates = _dot(hm, wg_ref[...]) + bg_ref[...]
            if between is not None:
                gates = _tie(gates, between())
        store_z(c0, zc)
    return x1, gates


def _run(steps):
    for _ in steps:
        pass


def _all_bits(*arrays):
    acc = None
    for a in arrays:
        bits = lax.bitcast_convert_type(a.reshape(-1, a.shape[-1]), jnp.uint32)
        for r in range(0, bits.shape[0], SUBLANES):
            for c in range(0, bits.shape[1], LANES):
                v = bits[r:r + SUBLANES, c:c + LANES]
                acc = v if acc is None else acc | v
    return acc


def _conv_group(u, ubuf, ush, w_ref, cb_ref, lg_ref, lb_ref, store_y, store_tail, *, bt, rt):
    bb, tt, _ = u.shape
    sh_rows = TAIL_PAD - SUBLANES + tt
    ubuf[:, TAIL_PAD:TAIL_PAD + tt, :] = u
    yield None
    for r in range(1, SUBLANES):
        ush[r - 1] = ubuf[:, r:r + sh_rows, :]
        yield None
    for b0 in range(0, bb, bt):
        for r0 in range(0, tt, rt):
            acc = jnp.broadcast_to(cb_ref[...].reshape(1, 1, D_CONV), (bt, rt, D_CONV))
            for j in range(CONV_WIDTH):
                q, r = divmod(TAIL_LO + j, SUBLANES)
                p0 = r0 + q * SUBLANES
                if r == 0:
                    win = ubuf[b0:b0 + bt, p0:p0 + rt, :]
                else:
                    win = ush[r - 1, b0:b0 + bt, p0:p0 + rt, :]
                acc = acc + w_ref[j:j + 1, :].reshape(1, 1, D_CONV) * win
            mu = jnp.mean(acc, axis=-1, keepdims=True)
            xc = acc - mu
            var = jnp.mean(xc * xc, axis=-1, keepdims=True)
            yn = xc * lax.rsqrt(var + EPS) * lg_ref[...].reshape(1, 1, D_CONV) + lb_ref[...].reshape(1, 1, D_CONV)
            out = yn * jax.nn.sigmoid(yn)
            store_y(b0, r0, out)
            yield _all_bits(out)
    tail = ubuf[:, TAIL_LO + tt:TAIL_PAD + tt, :]
    ubuf[:, TAIL_LO:TAIL_PAD, :] = tail
    store_tail(tail)


def _scan_rows(x, op, ident):
    row = lax.broadcasted_iota(jnp.int32, x.shape, 0)
    s = 1
    while s < x.shape[0]:
        x = op(x, jnp.where(row >= s, pltpu.roll(x, s, axis=0), ident))
        s *= 2
    return x


def _mlstm_group(gi, gf, load_z, store_h, mg_ref, cext, mcar):
    row = lax.broadcasted_iota(jnp.int32, (GROUP, GROUP), 0)
    col = lax.broadcasted_iota(jnp.int32, (GROUP, GROUP), 1)
    causal = col <= row
    ones_b = jnp.ones((GROUP, HEAD_DIM), BF16)
    cum = _scan_rows(_log_sigmoid(gf), jnp.add, 0.0)
    gv = gi - cum
    mx = _scan_rows(gv, jnp.maximum, -jnp.inf)
    mprev = mcar[...]
    mm = jnp.maximum(mprev, mx)
    mcol = cum + mm
    mm_last = mm[GROUP - 1:GROUP, :]
    w_k = jnp.exp(gv - mm_last)
    decay = jnp.exp(mprev - mm_last)
    mcar[...] = mcol[GROUP - 1:GROUP, :]
    gv_t = gv.T
    yield _all_bits(gv_t[0:SUBLANES, :])
    for h in range(N_HEADS):
        sl = slice(h * HEAD_DIM, (h + 1) * HEAD_DIM)
        mm_b = jnp.broadcast_to(mm[:, h:h + 1], (GROUP, GROUP))
        mcol_b = jnp.broadcast_to(mcol[:, h:h + 1], (GROUP, GROUP))
        w_intra = jnp.exp(jnp.where(causal, gv_t[h:h + 1, :] - mm_b, -jnp.inf))
        w_inter = jnp.exp(mprev[:, h:h + 1] - mm_b)
        qb = load_z(0, h).astype(BF16)
        kh = load_z(1, h) * (HEAD_DIM ** -0.5)
        v_ext = jnp.concatenate([load_z(2, h).astype(BF16), ones_b], axis=1)
        s = lax.dot_general(qb, kh.astype(BF16), NT_DIMS, preferred_element_type=F32)
        pv = _dot((s * w_intra).astype(BF16), v_ext)
        cprev = cext[h]
        qc = _dot(qb, cprev.astype(BF16))
        num = pv[:, :HEAD_DIM] + w_inter * qc[:, :HEAD_DIM]
        den = pv[:, HEAD_DIM:] + w_inter * qc[:, HEAD_DIM:]
        hh = num / jnp.maximum(jnp.abs(den), jnp.exp(-mcol_b))
        hh = hh * lax.rsqrt(jnp.mean(hh * hh, axis=-1, keepdims=True) + EPS) * mg_ref[:, sl]
        out = jax.nn.sigmoid(load_z(3, h)) * hh
        store_h(h, out)
        kw = (kh * w_k[:, h:h + 1]).astype(BF16)
        dc = lax.dot_general(kw, v_ext, TN_DIMS, preferred_element_type=F32)
        cnew = decay[:, h:h + 1] * cprev + dc
        cext[h] = cnew
        yield _all_bits(out, cnew)


def _piece_feeder(pieces, costs, call_weights):
    point_weights = [w / TIE_PARTS for w in call_weights for _ in range(TIE_PARTS)]
    done = [0, 0]

    def one_point():
        done[0] += 1
        target = sum(costs) * sum(point_weights[:done[0]]) / sum(point_weights)
        bits = None
        while done[1] < len(costs) and sum(costs[:done[1]]) + costs[done[1]] / 2 <= target:
            piece_bits = next(pieces)
            if piece_bits is not None:
                bits = piece_bits if bits is None else bits | piece_bits
            done[1] += 1
        if bits is None:
            return None
        zero_bits = lax.shift_right_logical(lax.shift_right_logical(bits, jnp.uint32(16)), jnp.uint32(16))
        return lax.bitcast_convert_type(zero_bits, F32)

    return lambda: [one_point() for _ in range(TIE_PARTS)]


def _weight_export(k, vmem_refs, out_refs, sems):
    return pltpu.make_async_copy(vmem_refs[k], out_refs[k], sems.at[k])


def _cast_scratch(n_exports):
    return [
        pltpu.VMEM((CAST_SLOTS, CAST_ROWS_WIDE, 2 * D_FF), F32),
        pltpu.VMEM((CAST_SLOTS, CAST_ROWS_NARROW, D_MODEL), F32),
        pltpu.SemaphoreType.DMA((CAST_SLOTS,)),
        pltpu.SemaphoreType.DMA((CAST_SLOTS,)),
        pltpu.SemaphoreType.DMA((n_exports,)),
    ]


def _stream_rows(src, n_chunks, stage, sems, consume):
    depth, chunk_rows, cols = stage.shape

    def load(i):
        return pltpu.make_async_copy(src.at[pl.ds(i * chunk_rows, chunk_rows), pl.ds(0, cols)],
                                     stage.at[i % depth], sems.at[i % depth])

    for i in range(min(depth - 1, n_chunks)):
        load(i).start()
    for i in range(n_chunks):
        if i + depth - 1 < n_chunks:
            load(i + depth - 1).start()
        load(i).wait()
        consume(i, stage[i % depth])


def _weights_to_bf16(wgu_hbm, wd_hbm, third_hbm, wgu_ref, wd_ref, third_ref, third_is_transposed,
                     stage_wide, stage_narrow, sems_wide, sems_narrow):
    wide_rows, narrow_rows = stage_wide.shape[1], stage_narrow.shape[1]

    def put_wgu(i, chunk):
        wgu_ref[i * wide_rows:(i + 1) * wide_rows, :] = chunk.astype(BF16)

    def put_wd(i, chunk):
        wd_ref[i * narrow_rows:(i + 1) * narrow_rows, :] = chunk.astype(BF16)

    def put_third(i, chunk):
        if third_is_transposed:
            third_ref[:, i * narrow_rows:(i + 1) * narrow_rows] = chunk.T.astype(BF16)
        else:
            third_ref[i * narrow_rows:(i + 1) * narrow_rows, :] = chunk.astype(BF16)

    third_rows = third_ref.shape[1] if third_is_transposed else third_ref.shape[0]
    _stream_rows(wgu_hbm, wgu_ref.shape[0] // wide_rows, stage_wide, sems_wide, put_wgu)
    _stream_rows(wd_hbm, wd_ref.shape[0] // narrow_rows, stage_narrow, sems_narrow, put_wd)
    _stream_rows(third_hbm, third_rows // narrow_rows, stage_narrow, sems_narrow, put_third)


def _gate_weights(win_hbm, wg_ref, stage_narrow, sems_narrow):
    rows = stage_narrow.shape[1]
    tail = pltpu.make_async_copy(win_hbm.at[pl.ds(D_IN - rows, rows), :], stage_narrow.at[0], sems_narrow.at[0])
    tail.start()
    tail.wait()
    cols = stage_narrow[0].T
    lane = lax.broadcasted_iota(jnp.int32, cols.shape, 1)
    i_part = jnp.where(lane < N_HEADS, pltpu.roll(cols, 2 * N_HEADS, axis=1), 0.0)
    f_part = jnp.where(lane < N_HEADS, pltpu.roll(cols, N_HEADS, axis=1), 0.0)
    wg_ref[...] = jnp.concatenate([i_part, f_part], axis=1).astype(BF16)


def _front_kernel(x_ref, ada_ref, wgu_hbm, wd_hbm, win_hbm, g1_ref, g2_ref, bin_ref, bg_ref, mg_ref,
                  x1_ref, u_ref, h_ref, c_ref, n_ref, m_ref, wgu_out, wd_out, win_out, wg_out,
                  hid_ref, zbuf, gbuf, cext, mcar, wgu_ref, wd_ref, win_ref, wg_ref,
                  stage_wide, stage_narrow, sems_wide, sems_narrow, out_sems, *, tt, nt, steps):
    s = pl.program_id(0)
    tl = lax.rem(jnp.maximum(s - 1, 0), nt)
    resident = (wgu_ref, wd_ref, win_ref, wg_ref)
    exported = (wgu_out, wd_out, win_out, wg_out)

    @pl.when(s == 0)
    def _():
        _weights_to_bf16(wgu_hbm, wd_hbm, win_hbm, wgu_ref, wd_ref, win_ref, True,
                         stage_wide, stage_narrow, sems_wide, sems_narrow)
        _gate_weights(win_hbm, wg_ref, stage_narrow, sems_narrow)
        for k in range(len(resident)):
            _weight_export(k, resident, exported, out_sems).start()
        zbuf[...] = jnp.zeros(zbuf.shape, F32)
        gbuf[...] = jnp.zeros(gbuf.shape, F32)

    @pl.when(s == steps)
    def _():
        for k in range(len(resident)):
            _weight_export(k, resident, exported, out_sems).wait()

    @pl.when(tl == 0)
    def _():
        cext[...] = jnp.zeros(cext.shape, F32)
        mcar[...] = jnp.zeros(mcar.shape, F32)

    def mlstm_pieces(g):
        rs = slice(g * GROUP, (g + 1) * GROUP)

        def load_z(which, h):
            c0 = which * D_MLSTM + h * HEAD_DIM
            return zbuf[rs, c0:c0 + HEAD_DIM]

        def store_h(h, val):
            h_ref[0, rs, h * HEAD_DIM:(h + 1) * HEAD_DIM] = val.astype(BF16)

        return _mlstm_group(gbuf[rs, 0:LANES], gbuf[rs, LANES:D_GATE], load_z, store_h, mg_ref, cext, mcar)

    def mixer_pieces():
        for g in range(tt // GROUP):
            yield from mlstm_pieces(g)

    @pl.when(s < steps)
    def _():
        pieces = mixer_pieces()
        costs = ([10] + [5] * N_HEADS) * (tt // GROUP)
        glu_in = []

        def store_z(c0, zc):
            if c0 < Q_OFF:
                glu_in.append(zc)
                if len(glu_in) == 2:
                    u_ref[0] = glu_in[0] * jax.nn.sigmoid(glu_in[1])
            else:
                if c0 == Q_OFF:
                    _run(pieces)
                zbuf[:, c0 - Q_OFF:c0 - Q_OFF + D_CONV] = zc

        x1, gates = _ffn1_inproj(x_ref[...], ada_ref[pl.ds(s // nt, 1), :], g1_ref, wgu_ref, wd_ref, g2_ref,
                                 win_ref, bin_ref, wg_ref, bg_ref, hid_ref, store_z,
                                 _piece_feeder(pieces, costs, FRONT_TIE_WEIGHTS))
        x1_ref[...] = x1
        gbuf[...] = gates

    @pl.when(s == steps)
    def _():
        _run(mixer_pieces())

    @pl.when(tl == nt - 1)
    def _():
        for h in range(N_HEADS):
            c_ref[0, h] = cext[h, :, :HEAD_DIM]
            n_ref[0, h:h + 1, :] = cext[h, :, HEAD_DIM:].T[0:1, :]
        m_ref[0] = mcar[...]


def _front(x, ada, p, *, tt):
    bsz, seq, _ = x.shape
    nt = seq // tt
    steps = bsz * nt
    cur = lambda s: jnp.minimum(s, steps - 1)
    lag = lambda s: jnp.maximum(s - 1, 0)
    big = [p["wgu1"], p["wd1"], p["win"]]
    big_shapes = [(D_MODEL, 2 * D_FF), (D_FF, D_MODEL), (D_MODEL, D_Z), (D_MODEL, D_GATE)]
    small = [p["g1"], p["g2"], p["b_main"], p["b_gate"], p["mn_g"]]
    any_spec = pl.BlockSpec(memory_space=pl.ANY)
    return pl.pallas_call(
        functools.partial(_front_kernel, tt=tt, nt=nt, steps=steps),
        grid=(steps + 1,),
        in_specs=[
            pl.BlockSpec((1, tt, D_MODEL), lambda s: (cur(s) // nt, cur(s) % nt, 0)),
            _const_spec(ada.shape),
        ] + [any_spec] * len(big) + [_const_spec(w.shape) for w in small],
        out_specs=[
            pl.BlockSpec((1, tt, D_MODEL), lambda s: (cur(s) // nt, cur(s) % nt, 0)),
            pl.BlockSpec((1, tt, D_CONV), lambda s: (cur(s) // nt, cur(s) % nt, 0)),
            pl.BlockSpec((1, tt, D_MLSTM), lambda s: (lag(s) // nt, lag(s) % nt, 0)),
            pl.BlockSpec((1, N_HEADS, HEAD_DIM, HEAD_DIM), lambda s: (lag(s) // nt, 0, 0, 0)),
            pl.BlockSpec((1, N_HEADS, HEAD_DIM), lambda s: (lag(s) // nt, 0, 0)),
            pl.BlockSpec((1, 1, LANES), lambda s: (lag(s) // nt, 0, 0)),
        ] + [any_spec] * len(big_shapes),
        out_shape=[
            jax.ShapeDtypeStruct(x.shape, F32),
            jax.ShapeDtypeStruct((bsz, seq, D_CONV), F32),
            jax.ShapeDtypeStruct((bsz, seq, D_MLSTM), BF16),
            jax.ShapeDtypeStruct((bsz, N_HEADS, HEAD_DIM, HEAD_DIM), F32),
            jax.ShapeDtypeStruct((bsz, N_HEADS, HEAD_DIM), F32),
            jax.ShapeDtypeStruct((bsz, 1, LANES), F32),
        ] + [jax.ShapeDtypeStruct(shape, BF16) for shape in big_shapes],
        scratch_shapes=[
            pltpu.VMEM((tt, D_FF), BF16),
            pltpu.VMEM((tt, 4 * D_MLSTM), F32),
            pltpu.VMEM((tt, D_GATE), F32),
            pltpu.VMEM((N_HEADS, HEAD_DIM, 2 * HEAD_DIM), F32),
            pltpu.VMEM((1, LANES), F32),
        ] + [pltpu.VMEM(shape, BF16) for shape in big_shapes] + _cast_scratch(len(big_shapes)),
        compiler_params=_params(1),
        name="front",
    )(x, ada, *big, *small)


def _back_kernel(x_ref, ada_ref, u_ref, h_ref, wgu_hbm, wd_hbm, wout_hbm, g3_ref, gf_ref,
                 cw_ref, cb_ref, lg_ref, lb_ref,
                 o_ref, cs_ref, wgu_out, wd_out, wout_out,
                 hid_ref, ubuf, ush, ycur, ynext, wgu_ref, wd_ref, wout_ref,
                 stage_wide, stage_narrow, sems_wide, sems_narrow, out_sems, *, tt, nt, rt, steps):
    s = pl.program_id(0)
    tc = lax.rem(jnp.minimum(s, steps - 1), nt)
    resident = (wgu_ref, wd_ref, wout_ref)
    exported = (wgu_out, wd_out, wout_out)

    @pl.when(s == 0)
    def _():
        _weights_to_bf16(wgu_hbm, wd_hbm, wout_hbm, wgu_ref, wd_ref, wout_ref, False,
                         stage_wide, stage_narrow, sems_wide, sems_narrow)
        for k in range(len(resident)):
            _weight_export(k, resident, exported, out_sems).start()
        ycur[...] = jnp.zeros(ycur.shape, BF16)

    @pl.when(s == steps)
    def _():
        for k in range(len(resident)):
            _weight_export(k, resident, exported, out_sems).wait()

    @pl.when(tc == 0)
    def _():
        ubuf[:, 0:TAIL_PAD, :] = jnp.zeros((1, TAIL_PAD, D_CONV), F32)

    def store_y(b0, r0, val):
        ynext[r0:r0 + rt, :] = val[0].astype(BF16)

    def matmul_half(pieces, between):
        x = x_ref[...]
        ada = ada_ref[pl.ds(jnp.maximum(s - 1, 0) // nt, 1), :]
        mix = _dot(ycur[...], wout_ref[:D_CONV, :]) + _dot(h_ref[0], wout_ref[D_CONV:, :])
        if between is not None:
            mix = _tie(mix, between())
        x2 = x + _ada_row(ada, 5) * mix.reshape(1, tt, D_MODEL)
        xm = _rms(x2, g3_ref[...]) * (1.0 + _ada_row(ada, 7)) + _ada_row(ada, 6)
        ff = _swiglu(xm.reshape(tt, D_MODEL).astype(BF16), wgu_ref, wd_ref, hid_ref, between)
        _run(pieces)
        x3 = x2 + 0.5 * _ada_row(ada, 8) * ff.reshape(1, tt, D_MODEL)
        o_ref[...] = _rms(x3, gf_ref[...])

    @pl.when(s < steps)
    def _():
        pieces = _conv_group(u_ref[...], ubuf, ush, cw_ref, cb_ref, lg_ref, lb_ref, store_y,
                             lambda tail: None, bt=1, rt=rt)
        costs = [2] + [5] * (SUBLANES - 1) + [7 * rt // (2 * SUBLANES)] * (tt // rt)
        matmul_half(pieces, _piece_feeder(pieces, costs, [1.0] * (1 + N_FF_CHUNKS)))
        ycur[...] = ynext[...]

    @pl.when(s == steps)
    def _():
        matmul_half((), None)

    @pl.when(tc == nt - 1)
    def _():
        cs_ref[...] = ubuf[:, TAIL_LO:TAIL_PAD, :]


def _back(x1, ada, u, h, p, *, tt, rt):
    bsz, seq, _ = x1.shape
    nt = seq // tt
    steps = bsz * nt
    cur = lambda s: jnp.minimum(s, steps - 1)
    lag = lambda s: jnp.maximum(s - 1, 0)
    big = [p["wgu2"], p["wd2"], p["wout"]]
    big_shapes = [(D_MODEL, 2 * D_FF), (D_FF, D_MODEL), (D_MODEL, D_MODEL)]
    small = [p["g3"], p["gf"], p["conv_w"], p["conv_b"], p["cn_g"], p["cn_b"]]
    any_spec = pl.BlockSpec(memory_space=pl.ANY)
    return pl.pallas_call(
        functools.partial(_back_kernel, tt=tt, nt=nt, rt=rt, steps=steps),
        grid=(steps + 1,),
        in_specs=[
            pl.BlockSpec((1, tt, D_MODEL), lambda s: (lag(s) // nt, lag(s) % nt, 0)),
            _const_spec(ada.shape),
            pl.BlockSpec((1, tt, D_CONV), lambda s: (cur(s) // nt, cur(s) % nt, 0)),
            pl.BlockSpec((1, tt, D_MLSTM), lambda s: (lag(s) // nt, lag(s) % nt, 0)),
        ] + [any_spec] * len(big) + [_const_spec(w.shape) for w in small],
        out_specs=[
            pl.BlockSpec((1, tt, D_MODEL), lambda s: (lag(s) // nt, lag(s) % nt, 0)),
            pl.BlockSpec((1, CONV_TAIL, D_CONV), lambda s: (cur(s) // nt, 0, 0)),
        ] + [any_spec] * len(big_shapes),
        out_shape=[
            jax.ShapeDtypeStruct(x1.shape, F32),
            jax.ShapeDtypeStruct((bsz, CONV_TAIL, D_CONV), F32),
        ] + [jax.ShapeDtypeStruct(shape, BF16) for shape in big_shapes],
        scratch_shapes=[
            pltpu.VMEM((tt, D_FF), BF16),
            pltpu.VMEM((1, TAIL_PAD + tt, D_CONV), F32),
            pltpu.VMEM((SUBLANES - 1, 1, TAIL_PAD - SUBLANES + tt, D_CONV), F32),
            pltpu.VMEM((tt, D_CONV), BF16),
            pltpu.VMEM((tt, D_CONV), BF16),
        ] + [pltpu.VMEM(shape, BF16) for shape in big_shapes] + _cast_scratch(len(big_shapes)),
        compiler_params=_params(1),
        name="back",
    )(x1, ada, u, h, *big, *small)


def _ffn_a_kernel(x_ref, ada_ref, g1_ref, wgu_ref, wd_ref, g2_ref, win_ref, bin_ref, wg_ref, bg_ref,
                  x1_ref, z_ref, gate_ref, hid_ref):
    def store_z(c0, zc):
        z_ref[:, c0:c0 + D_CONV] = zc

    x1, gates = _ffn1_inproj(x_ref[...], ada_ref[...], g1_ref, wgu_ref, wd_ref, g2_ref,
                             win_ref, bin_ref, wg_ref, bg_ref, hid_ref, store_z)
    x1_ref[...] = x1
    gate_ref[...] = gates


def _ffn_a(x, ada, p, *, bb):
    bsz, seq, _ = x.shape
    n = bb * seq
    weights = [p["g1"], p["wgu1"], p["wd1"], p["g2"], p["win"], p["b_main"], p["wg"], p["b_gate"]]
    return pl.pallas_call(
        _ffn_a_kernel,
        grid=(bsz // bb,),
        in_specs=[
            pl.BlockSpec((bb, seq, D_MODEL), lambda b: (b, 0, 0)),
            pl.BlockSpec((bb, N_ADA * D_MODEL), lambda b: (b, 0)),
        ] + [_const_spec(w.shape) for w in weights],
        out_specs=[
            pl.BlockSpec((bb, seq, D_MODEL), lambda b: (b, 0, 0)),
            pl.BlockSpec((n, D_Z), lambda b: (b, 0)),
            pl.BlockSpec((n, D_GATE), lambda b: (b, 0)),
        ],
        out_shape=[
            jax.ShapeDtypeStruct(x.shape, F32),
            jax.ShapeDtypeStruct((bsz * seq, D_Z), F32),
            jax.ShapeDtypeStruct((bsz * seq, D_GATE), F32),
        ],
        scratch_shapes=[pltpu.VMEM((n, D_FF), BF16)],
        compiler_params=_params(1),
        name="ffn_a",
    )(x, ada, *weights)


def _conv_kernel(a_ref, b_ref, cs_ref, w_ref, cb_ref, lg_ref, lb_ref, y_ref, so_ref, full):
    seq = a_ref.shape[1]
    full[0:CONV_TAIL] = cs_ref[...]
    for t in range(seq):
        full[CONV_TAIL + t] = a_ref[:, t, :] * jax.nn.sigmoid(b_ref[:, t, :])
    for t in range(seq):
        acc = jnp.broadcast_to(cb_ref[...], full.shape[1:])
        for j in range(CONV_WIDTH):
            acc = acc + w_ref[j:j + 1, :] * full[t + j]
        mu = jnp.mean(acc, axis=-1, keepdims=True)
        xc = acc - mu
        var = jnp.mean(xc * xc, axis=-1, keepdims=True)
        yn = xc * lax.rsqrt(var + EPS) * lg_ref[...] + lb_ref[...]
        y_ref[:, t, :] = yn * jax.nn.sigmoid(yn)
    so_ref[...] = full[seq:seq + CONV_TAIL]


def _conv(z3, state_t, p, *, bb):
    bsz, seq, _ = z3.shape
    weights = [p["conv_w"], p["conv_b"], p["cn_g"], p["cn_b"]]
    state_spec = pl.BlockSpec((CONV_TAIL, bb, D_CONV), lambda b: (0, b, 0))
    return pl.pallas_call(
        _conv_kernel,
        grid=(bsz // bb,),
        in_specs=[
            pl.BlockSpec((bb, seq, D_CONV), lambda b: (b, 0, 0)),
            pl.BlockSpec((bb, seq, D_CONV), lambda b: (b, 0, 1)),
            state_spec,
        ] + [_const_spec(w.shape) for w in weights],
        out_specs=[pl.BlockSpec((bb, seq, D_CONV), lambda b: (b, 0, 0)), state_spec],
        out_shape=[
            jax.ShapeDtypeStruct((bsz, seq, D_CONV), F32),
            jax.ShapeDtypeStruct((CONV_TAIL, bsz, D_CONV), F32),
        ],
        scratch_shapes=[pltpu.VMEM((CONV_TAIL + seq, bb, D_CONV), F32)],
        compiler_params=_params(1),
        name="conv",
    )(z3, z3, state_t, *weights)


def _mlstm_sample_kernel(q_ref, k_ref, v_ref, o_ref, gt_ref, mg_ref, c0_ref, n0_ref, m0_ref,
                         h_ref, c_ref, n_ref, m_ref, *, nseq, ls):
    row = lax.broadcasted_iota(jnp.int32, (GROUP, GROUP), 0)
    col = lax.broadcasted_iota(jnp.int32, (GROUP, GROUP), 1)
    shift = ls.bit_length() - 1
    seg_r = lax.shift_right_logical(row, shift)
    seg_c = lax.shift_right_logical(col, shift)
    causal = jnp.logical_and(col <= row, seg_r == seg_c)
    segtril = causal.astype(F32)
    lastsel = (col == seg_r * ls + (ls - 1)).astype(F32)

    gates = gt_ref[...].reshape(GROUP, D_GATE)
    gi = gates[:, :LANES]
    cum = _dot_exact(segtril, _log_sigmoid(gates[:, LANES:]))
    cumlast = _dot_exact(lastsel, cum)
    cum_t = cum.T
    gi_t = gi.T
    mprev = m0_ref[...].reshape(GROUP, LANES)
    qg = q_ref[...].reshape(GROUP, D_MLSTM)
    kg = k_ref[...].reshape(GROUP, D_MLSTM) * (HEAD_DIM ** -0.5)
    vg = v_ref[...].reshape(GROUP, D_MLSTM)
    og = o_ref[...].reshape(GROUP, D_MLSTM)

    m_all = jnp.zeros((GROUP, LANES), F32)
    lane = lax.broadcasted_iota(jnp.int32, (GROUP, LANES), 1)
    hs = []
    for h in range(N_HEADS):
        sl = slice(h * HEAD_DIM, (h + 1) * HEAD_DIM)
        cum_col = cum[:, h:h + 1]
        dmat = jnp.where(causal, cum_col - cum_t[h:h + 1, :] + gi_t[h:h + 1, :], -jnp.inf)
        inter = cum_col + mprev[:, h:h + 1]
        m_col = jnp.maximum(inter, jnp.max(dmat, axis=-1, keepdims=True))
        w_intra = jnp.exp(dmat - m_col)
        w_inter = jnp.exp(inter - m_col)
        qh = qg[:, sl]
        qb = qh.astype(BF16)
        s = lax.dot_general(qb, kg[:, sl].astype(BF16), NT_DIMS, preferred_element_type=F32)
        p = s * w_intra
        num = _dot(p.astype(BF16), vg[:, sl].astype(BF16))
        den = jnp.sum(p, axis=-1, keepdims=True)
        qc = jnp.zeros((GROUP, HEAD_DIM), F32)
        qn = jnp.zeros((GROUP, 1), F32)
        for sq in range(nseq):
            qc = jnp.where(seg_r == sq, _dot(qb, c0_ref[sq, h].astype(BF16)), qc)
            qn = jnp.where(seg_r[:, 0:1] == sq,
                           jnp.sum(qh * n0_ref[sq, h:h + 1, :], axis=-1, keepdims=True), qn)
        num = num + w_inter * qc
        den = den + w_inter * qn
        hh = num / jnp.maximum(jnp.abs(den), jnp.exp(-m_col))
        hh = hh * lax.rsqrt(jnp.mean(hh * hh, axis=-1, keepdims=True) + EPS) * mg_ref[:, sl]
        hs.append(jax.nn.sigmoid(og[:, sl]) * hh)
        m_all = jnp.where(lane == h, m_col, m_all)

    h_ref[...] = jnp.concatenate(hs, axis=-1).reshape(h_ref.shape)
    m_ref[...] = m_all.reshape(m_ref.shape)

    mnew = _dot_exact(lastsel, m_all)
    for h in range(N_HEADS):
        sl = slice(h * HEAD_DIM, (h + 1) * HEAD_DIM)
        last_col = cumlast[:, h:h + 1]
        mnew_col = mnew[:, h:h + 1]
        w_k = jnp.exp(last_col - cum[:, h:h + 1] + gi[:, h:h + 1] - mnew_col)
        decay = jnp.exp(last_col + mprev[:, h:h + 1] - mnew_col)
        kw = kg[:, sl] * w_k
        kwb = kw.astype(BF16)
        vh = vg[:, sl]
        for sq in range(nseq):
            vs = jnp.where(seg_r == sq, vh, 0.0)
            kws = jnp.where(seg_r == sq, kw, 0.0)
            dc = lax.dot_general(kwb, vs.astype(BF16), TN_DIMS, preferred_element_type=F32)
            dn = jnp.sum(kws, axis=0, keepdims=True)
            dec = decay[sq * ls:sq * ls + 1, :]
            c_ref[sq, h] = dec * c0_ref[sq, h] + dc
            n_ref[sq, h:h + 1, :] = dec * n0_ref[sq, h:h + 1, :] + dn


def _mlstm_sample(z3, gates3, mg, c0, n0, m_tok, *, bb):
    bsz, seq, _ = z3.shape
    assert bb * seq == GROUP
    zspec = lambda k: pl.BlockSpec((bb, seq, D_MLSTM), lambda b: (b, 0, k))
    c_spec = pl.BlockSpec((bb, N_HEADS, HEAD_DIM, HEAD_DIM), lambda b: (b, 0, 0, 0))
    n_spec = pl.BlockSpec((bb, N_HEADS, HEAD_DIM), lambda b: (b, 0, 0))
    m_spec = pl.BlockSpec((bb, seq, LANES), lambda b: (b, 0, 0))
    return pl.pallas_call(
        functools.partial(_mlstm_sample_kernel, nseq=bb, ls=seq),
        grid=(bsz // bb,),
        in_specs=[zspec(2), zspec(3), zspec(4), zspec(5),
                  pl.BlockSpec((bb, seq, D_GATE), lambda b: (b, 0, 0)),
                  _const_spec(mg.shape),
                  c_spec, n_spec, m_spec],
        out_specs=[pl.BlockSpec((bb, seq, D_MLSTM), lambda b: (b, 0, 0)), c_spec, n_spec, m_spec],
        out_shape=[
            jax.ShapeDtypeStruct((bsz, seq, D_MLSTM), F32),
            jax.ShapeDtypeStruct((bsz, N_HEADS, HEAD_DIM, HEAD_DIM), F32),
            jax.ShapeDtypeStruct((bsz, N_HEADS, HEAD_DIM), F32),
            jax.ShapeDtypeStruct((bsz, seq, LANES), F32),
        ],
        compiler_params=_params(1),
        name="mlstm_sample",
    )(z3, z3, z3, z3, gates3, mg, c0, n0, m_tok)


def _ffn_b_kernel(x_ref, ada_ref, y_ref, h_ref, wout_ref, g3_ref, wgu_ref, wd_ref, gf_ref, o_ref, hid_ref):
    x = x_ref[...]
    bb, tt, _ = x.shape
    n = bb * tt
    ada = ada_ref[...]
    mix = (_dot(y_ref[...].reshape(n, D_CONV).astype(BF16), wout_ref[:D_CONV, :])
           + _dot(h_ref[...].reshape(n, D_MLSTM).astype(BF16), wout_ref[D_CONV:, :]))
    x2 = x + _ada_row(ada, 5) * mix.reshape(bb, tt, D_MODEL)
    xm = _rms(x2, g3_ref[...]) * (1.0 + _ada_row(ada, 7)) + _ada_row(ada, 6)
    ff = _swiglu(xm.reshape(n, D_MODEL).astype(BF16), wgu_ref, wd_ref, hid_ref)
    x3 = x2 + 0.5 * _ada_row(ada, 8) * ff.reshape(bb, tt, D_MODEL)
    o_ref[...] = _rms(x3, gf_ref[...])


def _ffn_b(x, ada, y, h, p, *, bb):
    bsz, seq, _ = x.shape
    tok = lambda w: pl.BlockSpec((bb, seq, w), lambda b: (b, 0, 0))
    weights = [p["wout"], p["g3"], p["wgu2"], p["wd2"], p["gf"]]
    return pl.pallas_call(
        _ffn_b_kernel,
        grid=(bsz // bb,),
        in_specs=[tok(D_MODEL), pl.BlockSpec((bb, N_ADA * D_MODEL), lambda b: (b, 0)), tok(D_CONV), tok(D_MLSTM)]
        + [_const_spec(w.shape) for w in weights],
        out_specs=tok(D_MODEL),
        out_shape=jax.ShapeDtypeStruct(x.shape, F32),
        scratch_shapes=[pltpu.VMEM((bb * seq, D_FF), BF16)],
        compiler_params=_params(1),
        name="ffn_b",
    )(x, ada, y, h, *weights)


def _prompt_trunk(x, ada, p):
    x1, u, h, c_new, n_new, m_row, wgu1, wd1, win, wg = _front(x, ada, p, tt=512)
    out, conv_new, wgu2, wd2, wout = _back(x1, ada, u, h, p, tt=512, rt=32)
    bf16_weights = dict(wgu1=wgu1, wd1=wd1, win=win, wg=wg, wgu2=wgu2, wd2=wd2, wout=wout)
    return (out, conv_new[None], c_new[None], n_new[None], m_row[None, :, 0, :N_HEADS]), bf16_weights


def _sample_trunk(x, ada, states, p):
    bsz, seq, _ = x.shape
    conv_state, c0, n0, m0 = states
    x1, z, gates = _ffn_a(x, ada, p, bb=32)
    z3 = z.reshape(bsz, seq, D_Z)
    gates3 = gates.reshape(bsz, seq, D_GATE)
    y, conv_new_t = _conv(z3, jnp.transpose(conv_state, (1, 0, 2)), p, bb=32)
    conv_new = jnp.transpose(conv_new_t, (1, 0, 2))
    m_tok = jnp.broadcast_to(
        jnp.pad(m0, ((0, 0), (0, LANES - N_HEADS)))[:, None, :], (bsz, seq, LANES))
    h, c_new, n_new, m_tok_new = _mlstm_sample(z3, gates3, p["mn_g"], c0, n0, m_tok, bb=GROUP // seq)
    out = _ffn_b(x1, ada, y, h, p, bb=64)
    return out, conv_new[None], c_new[None], n_new[None], m_tok_new[None, :, seq - 1, :N_HEADS]


def _gate_cols(a):
    pad = ((0, 0), (0, LANES - N_HEADS))
    return jnp.concatenate([jnp.pad(a[:, :N_HEADS], pad), jnp.pad(a[:, N_HEADS:], pad)], axis=1)


def kernel(x_prompt, x_sample, c_prompt, c_sample, state_conv, state_C, state_n, state_m, w_ada, b_ada, norm_ffn1, ffn1_w_gu, ffn1_w_down, norm_mix, w_in, b_in, conv_w, conv_b, conv_norm_g, conv_norm_b, mlstm_norm_g, w_out, norm_ffn2, ffn2_w_gu, ffn2_w_down, norm_final):
    assert w_ada.shape[0] == 1, "single layer"
    w_in_t = jnp.transpose(w_in[0])
    p = {
        "g1": norm_ffn1, "g2": norm_mix, "g3": norm_ffn2, "gf": norm_final[None],
        "wgu1": ffn1_w_gu[0], "wd1": ffn1_w_down[0], "wgu2": ffn2_w_gu[0], "wd2": ffn2_w_down[0],
        "win": w_in_t, "wout": w_out[0],
        "b_main": b_in,
        "b_gate": _gate_cols(b_in[:, D_Z:]),
        "conv_w": jnp.pad(conv_w[0], ((0, TAIL_PAD - CONV_WIDTH), (0, 0))),
        "conv_b": conv_b, "cn_g": conv_norm_g, "cn_b": conv_norm_b, "mn_g": mlstm_norm_g,
    }
    bs = x_sample.shape[0]
    ada = _ada(jnp.concatenate([c_sample, c_prompt], axis=0), w_ada[0], b_ada)
    (yp, conv_p, c_p, n_p, m_p), bf16_weights = _prompt_trunk(x_prompt, ada[bs:], p)
    ys, conv_s, c_s, n_s, m_s = _sample_trunk(
        x_sample, ada, (state_conv[0], state_C[0], state_n[0], state_m[0]), {**p, **bf16_weights})
    return (yp, ys, conv_p, c_p, n_p, m_p, conv_s, c_s, n_s, m_s)
```
